```python
import math
import jax
import jax.numpy as jnp
from jax import lax
import numpy as np

D_MODEL = 1024
BATCH = 8
SEQ = 2048
DEPTH = 4
DEC_BATCH = 128
DEC_SEQ = 8
PAST_LEN = 16384
PAGE_SIZE = 128

GROUP_W = D_MODEL // 4
MIX_W = 4 * GROUP_W
HEAD_DIM = 64
RW_HEADS = GROUP_W // HEAD_DIM
RW_W_RANK = 64
RW_A_RANK = 64
RW_G_RANK = 128
RW_COLS = 3 * GROUP_W + RW_W_RANK + RW_A_RANK + RW_G_RANK
RW_DECAY_SCALE = 0.606531
RW_GN_EPS = 64e-5
LRU_BLOCKS = 4
LRU_BW = GROUP_W // LRU_BLOCKS
LRU_C = 8.0
CONV_W = 4
S5_GROUP = 16
S5_GROUPS = GROUP_W // S5_GROUP
S5_P = 64
ML_HEADS = GROUP_W // HEAD_DIM
ML_CHUNK = 64
IN_COLS = RW_COLS + 7 * GROUP_W + 2 * ML_HEADS
D_FF = 2752
EPS = 1e-6
F32 = jnp.float32

kernel_name = 'hybrid_rwkv7_rglru_s5_mlstm_step'


def _rmsnorm(x, g):
    xf = x.astype(F32)
    y = xf * lax.rsqrt(jnp.mean(xf * xf, axis=-1, keepdims=True) + EPS)
    return (y * g.astype(F32)).astype(x.dtype)


def _swiglu(x, wg, wu, wd):
    return (jax.nn.silu(x @ wg) * (x @ wu)) @ wd


def _split(z, sizes):
    idx = np.cumsum(np.array(sizes))[:-1].tolist()
    return jnp.split(z, idx, axis=-1)


def _head_ln(o, eps):
    mu = jnp.mean(o, axis=-1, keepdims=True)
    var = jnp.mean(jnp.square(o - mu), axis=-1, keepdims=True)
    y = (o - mu) * lax.rsqrt(var + eps)
    return y.reshape(o.shape[:-2] + (o.shape[-2] * o.shape[-1],))


def _lin_combine(e1, e2):
    a1, b1 = e1
    a2, b2 = e2
    return (a1 * a2, a2 * b1 + b2)


def _cplx_combine(e1, e2):
    a1r, a1i, b1r, b1i = e1
    a2r, a2i, b2r, b2i = e2
    return (a2r * a1r - a2i * a1i,
            a2r * a1i + a2i * a1r,
            a2r * b1r - a2i * b1i + b2r,
            a2r * b1i + a2i * b1r + b2i)


def _rwkv7(p, shift0, S0, W):
    Bn, L, _ = p.shape
    prev = jnp.concatenate([shift0.astype(F32)[:, None, :], p[:, :-1]], axis=1)
    xm = p + (prev - p) * W['rw_mu']
    r, k, v, lw, la, lg = _split(xm, [GROUP_W, GROUP_W, GROUP_W, RW_W_RANK, RW_A_RANK, RW_G_RANK])
    w = jnp.exp(-RW_DECAY_SCALE * jax.nn.sigmoid(W['rw_w0'] + jnp.tanh(lw) @ W['rw_w2']))
    a = jax.nn.sigmoid(W['rw_a0'] + la @ W['rw_a2'])
    g = jax.nn.sigmoid(lg) @ W['rw_g2']

    def hs(t):
        return t.reshape(Bn, L, RW_HEADS, HEAD_DIM)

    kk = hs(k * W['rw_kk'])
    kk = kk * lax.rsqrt(jnp.maximum(jnp.sum(kk * kk, axis=-1, keepdims=True), 1e-12))
    k = k * (1.0 + (a - 1.0) * W['rw_ka'])
    r, w, k, v, a = hs(r), hs(w), hs(k), hs(v), hs(a)

    def step(S, inp):
        r_t, w_t, k_t, v_t, kk_t, a_t = inp
        sa = jnp.einsum('bhij,bhj->bhi', S, -kk_t)
        S = (S * w_t[:, :, None, :] + sa[..., None] * (kk_t * a_t)[:, :, None, :]
             + v_t[..., None] * k_t[:, :, None, :])
        return S, jnp.einsum('bhij,bhj->bhi', S, r_t)

    xs = tuple(jnp.swapaxes(t, 0, 1) for t in (r, w, k, v, kk, a))
    S_last, o = lax.scan(step, S0.astype(F32), xs)
    o = jnp.swapaxes(o, 0, 1)
    y = _head_ln(o, RW_GN_EPS) * W['rw_ln_g'] + W['rw_ln_b']
    bonus = jnp.sum(r * k * W['rw_rk'], axis=-1, keepdims=True) * v
    y = (y + bonus.reshape(Bn, L, GROUP_W)) * g
    return y, p[:, -1], S_last


def _rglru(xb, gb, buf0, h0, W):
    Bn, L, _ = xb.shape
    xp = jnp.concatenate([buf0.astype(F32), xb], axis=1)
    cw = W['lru_conv_w']
    xc = W['lru_conv_b'] + sum(xp[:, j:j + L] * cw[j] for j in range(CONV_W))
    xh = xc.reshape(Bn, L, LRU_BLOCKS, LRU_BW)
    gate_r = jax.nn.sigmoid(jnp.einsum('blnc,ncd->blnd', xh, W['lru_wa']).reshape(Bn, L, GROUP_W) + W['lru_ba'])
    gate_i = jax.nn.sigmoid(jnp.einsum('blnc,ncd->blnd', xh, W['lru_wx']).reshape(Bn, L, GROUP_W) + W['lru_bx'])
    log_a = -LRU_C * gate_r * jax.nn.softplus(-W['lru_lambda'])
    a = jnp.exp(log_a)
    b = jnp.sqrt(-jnp.expm1(2.0 * log_a)) * (gate_i * xc)
    a_cum, h = lax.associative_scan(_lin_combine, (a, b), axis=1)
    h = h + a_cum * h0.astype(F32)[:, None]
    y = h * jax.nn.gelu(gb)
    return y, xp[:, L:], h[:, -1]


def _s5(u, s_re0, s_im0, W):
    Bn, L, _ = u.shape
    ug = u.reshape(Bn, L, S5_GROUPS, S5_GROUP)
    lr = W['s5_a_re'].astype(F32)
    li = W['s5_a_im'].astype(F32)
    dt = jnp.exp(W['s5_log_dt'].astype(F32))[:, None]
    mag = jnp.exp(lr * dt)
    ar, ai = mag * jnp.cos(li * dt), mag * jnp.sin(li * dt)
    den = lr * lr + li * li
    zr = ((ar - 1.0) * lr + ai * li) / den
    zi = (ai * lr - (ar - 1.0) * li) / den
    b_re, b_im = W['s5_b_re'].astype(F32), W['s5_b_im'].astype(F32)
    bbr = zr[..., None] * b_re - zi[..., None] * b_im
    bbi = zr[..., None] * b_im + zi[..., None] * b_re
    bu_r = jnp.einsum('blgh,gph->blgp', ug, bbr)
    bu_i = jnp.einsum('blgh,gph->blgp', ug, bbi)
    shp = bu_r.shape
    acr, aci, xr, xi = lax.associative_scan(
        _cplx_combine, (jnp.broadcast_to(ar, shp), jnp.broadcast_to(ai, shp), bu_r, bu_i), axis=1)
    s0r = s_re0.astype(F32)[:, None]
    s0i = s_im0.astype(F32)[:, None]
    xr, xi = xr + acr * s0r - aci * s0i, xi + acr * s0i + aci * s0r
    y = (jnp.einsum('ghp,blgp->blgh', W['s5_c_re'], xr)
         - jnp.einsum('ghp,blgp->blgh', W['s5_c_im'], xi)).reshape(Bn, L, GROUP_W)
    y = jax.nn.gelu(y + W['s5_d'] * u)
    y = y * jax.nn.sigmoid(y @ W['s5_glu_w'] + W['s5_glu_b'])
    return y, xr[:, -1], xi[:, -1]


def _mlstm(q, k, v, og, ig, fg, C0, n0, m0, W):
    Bn, L, _ = q.shape
    c = math.gcd(L, ML_CHUNK)
    nc = L // c

    def hc(t):
        return t.reshape(Bn, nc, c, ML_HEADS, HEAD_DIM).transpose(1, 0, 3, 2, 4)

    def gc(t):
        return t.reshape(Bn, nc, c, ML_HEADS).transpose(1, 0, 3, 2)

    qh, kh, vh = hc(q), hc(k) * (HEAD_DIM ** -0.5), hc(v)
    log_i = gc(ig + W['ml_bi'])
    log_f = gc(jax.nn.log_sigmoid(fg + W['ml_bf']))
    causal = jnp.tril(jnp.ones((c, c), dtype=bool))

    def step(carry, inp):
        C, n, m = carry
        qc, kc, vc, li, lf = inp
        b = jnp.cumsum(lf, axis=-1)
        dmat = jnp.where(causal, b[..., :, None] - b[..., None, :] + li[..., None, :], -jnp.inf)
        inter = b + m[..., None]
        mt = jnp.maximum(jnp.max(dmat, axis=-1), inter)
        pm = jnp.exp(dmat - mt[..., None]) * jnp.einsum('bhtd,bhsd->bhts', qc, kc)
        ei = jnp.exp(inter - mt)
        num = jnp.einsum('bhts,bhsd->bhtd', pm, vc) + ei[..., None] * jnp.einsum('bhij,bhtj->bhti', C, qc)
        den = jnp.sum(pm, axis=-1) + ei * jnp.einsum('bhj,bhtj->bht', n, qc)
        h = num / jnp.maximum(jnp.abs(den), jnp.exp(-mt))[..., None]
        b_last = b[..., -1]
        wk = b_last[..., None] - b + li
        m_new = jnp.maximum(b_last + m, jnp.max(wk, axis=-1))
        sc = jnp.exp(wk - m_new[..., None])
        decay = jnp.exp(b_last + m - m_new)
        C_new = decay[..., None, None] * C + jnp.einsum('bhs,bhsi,bhsj->bhij', sc, vc, kc)
        n_new = decay[..., None] * n + jnp.einsum('bhs,bhsj->bhj', sc, kc)
        return (C_new, n_new, m_new), h

    (C, n, m), h = lax.scan(step, (C0.astype(F32), n0.astype(F32), m0.astype(F32)),
                            (qh, kh, vh, log_i, log_f))
    h = h.transpose(1, 0, 3, 2, 4).reshape(Bn, L, ML_HEADS, HEAD_DIM)
    y = _head_ln(h, EPS) * W['ml_ln_g'] * jax.nn.sigmoid(og)
    return y, C, n, m


def _mixer(h, st, W):
    rw_shift, rw_S, lru_buf, lru_h, s5_re, s5_im, ml_C, ml_n, ml_m = st
    z = (h @ W['w_in']).astype(F32)
    p_rw, lx, lg, s5u, q, k, v, og, ig, fg = _split(
        z, [RW_COLS] + [GROUP_W] * 7 + [ML_HEADS, ML_HEADS])
    y_rw, n_shift, n_rwS = _rwkv7(p_rw, rw_shift, rw_S, W)
    y_lru, n_buf, n_h = _rglru(lx, lg, lru_buf, lru_h, W)
    y_s5, n_re, n_im = _s5(s5u, s5_re, s5_im, W)
    y_ml, n_C, n_n, n_m = _mlstm(q, k, v, og, ig, fg, ml_C, ml_n, ml_m, W)
    cat = jnp.concatenate([y_rw, _rmsnorm(y_lru, W['lru_norm']), _rmsnorm(y_s5, W['s5_norm']), y_ml], axis=-1)
    out = (cat @ W['w_out']).astype(h.dtype)
    new = (n_shift, n_rwS, n_buf, n_h, n_re, n_im, n_C, n_n, n_m)
    return out, tuple(nw.astype(s.dtype) for nw, s in zip(new, st))


def _layer(x, st, W):
    x = x + 0.5 * _rmsnorm(_swiglu(_rmsnorm(x, W['ffn1_pre']), W['ffn1_wg'], W['ffn1_wu'], W['ffn1_wd']), W['ffn1_post'])
    m, st = _mixer(_rmsnorm(x, W['mix_pre']), st, W)
    x = x + _rmsnorm(m, W['mix_post'])
    x = x + 0.5 * _rmsnorm(_swiglu(_rmsnorm(x, W['ffn2_pre']), W['ffn2_wg'], W['ffn2_wu'], W['ffn2_wd']), W['ffn2_post'])
    return x, st


def _zero_state(n, dtype):
    return (jnp.zeros((n, RW_COLS), dtype),
            jnp.zeros((n, RW_HEADS, HEAD_DIM, HEAD_DIM), dtype),
            jnp.zeros((n, CONV_W - 1, GROUP_W), dtype),
            jnp.zeros((n, GROUP_W), dtype),
            jnp.zeros((n, S5_GROUPS, S5_P), dtype),
            jnp.zeros((n, S5_GROUPS, S5_P), dtype),
            jnp.zeros((n, ML_HEADS, HEAD_DIM, HEAD_DIM), dtype),
            jnp.zeros((n, ML_HEADS, HEAD_DIM), dtype),
            jnp.zeros((n, ML_HEADS), dtype))


def setup_inputs(seed: int = 0) -> dict:
    key = jax.random.key(seed)
    keys = iter(jax.random.split(key, 96))

    def nrm(shape, scale):
        return jax.random.normal(next(keys), shape, F32) * scale

    def gain(shape):
        return 1.0 + nrm(shape, 0.02)

    def unif(shape, lo, hi):
        return jax.random.uniform(next(keys), shape, F32, lo, hi)

    L, G, D, FF = DEPTH, GROUP_W, D_MODEL, D_FF
    a8 = unif((L, G), 0.9, 0.999)
    a_lru = a8 ** (1.0 / LRU_C)
    return {
        'x_prompt': nrm((BATCH, SEQ, D), 1.0),
        'x_sample': nrm((DEC_BATCH, DEC_SEQ, D), 1.0),
        'state_rwkv_shift': nrm((L, DEC_BATCH, RW_COLS), 1.0),
        'state_rwkv_wkv': nrm((L, DEC_BATCH, RW_HEADS, HEAD_DIM, HEAD_DIM), 0.3),
        'state_lru_conv': nrm((L, DEC_BATCH, CONV_W - 1, G), 1.0),
        'state_lru_h': nrm((L, DEC_BATCH, G), 0.5),
        'state_s5_re': nrm((L, DEC_BATCH, S5_GROUPS, S5_P), 0.5),
        'state_s5_im': nrm((L, DEC_BATCH, S5_GROUPS, S5_P), 0.5),
        'state_mlstm_C': nrm((L, DEC_BATCH, ML_HEADS, HEAD_DIM, HEAD_DIM), 0.3),
        'state_mlstm_n': nrm((L, DEC_BATCH, ML_HEADS, HEAD_DIM), 0.3),
        'state_mlstm_m': nrm((L, DEC_BATCH, ML_HEADS), 1.0),
        'ffn1_pre': gain((L, D)),
        'ffn1_wg': nrm((L, D, FF), D ** -0.5),
        'ffn1_wu': nrm((L, D, FF), D ** -0.5),
        'ffn1_wd': nrm((L, FF, D), FF ** -0.5),
        'ffn1_post': gain((L, D)),
        'mix_pre': gain((L, D)),
        'w_in': nrm((L, D, IN_COLS), D ** -0.5),
        'w_out': nrm((L, MIX_W, D), MIX_W ** -0.5),
        'mix_post': gain((L, D)),
        'ffn2_pre': gain((L, D)),
        'ffn2_wg': nrm((L, D, FF), D ** -0.5),
        'ffn2_wu': nrm((L, D, FF), D ** -0.5),
        'ffn2_wd': nrm((L, FF, D), FF ** -0.5),
        'ffn2_post': gain((L, D)),
        'rw_mu': unif((L, RW_COLS), 0.0, 1.0),
        'rw_w0': nrm((L, G), 1.0),
        'rw_w2': nrm((L, RW_W_RANK, G), 0.1),
        'rw_a0': nrm((L, G), 0.5),
        'rw_a2': nrm((L, RW_A_RANK, G), 0.1),
        'rw_g2': nrm((L, RW_G_RANK, G), RW_G_RANK ** -0.5),
        'rw_kk': 0.85 + nrm((L, G), 0.02),
        'rw_ka': 1.0 + nrm((L, G), 0.02),
        'rw_rk': nrm((L, RW_HEADS, HEAD_DIM), 0.1),
        'rw_ln_g': gain((L, G)),
        'rw_ln_b': nrm((L, G), 0.02),
        'lru_conv_w': nrm((L, CONV_W, G), CONV_W ** -0.5),
        'lru_conv_b': nrm((L, G), 0.02),
        'lru_wa': nrm((L, LRU_BLOCKS, LRU_BW, LRU_BW), LRU_BW ** -0.5),
        'lru_ba': nrm((L, G), 0.02),
        'lru_wx': nrm((L, LRU_BLOCKS, LRU_BW, LRU_BW), LRU_BW ** -0.5),
        'lru_bx': nrm((L, G), 0.02),
        'lru_lambda': jnp.log(a_lru) - jnp.log1p(-a_lru),
        'lru_norm': gain((L, G)),
        's5_a_re': -0.5 + nrm((L, S5_GROUPS, S5_P), 0.01),
        's5_a_im': jnp.pi * jnp.arange(S5_P, dtype=F32) + nrm((L, S5_GROUPS, S5_P), 0.01),
        's5_log_dt': unif((L, S5_GROUPS), math.log(0.001), math.log(0.1)),
        's5_b_re': nrm((L, S5_GROUPS, S5_P, S5_GROUP), (2 * S5_GROUP) ** -0.5),
        's5_b_im': nrm((L, S5_GROUPS, S5_P, S5_GROUP), (2 * S5_GROUP) ** -0.5),
        's5_c_re': nrm((L, S5_GROUPS, S5_GROUP, S5_P), (2 * S5_P) ** -0.5),
        's5_c_im': nrm((L, S5_GROUPS, S5_GROUP, S5_P), (2 * S5_P) ** -0.5),
        's5_d': nrm((L, G), 0.5),
        's5_glu_w': nrm((L, G, G), G ** -0.5),
        's5_glu_b': nrm((L, G), 0.02),
        's5_norm': gain((L, G)),
        'ml_bi': nrm((L, ML_HEADS), 0.1),
        'ml_bf': jnp.linspace(3.0, 6.0, ML_HEADS, dtype=F32) + nrm((L, ML_HEADS), 0.1),
        'ml_ln_g': gain((L, G)),
    }


def reference(x_prompt, x_sample, state_rwkv_shift, state_rwkv_wkv, state_lru_conv, state_lru_h,
              state_s5_re, state_s5_im, state_mlstm_C, state_mlstm_n, state_mlstm_m,
              ffn1_pre, ffn1_wg, ffn1_wu, ffn1_wd, ffn1_post, mix_pre, w_in, w_out, mix_post,
              ffn2_pre, ffn2_wg, ffn2_wu, ffn2_wd, ffn2_post,
              rw_mu, rw_w0, rw_w2, rw_a0, rw_a2, rw_g2, rw_kk, rw_ka, rw_rk, rw_ln_g, rw_ln_b,
              lru_conv_w, lru_conv_b, lru_wa, lru_ba, lru_wx, lru_bx, lru_lambda, lru_norm,
              s5_a_re, s5_a_im, s5_log_dt, s5_b_re, s5_b_im, s5_c_re, s5_c_im, s5_d, s5_glu_w, s5_glu_b, s5_norm,
              ml_bi, ml_bf, ml_ln_g):
    W = {
        'ffn1_pre': ffn1_pre, 'ffn1_wg': ffn1_wg, 'ffn1_wu': ffn1_wu, 'ffn1_wd': ffn1_wd, 'ffn1_post': ffn1_post,
        'mix_pre': mix_pre, 'w_in': w_in, 'w_out': w_out, 'mix_post': mix_post,
        'ffn2_pre': ffn2_pre, 'ffn2_wg': ffn2_wg, 'ffn2_wu': ffn2_wu, 'ffn2_wd': ffn2_wd, 'ffn2_post': ffn2_post,
        'rw_mu': rw_mu, 'rw_w0': rw_w0, 'rw_w2': rw_w2, 'rw_a0': rw_a0, 'rw_a2': rw_a2, 'rw_g2': rw_g2,
        'rw_kk': rw_kk, 'rw_ka': rw_ka, 'rw_rk': rw_rk, 'rw_ln_g': rw_ln_g, 'rw_ln_b': rw_ln_b,
        'lru_conv_w': lru_conv_w, 'lru_conv_b': lru_conv_b, 'lru_wa': lru_wa, 'lru_ba': lru_ba,
        'lru_wx': lru_wx, 'lru_bx': lru_bx, 'lru_lambda': lru_lambda, 'lru_norm': lru_norm,
        's5_a_re': s5_a_re, 's5_a_im': s5_a_im, 's5_log_dt': s5_log_dt, 's5_b_re': s5_b_re, 's5_b_im': s5_b_im,
        's5_c_re': s5_c_re, 's5_c_im': s5_c_im, 's5_d': s5_d, 's5_glu_w': s5_glu_w, 's5_glu_b': s5_glu_b,
        's5_norm': s5_norm,
        'ml_bi': ml_bi, 'ml_bf': ml_bf, 'ml_ln_g': ml_ln_g,
    }
    sample_state = (state_rwkv_shift, state_rwkv_wkv, state_lru_conv, state_lru_h,
                    state_s5_re, state_s5_im, state_mlstm_C, state_mlstm_n, state_mlstm_m)
    y_prompt, y_sample = x_prompt, x_sample
    outs_p, outs_s = [], []
    for l in range(DEPTH):
        Wl = {name: arr[l] for name, arr in W.items()}
        y_prompt, st_p = _layer(y_prompt, _zero_state(x_prompt.shape[0], x_prompt.dtype), Wl)
        y_sample, st_s = _layer(y_sample, tuple(s[l] for s in sample_state), Wl)
        outs_p.append(st_p)
        outs_s.append(st_s)
    P = [jnp.stack([o[i] for o in outs_p]) for i in range(9)]
    S = [jnp.stack([o[i] for o in outs_s]) for i in range(9)]
    p_rwkv_shift, p_rwkv_wkv, p_lru_conv, p_lru_h, p_s5_re, p_s5_im, p_mlstm_C, p_mlstm_n, p_mlstm_m = P
    s_rwkv_shift, s_rwkv_wkv, s_lru_conv, s_lru_h, s_s5_re, s_s5_im, s_mlstm_C, s_mlstm_n, s_mlstm_m = S
    return (y_prompt, y_sample,
            p_rwkv_shift, p_rwkv_wkv, p_lru_conv, p_lru_h, p_s5_re, p_s5_im, p_mlstm_C, p_mlstm_n, p_mlstm_m,
            s_rwkv_shift, s_rwkv_wkv, s_lru_conv, s_lru_h, s_s5_re, s_s5_im, s_mlstm_C, s_mlstm_n, s_mlstm_m)
```

```python
import functools
import math

import jax
import jax.numpy as jnp
from jax import lax
from jax.experimental import pallas as pl
from jax.experimental.pallas import tpu as pltpu

F32 = jnp.float32
BF16 = jnp.bfloat16

LANES = 128
SUBLANES = 8
VMEM_LIMIT_BYTES = 56 * 1024 * 1024

D_MODEL = 1024
GROUP_W = D_MODEL // 4
HEAD_DIM = 64
HEADS = GROUP_W // HEAD_DIM
RW_W_RANK = 64
RW_A_RANK = 64
RW_G_RANK = 128
RW_COLS = 3 * GROUP_W + RW_W_RANK + RW_A_RANK + RW_G_RANK
RW_DECAY_SCALE = 0.606531
RW_GN_EPS = 64e-5
LRU_BLOCKS = 4
LRU_C = 8.0
CONV_W = 4
S5_GROUP = 16
S5_GROUPS = GROUP_W // S5_GROUP
S5_P = 64
S5_STATE = S5_GROUPS * S5_P
IN_COLS = RW_COLS + 7 * GROUP_W + 2 * HEADS
IN_COLS_PAD = -(-IN_COLS // LANES) * LANES
GATE_COL_BLOCK = (RW_COLS + 7 * GROUP_W) // LANES
D_FF = 2752
D_FF_PAD = -(-D_FF // LANES) * LANES
EPS = 1e-6
HI = lax.Precision.HIGHEST


def _dot(a, b):
    return jnp.dot(a.astype(BF16), b.astype(BF16), preferred_element_type=F32)


def _dot_nt(a, b, precision=None):
    return lax.dot_general(a, b, (((1,), (1,)), ((), ())), precision=precision,
                           preferred_element_type=F32)


def _dot_tn(a, b, precision=None):
    return lax.dot_general(a, b, (((0,), (0,)), ((), ())), precision=precision,
                           preferred_element_type=F32)


def _dot_hi(a, b):
    return jnp.dot(a, b, precision=HI, preferred_element_type=F32)


def _rms(x, g):
    return x * lax.rsqrt(jnp.mean(x * x, axis=-1, keepdims=True) + EPS) * g


def _lower_tri(n, strict):
    r = lax.broadcasted_iota(jnp.int32, (n, n), 0)
    c = lax.broadcasted_iota(jnp.int32, (n, n), 1)
    return (r > c) if strict else (r >= c)


def _const_spec(shape):
    nd = len(shape)
    return pl.BlockSpec(shape, lambda *_: (0,) * nd, pipeline_mode=pl.Buffered(1))


def _params(sem):
    return pltpu.CompilerParams(dimension_semantics=sem, vmem_limit_bytes=VMEM_LIMIT_BYTES)


def _swiglu_half_step(x, pre, wg, wu, wd, post):
    h = _rms(x, pre).astype(BF16)
    g = jnp.dot(h, wg, preferred_element_type=F32)
    u = jnp.dot(h, wu, preferred_element_type=F32)
    act = (jax.nn.silu(g) * u).astype(BF16)
    f = jnp.dot(act, wd, preferred_element_type=F32)
    return x + 0.5 * _rms(f, post)


def _ffn_in_kernel(x_ref, pre_ref, wg_ref, wu_ref, wd_ref, post_ref, mpre_ref, win_ref,
                   xo_ref, z_ref):
    x1 = _swiglu_half_step(x_ref[...], pre_ref[...], wg_ref[...], wu_ref[...], wd_ref[...],
                           post_ref[...])
    xo_ref[...] = x1
    z_ref[...] = jnp.dot(_rms(x1, mpre_ref[...]).astype(BF16), win_ref[...],
                         preferred_element_type=F32)


def _out_ffn_kernel(x_ref, yrw_ref, ylru_ref, ys5_ref, yml_ref, wout_ref, mpost_ref,
                    pre_ref, wg_ref, wu_ref, wd_ref, post_ref, xo_ref):
    cat = jnp.concatenate([yrw_ref[...], ylru_ref[...], ys5_ref[...], yml_ref[...]], axis=-1)
    m = jnp.dot(cat.astype(BF16), wout_ref[...], preferred_element_type=F32)
    x2 = x_ref[...] + _rms(m, mpost_ref[...])
    xo_ref[...] = _swiglu_half_step(x2, pre_ref[...], wg_ref[...], wu_ref[...], wd_ref[...],
                                    post_ref[...])


def _token_tile(n):
    for tm in (512, 256, 128, 64, 32, 16, 8):
        if n % tm == 0:
            return tm
    raise ValueError(f"token count {n} is not a multiple of {SUBLANES}")


def _ffn_in_call(x, pre, wg, wu, wd, post, mpre, win):
    n = x.shape[0]
    tm = _token_tile(n)
    row = lambda w: pl.BlockSpec((tm, w), lambda i: (i, 0))
    return pl.pallas_call(
        _ffn_in_kernel,
        grid=(n // tm,),
        in_specs=[row(D_MODEL), _const_spec((1, D_MODEL)), _const_spec((D_MODEL, D_FF_PAD)),
                  _const_spec((D_MODEL, D_FF_PAD)), _const_spec((D_FF_PAD, D_MODEL)),
                  _const_spec((1, D_MODEL)), _const_spec((1, D_MODEL)),
                  _const_spec((D_MODEL, IN_COLS_PAD))],
        out_specs=[row(D_MODEL), row(IN_COLS_PAD)],
        out_shape=[jax.ShapeDtypeStruct((n, D_MODEL), F32),
                   jax.ShapeDtypeStruct((n, IN_COLS_PAD), F32)],
        compiler_params=_params(("parallel",)),
        name="ffn1_inproj",
    )(x, pre, wg, wu, wd, post, mpre, win)


def _out_ffn_call(x, yrw, ylru, ys5, yml, wout, mpost, pre, wg, wu, wd, post):
    n = x.shape[0]
    tm = _token_tile(n)
    row = lambda w: pl.BlockSpec((tm, w), lambda i: (i, 0))
    return pl.pallas_call(
        _out_ffn_kernel,
        grid=(n // tm,),
        in_specs=[row(D_MODEL), row(GROUP_W), row(GROUP_W), row(GROUP_W), row(GROUP_W),
                  _const_spec((D_MODEL, D_MODEL)), _const_spec((1, D_MODEL)),
                  _const_spec((1, D_MODEL)), _const_spec((D_MODEL, D_FF_PAD)),
                  _const_spec((D_MODEL, D_FF_PAD)), _const_spec((D_FF_PAD, D_MODEL)),
                  _const_spec((1, D_MODEL))],
        out_specs=row(D_MODEL),
        out_shape=jax.ShapeDtypeStruct((n, D_MODEL), F32),
        compiler_params=_params(("parallel",)),
        name="outproj_ffn2",
    )(x, yrw, ylru, ys5, yml, wout, mpost, pre, wg, wu, wd, post)


def _rwkv_kernel(p_ref, shift0_ref, s0_ref, mu_ref, w0_ref, w2_ref, a0_ref, a2_ref, g2_ref,
                 kkw_ref, ka_ref, rk_ref, lng_ref, lnb_ref, hsum_ref,
                 y_ref, shift_ref, s_ref, pbuf, s_scr, *, tc):
    c = pl.program_id(1)

    @pl.when(c == 0)
    def _():
        pbuf[SUBLANES - 1:SUBLANES, :] = shift0_ref[...]
        s_scr[...] = s0_ref[...]

    p = p_ref[...]
    pbuf[SUBLANES:SUBLANES + tc, :] = p
    prev = pbuf[SUBLANES - 1:SUBLANES - 1 + tc, :]
    pbuf[SUBLANES - 1:SUBLANES, :] = p[tc - 1:tc, :]
    shift_ref[...] = p[tc - 1:tc, :]

    xm = p + (prev - p) * mu_ref[...]
    g = GROUP_W
    r, k, v = xm[:, 0:g], xm[:, g:2 * g], xm[:, 2 * g:3 * g]
    lw = xm[:, 3 * g:3 * g + RW_W_RANK]
    la = xm[:, 3 * g + RW_W_RANK:3 * g + RW_W_RANK + RW_A_RANK]
    lg = xm[:, 3 * g + RW_W_RANK + RW_A_RANK:]
    logw = -RW_DECAY_SCALE * jax.nn.sigmoid(w0_ref[...] + _dot(jnp.tanh(lw), w2_ref[...]))
    a = jax.nn.sigmoid(a0_ref[...] + _dot(la, a2_ref[...]))
    gate = _dot(jax.nn.sigmoid(lg), g2_ref[...])

    hsum = hsum_ref[...]
    kk = k * kkw_ref[...]
    kk = kk * lax.rsqrt(jnp.maximum(_dot_hi(kk * kk, hsum), 1e-12))
    k2 = k * (1.0 + (a - 1.0) * ka_ref[...])

    cum = _dot_hi(_lower_tri(tc, False).astype(F32), logw)
    total = cum[tc - 1:tc, :]
    at = -kk * jnp.exp(cum - logw)
    inv = jnp.exp(-cum)
    bt = kk * a * inv
    kt = k2 * inv
    rt = r * jnp.exp(cum)
    rest = jnp.exp(total - cum)
    bw = kk * a * rest
    kw = k2 * rest
    wtot = jnp.exp(total)

    strict = _lower_tri(tc, True)
    incl = _lower_tri(tc, False)
    n_double = max(1, int(math.ceil(math.log2(tc))))
    outs = []
    for h in range(HEADS):
        sl = slice(h * HEAD_DIM, (h + 1) * HEAD_DIM)
        ah, bh, kh, rh, vh = at[:, sl], bt[:, sl], kt[:, sl], rt[:, sl], v[:, sl]
        s_h = s_scr[h]
        m_ab = jnp.where(strict, _dot_nt(ah, bh, HI), 0.0)
        m_ak = jnp.where(strict, _dot_nt(ah, kh, HI), 0.0)
        m_rb = jnp.where(incl, _dot_nt(rh, bh, HI), 0.0)
        m_rk = jnp.where(incl, _dot_nt(rh, kh, HI), 0.0)
        u = _dot_nt(ah, s_h, HI) + _dot_hi(m_ak, vh)
        pw = m_ab
        for i in range(n_double):
            u = u + _dot_hi(pw, u)
            if i + 1 < n_double:
                pw = _dot_hi(pw, pw)
        outs.append(_dot_nt(rh, s_h, HI) + _dot_hi(m_rb, u) + _dot_hi(m_rk, vh))
        s_scr[h] = (s_h * wtot[:, sl] + _dot_tn(u, bw[:, sl], HI) + _dot_tn(vh, kw[:, sl], HI))
    o = jnp.concatenate(outs, axis=-1)

    mean = _dot_hi(o, hsum) * (1.0 / HEAD_DIM)
    cen = o - mean
    var = _dot_hi(cen * cen, hsum) * (1.0 / HEAD_DIM)
    y = cen * lax.rsqrt(var + RW_GN_EPS) * lng_ref[...] + lnb_ref[...]
    bonus = _dot_hi(r * k2 * rk_ref[...], hsum) * v
    y_ref[...] = (y + bonus) * gate
    s_ref[...] = s_scr[...]


def _rwkv_call(z, shift0, s0, w, tc):
    b, l, _ = z.shape
    vec = lambda n: _const_spec((1, n))
    return pl.pallas_call(
        functools.partial(_rwkv_kernel, tc=tc),
        grid=(b, l // tc),
        in_specs=[pl.BlockSpec((None, tc, RW_COLS), lambda i, c: (i, c, 0)),
                  pl.BlockSpec((None, 1, RW_COLS), lambda i, c: (i, 0, 0)),
                  pl.BlockSpec((None, HEADS, HEAD_DIM, HEAD_DIM), lambda i, c: (i, 0, 0, 0)),
                  vec(RW_COLS), vec(GROUP_W), _const_spec((RW_W_RANK, GROUP_W)), vec(GROUP_W),
                  _const_spec((RW_A_RANK, GROUP_W)), _const_spec((RW_G_RANK, GROUP_W)),
                  vec(GROUP_W), vec(GROUP_W), vec(GROUP_W), vec(GROUP_W), vec(GROUP_W),
                  _const_spec((GROUP_W, GROUP_W))],
        out_specs=[pl.BlockSpec((None, tc, GROUP_W), lambda i, c: (i, c, 0)),
                   pl.BlockSpec((None, 1, RW_COLS), lambda i, c: (i, 0, 0)),
                   pl.BlockSpec((None, HEADS, HEAD_DIM, HEAD_DIM), lambda i, c: (i, 0, 0, 0))],
        out_shape=[jax.ShapeDtypeStruct((b, l, GROUP_W), F32),
                   jax.ShapeDtypeStruct((b, 1, RW_COLS), F32),
                   jax.ShapeDtypeStruct((b, HEADS, HEAD_DIM, HEAD_DIM), F32)],
        scratch_shapes=[pltpu.VMEM((SUBLANES + tc, RW_COLS), F32),
                        pltpu.VMEM((HEADS, HEAD_DIM, HEAD_DIM), F32)],
        compiler_params=_params(("parallel", "arbitrary")),
        name="rwkv7",
    )(z, shift0, s0, w["rw_mu"], w["rw_w0"], w["rw_w2"], w["rw_a0"], w["rw_a2"], w["rw_g2"],
      w["rw_kk"], w["rw_ka"], w["rw_rk"], w["rw_ln_g"], w["rw_ln_b"], w["head_sum"])


def _shifted(buf, x, d, pad, tc):
    buf[pad:pad + tc, :] = x
    return buf[pad - d:pad - d + tc, :]


def _lru_kernel(x_ref, gate_ref, buf0_ref, h0_ref, cw_ref, cb_ref, wax_ref, bax_ref, lam_ref,
                norm_ref, y_ref, buf_ref, h_ref, xbuf, abuf, bbuf, h_scr, *, tc, pad):
    c = pl.program_id(1)

    @pl.when(c == 0)
    def _():
        xbuf[SUBLANES - (CONV_W - 1):SUBLANES, :] = buf0_ref[...]
        h_scr[...] = h0_ref[...]
        abuf[0:pad, :] = jnp.ones((pad, GROUP_W), F32)
        bbuf[0:pad, :] = jnp.zeros((pad, GROUP_W), F32)

    x = x_ref[...]
    xbuf[SUBLANES:SUBLANES + tc, :] = x
    cw = cw_ref[...]
    xc = cb_ref[...]
    for j in range(CONV_W):
        lo = SUBLANES - (CONV_W - 1) + j
        xc = xc + xbuf[lo:lo + tc, :] * cw[j:j + 1, :]
    tail = x[tc - (CONV_W - 1):tc, :]
    xbuf[SUBLANES - (CONV_W - 1):SUBLANES, :] = tail
    buf_ref[...] = tail

    gates = jax.nn.sigmoid(_dot(xc, wax_ref[...]) + bax_ref[...])
    gate_r, gate_i = gates[:, :GROUP_W], gates[:, GROUP_W:]
    log_a = -LRU_C * gate_r * jax.nn.softplus(-lam_ref[...])
    a = jnp.exp(log_a)
    bb = jnp.sqrt(-jnp.tanh(log_a) * (a * a + 1.0)) * (gate_i * xc)

    d = 1
    while d < tc:
        a_sh = _shifted(abuf, a, d, pad, tc)
        b_sh = _shifted(bbuf, bb, d, pad, tc)
        bb = a * b_sh + bb
        a = a * a_sh
        d *= 2
    h = bb + a * h_scr[...]
    h_last = h[tc - 1:tc, :]
    h_scr[...] = h_last
    h_ref[...] = h_last
    y_ref[...] = _rms(h * jax.nn.gelu(gate_ref[...]), norm_ref[...])


def _lru_call(z, buf0, h0, w, tc):
    b, l, _ = z.shape
    pad = max(SUBLANES, tc // 2)
    vec = lambda n: _const_spec((1, n))
    cb = RW_COLS // GROUP_W
    return pl.pallas_call(
        functools.partial(_lru_kernel, tc=tc, pad=pad),
        grid=(b, l // tc),
        in_specs=[pl.BlockSpec((None, tc, GROUP_W), lambda i, c: (i, c, cb)),
                  pl.BlockSpec((None, tc, GROUP_W), lambda i, c: (i, c, cb + 1)),
                  pl.BlockSpec((None, CONV_W - 1, GROUP_W), lambda i, c: (i, 0, 0)),
                  pl.BlockSpec((None, 1, GROUP_W), lambda i, c: (i, 0, 0)),
                  _const_spec((CONV_W, GROUP_W)), vec(GROUP_W),
                  _const_spec((GROUP_W, 2 * GROUP_W)), vec(2 * GROUP_W), vec(GROUP_W),
                  vec(GROUP_W)],
        out_specs=[pl.BlockSpec((None, tc, GROUP_W), lambda i, c: (i, c, 0)),
                   pl.BlockSpec((None, CONV_W - 1, GROUP_W), lambda i, c: (i, 0, 0)),
                   pl.BlockSpec((None, 1, GROUP_W), lambda i, c: (i, 0, 0))],
        out_shape=[jax.ShapeDtypeStruct((b, l, GROUP_W), F32),
                   jax.ShapeDtypeStruct((b, CONV_W - 1, GROUP_W), F32),
                   jax.ShapeDtypeStruct((b, 1, GROUP_W), F32)],
        scratch_shapes=[pltpu.VMEM((SUBLANES + tc, GROUP_W), F32),
                        pltpu.VMEM((pad + tc, GROUP_W), F32),
                        pltpu.VMEM((pad + tc, GROUP_W), F32),
                        pltpu.VMEM((1, GROUP_W), F32)],
        compiler_params=_params(("parallel", "arbitrary")),
        name="rglru",
    )(z, z, buf0, h0, w["lru_conv_w"], w["lru_conv_b"], w["lru_wax"], w["lru_bax"],
      w["lru_lambda"], w["lru_norm"])


def _s5_kernel(u_ref, re0_ref, im0_ref, ar_ref, ai_ref, bb_ref, cc_ref, d_ref, gw_ref, gb_ref,
               norm_ref, y_ref, re_ref, im_ref, rbuf, ibuf, re_scr, im_scr, *, tc, pad):
    c = pl.program_id(1)

    @pl.when(c == 0)
    def _():
        re_scr[...] = re0_ref[...]
        im_scr[...] = im0_ref[...]
        rbuf[0:pad, :] = jnp.zeros((pad, S5_STATE), F32)
        ibuf[0:pad, :] = jnp.zeros((pad, S5_STATE), F32)

    u = u_ref[...]
    bu = _dot(u, bb_ref[...])
    xr, xi = bu[:, :S5_STATE], bu[:, S5_STATE:]
    pr, pi = ar_ref[...], ai_ref[...]
    s_r, s_i = re_scr[...], im_scr[...]
    first = lax.broadcasted_iota(jnp.int32, (tc, 1), 0) == 0
    xr = xr + jnp.where(first, pr * s_r - pi * s_i, 0.0)
    xi = xi + jnp.where(first, pr * s_i + pi * s_r, 0.0)

    d = 1
    while d < tc:
        r_sh = _shifted(rbuf, xr, d, pad, tc)
        i_sh = _shifted(ibuf, xi, d, pad, tc)
        xr = xr + pr * r_sh - pi * i_sh
        xi = xi + pr * i_sh + pi * r_sh
        pr, pi = pr * pr - pi * pi, 2.0 * pr * pi
        d *= 2
    re_last, im_last = xr[tc - 1:tc, :], xi[tc - 1:tc, :]
    re_scr[...] = re_last
    im_scr[...] = im_last
    re_ref[...] = re_last
    im_ref[...] = im_last

    y = _dot(jnp.concatenate([xr, xi], axis=-1), cc_ref[...])
    y = jax.nn.gelu(y + d_ref[...] * u)
    y = y * jax.nn.sigmoid(_dot(y, gw_ref[...]) + gb_ref[...])
    y_ref[...] = _rms(y, norm_ref[...])


def _s5_call(z, re0, im0, w, tc):
    b, l, _ = z.shape
    pad = max(SUBLANES, tc // 2)
    vec = lambda n: _const_spec((1, n))
    cb = (RW_COLS + 2 * GROUP_W) // GROUP_W
    state = pl.BlockSpec((None, 1, S5_STATE), lambda i, c: (i, 0, 0))
    return pl.pallas_call(
        functools.partial(_s5_kernel, tc=tc, pad=pad),
        grid=(b, l // tc),
        in_specs=[pl.BlockSpec((None, tc, GROUP_W), lambda i, c: (i, c, cb)), state, state,
                  vec(S5_STATE), vec(S5_STATE), _const_spec((GROUP_W, 2 * S5_STATE)),
                  _const_spec((2 * S5_STATE, GROUP_W)), vec(GROUP_W),
                  _const_spec((GROUP_W, GROUP_W)), vec(GROUP_W), vec(GROUP_W)],
        out_specs=[pl.BlockSpec((None, tc, GROUP_W), lambda i, c: (i, c, 0)), state, state],
        out_shape=[jax.ShapeDtypeStruct((b, l, GROUP_W), F32),
                   jax.ShapeDtypeStruct((b, 1, S5_STATE), F32),
                   jax.ShapeDtypeStruct((b, 1, S5_STATE), F32)],
        scratch_shapes=[pltpu.VMEM((pad + tc, S5_STATE), F32),
                        pltpu.VMEM((pad + tc, S5_STATE), F32),
                        pltpu.VMEM((1, S5_STATE), F32), pltpu.VMEM((1, S5_STATE), F32)],
        compiler_params=_params(("parallel", "arbitrary")),
        name="s5",
    )(z, re0, im0, w["s5_ar"], w["s5_ai"], w["s5_bb"], w["s5_cc"], w["s5_d"], w["s5_glu_w"],
      w["s5_glu_b"], w["s5_norm"])


def _mlstm_kernel(q_ref, k_ref, v_ref, og_ref, gate_ref, c0_ref, n0_ref, m0_ref, gbias_ref,
                  lng_ref, y_ref, c_ref, n_ref, m_ref, c_scr, n_scr, m_scr, *, tc):
    c = pl.program_id(1)

    @pl.when(c == 0)
    def _():
        c_scr[...] = c0_ref[...]
        n_scr[...] = n0_ref[...]
        m_scr[...] = m0_ref[...]

    lane = lax.broadcasted_iota(jnp.int32, (1, LANES), 1)
    gates = gate_ref[...] + gbias_ref[...]
    log_f = jax.nn.log_sigmoid(gates)
    bcum = _dot_hi(_lower_tri(tc, False).astype(F32), log_f)
    cols = jnp.where(lane < HEADS, gates, bcum)
    eye = (lax.broadcasted_iota(jnp.int32, (SUBLANES, LANES), 0)
           == lax.broadcasted_iota(jnp.int32, (SUBLANES, LANES), 1)).astype(F32)
    rows = _dot_nt(eye, cols, HI)
    m_vec = m_scr[...]
    inter_all = bcum + m_vec
    b_last_all = bcum[tc - 1:tc, :]

    causal = _lower_tri(tc, False)
    q, k, v = q_ref[...], k_ref[...] * (HEAD_DIM ** -0.5), v_ref[...]
    outs = []
    m_new_vec = m_vec
    for h in range(HEADS):
        sl = slice(h * HEAD_DIM, (h + 1) * HEAD_DIM)
        fl = slice(HEADS + h, HEADS + h + 1)
        qh, kh, vh = q[:, sl], k[:, sl], v[:, sl]
        b_col, li_col = bcum[:, fl], cols[:, h:h + 1]
        b_row, li_row = rows[HEADS + h:HEADS + h + 1, :], rows[h:h + 1, :]
        inter = inter_all[:, fl]
        dmat = jnp.where(causal, b_col - b_row + li_row, -jnp.inf)
        mt = jnp.maximum(jnp.max(dmat, axis=-1, keepdims=True), inter)
        pm = jnp.exp(dmat - mt) * _dot_nt(qh.astype(BF16), kh.astype(BF16))
        ei = jnp.exp(inter - mt)
        c_h, n_h = c_scr[h], n_scr[h:h + 1, :]
        num = _dot(pm, vh) + ei * _dot_nt(qh.astype(BF16), c_h.astype(BF16))
        den = jnp.sum(pm, axis=-1, keepdims=True) + ei * jnp.sum(qh * n_h, axis=-1, keepdims=True)
        hh = num / jnp.maximum(jnp.abs(den), jnp.exp(-mt))
        mu = jnp.mean(hh, axis=-1, keepdims=True)
        cen = hh - mu
        var = jnp.mean(cen * cen, axis=-1, keepdims=True)
        outs.append(cen * lax.rsqrt(var + EPS))

        b_last, m_h = b_last_all[:, fl], m_vec[:, fl]
        wk = b_last - b_col + li_col
        m_new = jnp.maximum(b_last + m_h, jnp.max(wk, axis=0, keepdims=True))
        sc = jnp.exp(wk - m_new)
        decay = jnp.exp(b_last + m_h - m_new)
        c_scr[h] = decay * c_h + _dot_tn((sc * vh).astype(BF16), kh.astype(BF16))
        n_scr[h:h + 1, :] = decay * n_h + jnp.sum(sc * kh, axis=0, keepdims=True)
        m_new_vec = jnp.where(lane == HEADS + h, m_new, m_new_vec)
    m_scr[...] = m_new_vec
    y_ref[...] = jnp.concatenate(outs, axis=-1) * lng_ref[...] * jax.nn.sigmoid(og_ref[...])
    c_ref[...] = c_scr[...]
    n_ref[...] = n_scr[...]
    m_ref[...] = m_new_vec


def _mlstm_call(z, c0, n0, m0, w, tc):
    b, l, _ = z.shape
    cb = (RW_COLS + 3 * GROUP_W) // GROUP_W
    col = lambda j: pl.BlockSpec((None, tc, GROUP_W), lambda i, c: (i, c, cb + j))
    cst = pl.BlockSpec((None, HEADS, HEAD_DIM, HEAD_DIM), lambda i, c: (i, 0, 0, 0))
    nst = pl.BlockSpec((None, HEADS, HEAD_DIM), lambda i, c: (i, 0, 0))
    mst = pl.BlockSpec((None, 1, LANES), lambda i, c: (i, 0, 0))
    return pl.pallas_call(
        functools.partial(_mlstm_kernel, tc=tc),
        grid=(b, l // tc),
        in_specs=[col(0), col(1), col(2), col(3),
                  pl.BlockSpec((None, tc, LANES), lambda i, c: (i, c, GATE_COL_BLOCK)),
                  cst, nst, mst, _const_spec((1, LANES)), _const_spec((1, GROUP_W))],
        out_specs=[pl.BlockSpec((None, tc, GROUP_W), lambda i, c: (i, c, 0)), cst, nst, mst],
        out_shape=[jax.ShapeDtypeStruct((b, l, GROUP_W), F32),
                   jax.ShapeDtypeStruct((b, HEADS, HEAD_DIM, HEAD_DIM), F32),
                   jax.ShapeDtypeStruct((b, HEADS, HEAD_DIM), F32),
                   jax.ShapeDtypeStruct((b, 1, LANES), F32)],
        scratch_shapes=[pltpu.VMEM((HEADS, HEAD_DIM, HEAD_DIM), F32),
                        pltpu.VMEM((HEADS, HEAD_DIM), F32), pltpu.VMEM((1, LANES), F32)],
        compiler_params=_params(("parallel", "arbitrary")),
        name="mlstm",
    )(z, z, z, z, z, c0, n0, m0, w["ml_gate_bias"], w["ml_ln_g"])


def _block_diag(blocks):
    n, r, c = blocks.shape
    eye = jnp.eye(n, dtype=blocks.dtype)
    return (eye[:, None, :, None] * blocks[:, :, None, :]).reshape(n * r, n * c)


def _prep_layer(wl):
    row = lambda t: t.reshape(1, -1).astype(F32)
    pad_ff = D_FF_PAD - D_FF
    out = {}
    for f in ("ffn1", "ffn2"):
        out[f + "_pre"] = row(wl[f + "_pre"])
        out[f + "_post"] = row(wl[f + "_post"])
        out[f + "_wg"] = jnp.pad(wl[f + "_wg"], ((0, 0), (0, pad_ff))).astype(BF16)
        out[f + "_wu"] = jnp.pad(wl[f + "_wu"], ((0, 0), (0, pad_ff))).astype(BF16)
        out[f + "_wd"] = jnp.pad(wl[f + "_wd"], ((0, pad_ff), (0, 0))).astype(BF16)
    out["mix_pre"] = row(wl["mix_pre"])
    out["mix_post"] = row(wl["mix_post"])
    out["w_in"] = jnp.pad(wl["w_in"], ((0, 0), (0, IN_COLS_PAD - IN_COLS))).astype(BF16)
    out["w_out"] = wl["w_out"].astype(BF16)

    for name in ("rw_mu", "rw_w0", "rw_a0", "rw_kk", "rw_ka", "rw_rk", "rw_ln_g", "rw_ln_b"):
        out[name] = row(wl[name])
    for name in ("rw_w2", "rw_a2", "rw_g2"):
        out[name] = wl[name].astype(BF16)
    out["head_sum"] = _block_diag(jnp.ones((HEADS, HEAD_DIM, HEAD_DIM), F32))

    out["lru_conv_w"] = wl["lru_conv_w"].astype(F32)
    out["lru_conv_b"] = row(wl["lru_conv_b"])
    out["lru_wax"] = jnp.concatenate(
        [_block_diag(wl["lru_wa"]), _block_diag(wl["lru_wx"])], axis=1).astype(BF16)
    out["lru_bax"] = jnp.concatenate([row(wl["lru_ba"]), row(wl["lru_bx"])], axis=1)
    out["lru_lambda"] = row(wl["lru_lambda"])
    out["lru_norm"] = row(wl["lru_norm"])

    lr, li = wl["s5_a_re"].astype(F32), wl["s5_a_im"].astype(F32)
    dt = jnp.exp(wl["s5_log_dt"].astype(F32))[:, None]
    mag = jnp.exp(lr * dt)
    ar, ai = mag * jnp.cos(li * dt), mag * jnp.sin(li * dt)
    den = lr * lr + li * li
    zr = ((ar - 1.0) * lr + ai * li) / den
    zi = (ai * lr - (ar - 1.0) * li) / den
    b_re, b_im = wl["s5_b_re"].astype(F32), wl["s5_b_im"].astype(F32)
    bbr = zr[..., None] * b_re - zi[..., None] * b_im
    bbi = zr[..., None] * b_im + zi[..., None] * b_re
    to_in = lambda t: _block_diag(jnp.swapaxes(t, 1, 2))
    out["s5_ar"], out["s5_ai"] = row(ar), row(ai)
    out["s5_bb"] = jnp.concatenate([to_in(bbr), to_in(bbi)], axis=1).astype(BF16)
    to_out = lambda t: _block_diag(jnp.swapaxes(t, 1, 2))
    out["s5_cc"] = jnp.concatenate(
        [to_out(wl["s5_c_re"]), -to_out(wl["s5_c_im"])], axis=0).astype(BF16)
    out["s5_d"] = row(wl["s5_d"])
    out["s5_glu_w"] = wl["s5_glu_w"].astype(BF16)
    out["s5_glu_b"] = row(wl["s5_glu_b"])
    out["s5_norm"] = row(wl["s5_norm"])

    gate_bias = jnp.concatenate([wl["ml_bi"], wl["ml_bf"]]).astype(F32)
    out["ml_gate_bias"] = jnp.pad(gate_bias, (0, LANES - 2 * HEADS)).reshape(1, LANES)
    out["ml_ln_g"] = row(wl["ml_ln_g"])
    return out


def _chunk(l, cap):
    c = cap
    while c > SUBLANES and l % c:
        c //= 2
    if l % c:
        raise ValueError(f"sequence length {l} is not a multiple of {SUBLANES}")
    return c


def _layer(x, st, w):
    b, l, d = x.shape
    n = b * l
    shift0, s0, buf0, h0, re0, im0, c0, n0, m0 = st
    x1, z = _ffn_in_call(x.reshape(n, d), w["ffn1_pre"], w["ffn1_wg"], w["ffn1_wu"],
                         w["ffn1_wd"], w["ffn1_post"], w["mix_pre"], w["w_in"])
    z = z.reshape(b, l, IN_COLS_PAD)

    y_rw, n_shift, n_s = _rwkv_call(z, shift0.reshape(b, 1, RW_COLS), s0, w, _chunk(l, 64))
    y_lru, n_buf, n_h = _lru_call(z, buf0, h0.reshape(b, 1, GROUP_W), w, _chunk(l, 256))
    y_s5, n_re, n_im = _s5_call(z, re0.reshape(b, 1, S5_STATE), im0.reshape(b, 1, S5_STATE), w,
                                _chunk(l, 128))
    m0p = jnp.pad(m0, ((0, 0), (HEADS, LANES - 2 * HEADS))).reshape(b, 1, LANES)
    y_ml, n_c, n_n, n_m = _mlstm_call(z, c0, n0, m0p, w, _chunk(l, 128))

    flat = lambda t: t.reshape(n, GROUP_W)
    x3 = _out_ffn_call(x1, flat(y_rw), flat(y_lru), flat(y_s5), flat(y_ml), w["w_out"],
                       w["mix_post"], w["ffn2_pre"], w["ffn2_wg"], w["ffn2_wu"], w["ffn2_wd"],
                       w["ffn2_post"])
    new = (n_shift.reshape(b, RW_COLS), n_s, n_buf, n_h.reshape(b, GROUP_W),
           n_re.reshape(b, S5_GROUPS, S5_P), n_im.reshape(b, S5_GROUPS, S5_P), n_c, n_n,
           n_m.reshape(b, LANES)[:, HEADS:2 * HEADS])
    return x3.reshape(b, l, d), new


def _zero_state(n):
    return (jnp.zeros((n, RW_COLS), F32), jnp.zeros((n, HEADS, HEAD_DIM, HEAD_DIM), F32),
            jnp.zeros((n, CONV_W - 1, GROUP_W), F32), jnp.zeros((n, GROUP_W), F32),
            jnp.zeros((n, S5_GROUPS, S5_P), F32), jnp.zeros((n, S5_GROUPS, S5_P), F32),
            jnp.zeros((n, HEADS, HEAD_DIM, HEAD_DIM), F32), jnp.zeros((n, HEADS, HEAD_DIM), F32),
            jnp.zeros((n, HEADS), F32))


def kernel(x_prompt, x_sample, state_rwkv_shift, state_rwkv_wkv, state_lru_conv, state_lru_h, state_s5_re, state_s5_im, state_mlstm_C, state_mlstm_n, state_mlstm_m, ffn1_pre, ffn1_wg, ffn1_wu, ffn1_wd, ffn1_post, mix_pre, w_in, w_out, mix_post, ffn2_pre, ffn2_wg, ffn2_wu, ffn2_wd, ffn2_post, rw_mu, rw_w0, rw_w2, rw_a0, rw_a2, rw_g2, rw_kk, rw_ka, rw_rk, rw_ln_g, rw_ln_b, lru_conv_w, lru_conv_b, lru_wa, lru_ba, lru_wx, lru_bx, lru_lambda, lru_norm, s5_a_re, s5_a_im, s5_log_dt, s5_b_re, s5_b_im, s5_c_re, s5_c_im, s5_d, s5_glu_w, s5_glu_b, s5_norm, ml_bi, ml_bf, ml_ln_g):
    weights = dict(
        ffn1_pre=ffn1_pre, ffn1_wg=ffn1_wg, ffn1_wu=ffn1_wu, ffn1_wd=ffn1_wd, ffn1_post=ffn1_post,
        mix_pre=mix_pre, w_in=w_in, w_out=w_out, mix_post=mix_post,
        ffn2_pre=ffn2_pre, ffn2_wg=ffn2_wg, ffn2_wu=ffn2_wu, ffn2_wd=ffn2_wd, ffn2_post=ffn2_post,
        rw_mu=rw_mu, rw_w0=rw_w0, rw_w2=rw_w2, rw_a0=rw_a0, rw_a2=rw_a2, rw_g2=rw_g2,
        rw_kk=rw_kk, rw_ka=rw_ka, rw_rk=rw_rk, rw_ln_g=rw_ln_g, rw_ln_b=rw_ln_b,
        lru_conv_w=lru_conv_w, lru_conv_b=lru_conv_b, lru_wa=lru_wa, lru_ba=lru_ba,
        lru_wx=lru_wx, lru_bx=lru_bx, lru_lambda=lru_lambda, lru_norm=lru_norm,
        s5_a_re=s5_a_re, s5_a_im=s5_a_im, s5_log_dt=s5_log_dt, s5_b_re=s5_b_re, s5_b_im=s5_b_im,
        s5_c_re=s5_c_re, s5_c_im=s5_c_im, s5_d=s5_d, s5_glu_w=s5_glu_w, s5_glu_b=s5_glu_b,
        s5_norm=s5_norm, ml_bi=ml_bi, ml_bf=ml_bf, ml_ln_g=ml_ln_g)
    sample_state = (state_rwkv_shift, state_rwkv_wkv, state_lru_conv, state_lru_h,
                    state_s5_re, state_s5_im, state_mlstm_C, state_mlstm_n, state_mlstm_m)
    depth = ffn1_pre.shape[0]
    y_p, y_s = x_prompt.astype(F32), x_sample.astype(F32)
    outs_p, outs_s = [], []
    for layer in range(depth):
        w = _prep_layer({name: arr[layer] for name, arr in weights.items()})
        y_p, st_p = _layer(y_p, _zero_state(x_prompt.shape[0]), w)
        y_s, st_s = _layer(y_s, tuple(s[layer].astype(F32) for s in sample_state), w)
        outs_p.append(st_p)
        outs_s.append(st_s)
    p_states = [jnp.stack([o[i] for o in outs_p]) for i in range(9)]
    s_states = [jnp.stack([o[i] for o in outs_s]) for i in range(9)]
    return (y_p, y_s, *p_states, *s_states)
```

```python
import functools

import jax
import jax.numpy as jnp
from jax import lax
from jax.experimental import pallas as pl
from jax.experimental.pallas import tpu as pltpu

F32 = jnp.float32
BF16 = jnp.bfloat16

LANES = 128
SUBLANES = 8
VMEM_LIMIT_BYTES = 56 * 1024 * 1024

D_MODEL = 1024
GROUP_W = D_MODEL // 4
HEAD_DIM = 64
HEADS = GROUP_W // HEAD_DIM
RW_W_RANK = 64
RW_A_RANK = 64
RW_G_RANK = 128
RW_COLS = 3 * GROUP_W + RW_W_RANK + RW_A_RANK + RW_G_RANK
RW_DECAY_SCALE = 0.606531
RW_GN_EPS = 64e-5
LRU_C = 8.0
CONV_W = 4
S5_GROUP = 16
S5_GROUPS = GROUP_W // S5_GROUP
S5_P = 64
S5_STATE = S5_GROUPS * S5_P
MAIN_COLS = RW_COLS + 7 * GROUP_W
IN_COLS = MAIN_COLS + 2 * HEADS
IN_COLS_PAD = MAIN_COLS + 2 * LANES
IGATE_COL_BLOCK = MAIN_COLS // LANES
D_FF = 2752
D_FF_PAD = -(-D_FF // LANES) * LANES
EPS = 1e-6
HI = lax.Precision.HIGHEST


def _dot(a, b):
    return jnp.dot(a.astype(BF16), b.astype(BF16), preferred_element_type=F32)


def _dot_nt(a, b, precision=None):
    return lax.dot_general(a, b, (((1,), (1,)), ((), ())), precision=precision,
                           preferred_element_type=F32)


def _dot_tn(a, b):
    return lax.dot_general(a, b, (((0,), (0,)), ((), ())), preferred_element_type=F32)


def _dot_hi(a, b):
    return jnp.dot(a, b, precision=HI, preferred_element_type=F32)


def _split_dot(x, ones_like):
    hi = x.astype(BF16)
    lo = (x - hi.astype(F32)).astype(BF16)
    return (jnp.dot(hi, ones_like, preferred_element_type=F32)
            + jnp.dot(lo, ones_like, preferred_element_type=F32))


def _rms(x, g):
    return x * lax.rsqrt(jnp.mean(x * x, axis=-1, keepdims=True) + EPS) * g


def _seq_masks(bt, tc, reps=1):
    rows = bt * tc
    r = lax.broadcasted_iota(jnp.int32, (rows, reps * rows), 0)
    c = lax.broadcasted_iota(jnp.int32, (rows, reps * rows), 1) & (rows - 1)
    if bt == 1:
        return r > c, r >= c
    start = r - (r & (tc - 1))
    return (c < r) & (c >= start), (c <= r) & (c >= start)


def _rows(x3, tc):
    bt, _, w = x3.shape
    return jnp.broadcast_to(x3, (bt, tc, w)).reshape(bt * tc, w)


def _row_in_seq(bt, tc):
    return lax.broadcasted_iota(jnp.int32, (bt * tc, 1), 0) & (tc - 1)


def _const_spec(shape):
    nd = len(shape)
    return pl.BlockSpec(shape, lambda *_: (0,) * nd, pipeline_mode=pl.Buffered(1))


def _params(sem):
    return pltpu.CompilerParams(dimension_semantics=sem, vmem_limit_bytes=VMEM_LIMIT_BYTES)


def _swiglu_half_step(x, pre, wg, wu, wd, post):
    h = _rms(x, pre).astype(BF16)
    g = jnp.dot(h, wg, preferred_element_type=F32)
    u = jnp.dot(h, wu, preferred_element_type=F32)
    act = (jax.nn.silu(g) * u).astype(BF16)
    f = jnp.dot(act, wd, preferred_element_type=F32)
    return x + 0.5 * _rms(f, post)


def _ffn_in_kernel(x_ref, pre_ref, wg_ref, wu_ref, wd_ref, post_ref, mpre_ref, win_ref,
                   xo_ref, z_ref):
    x1 = _swiglu_half_step(x_ref[...], pre_ref[...], wg_ref[...], wu_ref[...], wd_ref[...],
                           post_ref[...])
    xo_ref[...] = x1
    z_ref[...] = jnp.dot(_rms(x1, mpre_ref[...]).astype(BF16), win_ref[...],
                         preferred_element_type=F32)


def _out_ffn_kernel(x_ref, yrw_ref, ylru_ref, ys5_ref, yml_ref, wout_ref, mpost_ref,
                    pre_ref, wg_ref, wu_ref, wd_ref, post_ref, xo_ref):
    cat = jnp.concatenate([yrw_ref[...], ylru_ref[...], ys5_ref[...], yml_ref[...]], axis=-1)
    m = jnp.dot(cat.astype(BF16), wout_ref[...], preferred_element_type=F32)
    x2 = x_ref[...] + _rms(m, mpost_ref[...])
    xo_ref[...] = _swiglu_half_step(x2, pre_ref[...], wg_ref[...], wu_ref[...], wd_ref[...],
                                    post_ref[...])


def _token_tile(n):
    for tm in (512, 256, 128, 64, 32, 16, 8):
        if n % tm == 0:
            return tm
    raise ValueError(f"token count {n} is not a multiple of {SUBLANES}")


def _ffn_in_call(x, pre, wg, wu, wd, post, mpre, win):
    n = x.shape[0]
    tm = _token_tile(n)
    row = lambda w: pl.BlockSpec((tm, w), lambda i: (i, 0))
    return pl.pallas_call(
        _ffn_in_kernel,
        grid=(n // tm,),
        in_specs=[row(D_MODEL), _const_spec((1, D_MODEL)), _const_spec((D_MODEL, D_FF_PAD)),
                  _const_spec((D_MODEL, D_FF_PAD)), _const_spec((D_FF_PAD, D_MODEL)),
                  _const_spec((1, D_MODEL)), _const_spec((1, D_MODEL)),
                  _const_spec((D_MODEL, IN_COLS_PAD))],
        out_specs=[row(D_MODEL), row(IN_COLS_PAD)],
        out_shape=[jax.ShapeDtypeStruct((n, D_MODEL), F32),
                   jax.ShapeDtypeStruct((n, IN_COLS_PAD), F32)],
        compiler_params=_params(("parallel",)),
        name="ffn1_inproj",
    )(x, pre, wg, wu, wd, post, mpre, win)


def _out_ffn_call(x, yrw, ylru, ys5, yml, wout, mpost, pre, wg, wu, wd, post):
    n = x.shape[0]
    tm = _token_tile(n)
    row = lambda w: pl.BlockSpec((tm, w), lambda i: (i, 0))
    return pl.pallas_call(
        _out_ffn_kernel,
        grid=(n // tm,),
        in_specs=[row(D_MODEL), row(GROUP_W), row(GROUP_W), row(GROUP_W), row(GROUP_W),
                  _const_spec((D_MODEL, D_MODEL)), _const_spec((1, D_MODEL)),
                  _const_spec((1, D_MODEL)), _const_spec((D_MODEL, D_FF_PAD)),
                  _const_spec((D_MODEL, D_FF_PAD)), _const_spec((D_FF_PAD, D_MODEL)),
                  _const_spec((1, D_MODEL))],
        out_specs=row(D_MODEL),
        out_shape=jax.ShapeDtypeStruct((n, D_MODEL), F32),
        compiler_params=_params(("parallel",)),
        name="outproj_ffn2",
    )(x, yrw, ylru, ys5, yml, wout, mpost, pre, wg, wu, wd, post)


def _seq_spec(bt, tc, w, col_block):
    return pl.BlockSpec((bt, tc, w), lambda i, c: (i, c, col_block))


def _state_spec(bt, *dims):
    zeros = (0,) * len(dims)
    return pl.BlockSpec((bt,) + dims, lambda i, c: (i,) + zeros)


def _rwkv_kernel(p_ref, shift0_ref, s0_ref, mu_ref, w0_ref, w2_ref, a0_ref, a2_ref, g2_ref,
                 kkw_ref, ka_ref, rk_ref, lng_ref, lnb_ref, hsum_ref,
                 y_ref, shift_ref, s_ref, pbuf, s_scr, *, bt, tc):
    c = pl.program_id(1)
    rows = bt * tc
    lo = SUBLANES - 1

    @pl.when(c == 0)
    def _():
        pbuf[:, lo:SUBLANES, :] = shift0_ref[...]
        s_scr[...] = s0_ref[...]

    p3 = p_ref[...]
    pbuf[:, SUBLANES:SUBLANES + tc, :] = p3
    prev = pbuf[:, lo:lo + tc, :].reshape(rows, RW_COLS)
    last = p3[:, tc - 1:tc, :]
    pbuf[:, lo:SUBLANES, :] = last
    shift_ref[...] = last
    p = p3.reshape(rows, RW_COLS)

    xm = p + (prev - p) * mu_ref[...]
    g = GROUP_W
    r, k, v = xm[:, 0:g], xm[:, g:2 * g], xm[:, 2 * g:3 * g]
    lw = xm[:, 3 * g:3 * g + RW_W_RANK]
    la = xm[:, 3 * g + RW_W_RANK:3 * g + RW_W_RANK + RW_A_RANK]
    lg = xm[:, 3 * g + RW_W_RANK + RW_A_RANK:]
    logw = -RW_DECAY_SCALE * jax.nn.sigmoid(w0_ref[...] + _dot(jnp.tanh(lw), w2_ref[...]))
    a = jax.nn.sigmoid(a0_ref[...] + _dot(la, a2_ref[...]))
    gate = _dot(jax.nn.sigmoid(lg), g2_ref[...])

    hsum = hsum_ref[...]
    kk = k * kkw_ref[...]
    kk = kk * lax.rsqrt(jnp.maximum(_split_dot(kk * kk, hsum), 1e-12))
    k2 = k * (1.0 + (a - 1.0) * ka_ref[...])

    _, incl = _seq_masks(bt, tc)
    strict2, incl2 = _seq_masks(bt, tc, reps=2)
    cum = _dot_hi(incl.astype(F32), logw)
    cum3 = cum.reshape(bt, tc, g)
    total3 = cum3[:, tc - 1:tc, :]
    rest = jnp.exp(total3 - cum3).reshape(rows, g)
    wtot3 = jnp.exp(total3)
    inv = jnp.exp(-cum)
    kka = kk * a
    at = -kk * jnp.exp(cum - logw)
    rt = r * jnp.exp(cum)
    bk_t = kka * inv, k2 * inv
    bk_w = kka * rest, k2 * rest

    n_double = max(1, (tc - 1).bit_length())
    outs = []
    for h in range(HEADS):
        sl = slice(h * HEAD_DIM, (h + 1) * HEAD_DIM)
        ah, rh, vh = at[:, sl], rt[:, sl], v[:, sl]
        bkh = jnp.concatenate([bk_t[0][:, sl], bk_t[1][:, sl]], axis=0).astype(BF16)
        bkw = jnp.concatenate([bk_w[0][:, sl], bk_w[1][:, sl]], axis=0)
        m_a = jnp.where(strict2, _dot_nt(ah.astype(BF16), bkh), 0.0)
        m_r = jnp.where(incl2, _dot_nt(rh.astype(BF16), bkh), 0.0)
        s_old = [s_scr[b, h] for b in range(bt)]
        from_state = []
        for b in range(bt):
            rs = slice(b * tc, (b + 1) * tc)
            ar = jnp.concatenate([ah[rs], rh[rs]], axis=0)
            from_state.append(_dot_nt(ar.astype(BF16), s_old[b].astype(BF16)))
        u = (jnp.concatenate([fs[:tc] for fs in from_state], axis=0)
             + _dot(m_a[:, rows:], vh))
        pw = m_a[:, :rows]
        for i in range(n_double):
            u = u + _dot(pw, u)
            if i + 1 < n_double:
                pw = _dot(pw, pw)
        uv = jnp.concatenate([u, vh], axis=0)
        outs.append(jnp.concatenate([fs[tc:] for fs in from_state], axis=0) + _dot(m_r, uv))
        for b in range(bt):
            rs = slice(b * tc, (b + 1) * tc)
            uv_b = jnp.concatenate([uv[rs], uv[rows + b * tc:rows + (b + 1) * tc]], axis=0)
            bkw_b = jnp.concatenate([bkw[rs], bkw[rows + b * tc:rows + (b + 1) * tc]], axis=0)
            s_scr[b, h] = (s_old[b] * wtot3[b, :, sl]
                           + _dot_tn(uv_b.astype(BF16), bkw_b.astype(BF16)))
    o = jnp.concatenate(outs, axis=-1)

    mean = _split_dot(o, hsum) * (1.0 / HEAD_DIM)
    cen = o - mean
    var = _split_dot(cen * cen, hsum) * (1.0 / HEAD_DIM)
    y = cen * lax.rsqrt(var + RW_GN_EPS) * lng_ref[...] + lnb_ref[...]
    bonus = _split_dot(r * k2 * rk_ref[...], hsum) * v
    y_ref[...] = ((y + bonus) * gate).reshape(bt, tc, g)
    s_ref[...] = s_scr[...]


def _rwkv_call(z, shift0, s0, w, bt, tc):
    b, l, _ = z.shape
    vec = lambda n: _const_spec((1, n))
    return pl.pallas_call(
        functools.partial(_rwkv_kernel, bt=bt, tc=tc),
        grid=(b // bt, l // tc),
        in_specs=[_seq_spec(bt, tc, RW_COLS, 0), _state_spec(bt, 1, RW_COLS),
                  _state_spec(bt, HEADS, HEAD_DIM, HEAD_DIM),
                  vec(RW_COLS), vec(GROUP_W), _const_spec((RW_W_RANK, GROUP_W)), vec(GROUP_W),
                  _const_spec((RW_A_RANK, GROUP_W)), _const_spec((RW_G_RANK, GROUP_W)),
                  vec(GROUP_W), vec(GROUP_W), vec(GROUP_W), vec(GROUP_W), vec(GROUP_W),
                  _const_spec((GROUP_W, GROUP_W))],
        out_specs=[_seq_spec(bt, tc, GROUP_W, 0), _state_spec(bt, 1, RW_COLS),
                   _state_spec(bt, HEADS, HEAD_DIM, HEAD_DIM)],
        out_shape=[jax.ShapeDtypeStruct((b, l, GROUP_W), F32),
                   jax.ShapeDtypeStruct((b, 1, RW_COLS), F32),
                   jax.ShapeDtypeStruct((b, HEADS, HEAD_DIM, HEAD_DIM), F32)],
        scratch_shapes=[pltpu.VMEM((bt, SUBLANES + tc, RW_COLS), F32),
                        pltpu.VMEM((bt, HEADS, HEAD_DIM, HEAD_DIM), F32)],
        compiler_params=_params(("parallel", "arbitrary")),
        name="rwkv7",
    )(z, shift0, s0, w["rw_mu"], w["rw_w0"], w["rw_w2"], w["rw_a0"], w["rw_a2"], w["rw_g2"],
      w["rw_kk"], w["rw_ka"], w["rw_rk"], w["rw_ln_g"], w["rw_ln_b"], w["head_sum"])


def _shifted(buf, x, d, pad, rows):
    buf[pad:pad + rows, :] = x
    return buf[pad - d:pad - d + rows, :]


def _lru_kernel(x_ref, gate_ref, buf0_ref, h0_ref, cw_ref, cb_ref, wax_ref, bax_ref, lam_ref,
                norm_ref, y_ref, buf_ref, h_ref, xbuf, abuf, bbuf, h_scr, *, bt, tc, pad):
    c = pl.program_id(1)
    rows = bt * tc
    lo = SUBLANES - (CONV_W - 1)

    @pl.when(c == 0)
    def _():
        xbuf[:, lo:SUBLANES, :] = buf0_ref[...]
        h_scr[...] = h0_ref[...]
        abuf[0:pad, :] = jnp.ones((pad, GROUP_W), F32)
        bbuf[0:pad, :] = jnp.zeros((pad, GROUP_W), F32)

    x3 = x_ref[...]
    xbuf[:, SUBLANES:SUBLANES + tc, :] = x3
    cw = cw_ref[...]
    xc3 = cb_ref[...]
    for j in range(CONV_W):
        xc3 = xc3 + xbuf[:, lo + j:lo + j + tc, :] * cw[j:j + 1, :]
    tail = x3[:, tc - (CONV_W - 1):tc, :]
    xbuf[:, lo:SUBLANES, :] = tail
    buf_ref[...] = tail
    xc = xc3.reshape(rows, GROUP_W)

    gates = jax.nn.sigmoid(_dot(xc, wax_ref[...]) + bax_ref[...])
    gate_r, gate_i = gates[:, :GROUP_W], gates[:, GROUP_W:]
    log_a = -LRU_C * gate_r * jax.nn.softplus(-lam_ref[...])
    a = jnp.exp(log_a)
    bb = jnp.sqrt(-jnp.tanh(log_a) * (a * a + 1.0)) * (gate_i * xc)

    rin = _row_in_seq(bt, tc)
    d = 1
    while d < tc:
        a_sh = jnp.where(rin >= d, _shifted(abuf, a, d, pad, rows), 1.0)
        b_sh = jnp.where(rin >= d, _shifted(bbuf, bb, d, pad, rows), 0.0)
        bb = a * b_sh + bb
        a = a * a_sh
        d *= 2
    h = bb + a * _rows(h_scr[...], tc)
    h_last = h.reshape(bt, tc, GROUP_W)[:, tc - 1:tc, :]
    h_scr[...] = h_last
    h_ref[...] = h_last
    gate = gate_ref[...].reshape(rows, GROUP_W)
    y_ref[...] = _rms(h * jax.nn.gelu(gate), norm_ref[...]).reshape(bt, tc, GROUP_W)


def _lru_call(z, buf0, h0, w, bt, tc):
    b, l, _ = z.shape
    pad = max(SUBLANES, tc // 2)
    rows = bt * tc
    vec = lambda n: _const_spec((1, n))
    cb = RW_COLS // GROUP_W
    return pl.pallas_call(
        functools.partial(_lru_kernel, bt=bt, tc=tc, pad=pad),
        grid=(b // bt, l // tc),
        in_specs=[_seq_spec(bt, tc, GROUP_W, cb), _seq_spec(bt, tc, GROUP_W, cb + 1),
                  _state_spec(bt, CONV_W - 1, GROUP_W), _state_spec(bt, 1, GROUP_W),
                  _const_spec((CONV_W, GROUP_W)), vec(GROUP_W),
                  _const_spec((GROUP_W, 2 * GROUP_W)), vec(2 * GROUP_W), vec(GROUP_W),
                  vec(GROUP_W)],
        out_specs=[_seq_spec(bt, tc, GROUP_W, 0), _state_spec(bt, CONV_W - 1, GROUP_W),
                   _state_spec(bt, 1, GROUP_W)],
        out_shape=[jax.ShapeDtypeStruct((b, l, GROUP_W), F32),
                   jax.ShapeDtypeStruct((b, CONV_W - 1, GROUP_W), F32),
                   jax.ShapeDtypeStruct((b, 1, GROUP_W), F32)],
        scratch_shapes=[pltpu.VMEM((bt, SUBLANES + tc, GROUP_W), F32),
                        pltpu.VMEM((pad + rows, GROUP_W), F32),
                        pltpu.VMEM((pad + rows, GROUP_W), F32),
                        pltpu.VMEM((bt, 1, GROUP_W), F32)],
        compiler_params=_params(("parallel", "arbitrary")),
        name="rglru",
    )(z, z, buf0, h0, w["lru_conv_w"], w["lru_conv_b"], w["lru_wax"], w["lru_bax"],
      w["lru_lambda"], w["lru_norm"])


def _s5_kernel(u_ref, re0_ref, im0_ref, ar_ref, ai_ref, bb_ref, cc_ref, d_ref, gw_ref, gb_ref,
               norm_ref, y_ref, re_ref, im_ref, rbuf, ibuf, re_scr, im_scr, *, bt, tc, pad):
    c = pl.program_id(1)
    rows = bt * tc

    @pl.when(c == 0)
    def _():
        re_scr[...] = re0_ref[...]
        im_scr[...] = im0_ref[...]
        rbuf[0:pad, :] = jnp.zeros((pad, S5_STATE), F32)
        ibuf[0:pad, :] = jnp.zeros((pad, S5_STATE), F32)

    u = u_ref[...].reshape(rows, GROUP_W)
    bu = _dot(u, bb_ref[...])
    xr, xi = bu[:, :S5_STATE], bu[:, S5_STATE:]
    pr, pi = ar_ref[...], ai_ref[...]
    s_r, s_i = re_scr[...], im_scr[...]
    rin = _row_in_seq(bt, tc)
    first = rin == 0
    xr = xr + jnp.where(first, _rows(pr * s_r - pi * s_i, tc), 0.0)
    xi = xi + jnp.where(first, _rows(pr * s_i + pi * s_r, tc), 0.0)

    d = 1
    while d < tc:
        r_sh = jnp.where(rin >= d, _shifted(rbuf, xr, d, pad, rows), 0.0)
        i_sh = jnp.where(rin >= d, _shifted(ibuf, xi, d, pad, rows), 0.0)
        xr = xr + pr * r_sh - pi * i_sh
        xi = xi + pr * i_sh + pi * r_sh
        pr, pi = pr * pr - pi * pi, 2.0 * pr * pi
        d *= 2
    re_last = xr.reshape(bt, tc, S5_STATE)[:, tc - 1:tc, :]
    im_last = xi.reshape(bt, tc, S5_STATE)[:, tc - 1:tc, :]
    re_scr[...] = re_last
    im_scr[...] = im_last
    re_ref[...] = re_last
    im_ref[...] = im_last

    y = _dot(jnp.concatenate([xr, xi], axis=-1), cc_ref[...])
    y = jax.nn.gelu(y + d_ref[...] * u)
    y = y * jax.nn.sigmoid(_dot(y, gw_ref[...]) + gb_ref[...])
    y_ref[...] = _rms(y, norm_ref[...]).reshape(bt, tc, GROUP_W)


def _s5_call(z, re0, im0, w, bt, tc):
    b, l, _ = z.shape
    pad = max(SUBLANES, tc // 2)
    rows = bt * tc
    vec = lambda n: _const_spec((1, n))
    cb = (RW_COLS + 2 * GROUP_W) // GROUP_W
    state = _state_spec(bt, 1, S5_STATE)
    return pl.pallas_call(
        functools.partial(_s5_kernel, bt=bt, tc=tc, pad=pad),
        grid=(b // bt, l // tc),
        in_specs=[_seq_spec(bt, tc, GROUP_W, cb), state, state,
                  vec(S5_STATE), vec(S5_STATE), _const_spec((GROUP_W, 2 * S5_STATE)),
                  _const_spec((2 * S5_STATE, GROUP_W)), vec(GROUP_W),
                  _const_spec((GROUP_W, GROUP_W)), vec(GROUP_W), vec(GROUP_W)],
        out_specs=[_seq_spec(bt, tc, GROUP_W, 0), state, state],
        out_shape=[jax.ShapeDtypeStruct((b, l, GROUP_W), F32),
                   jax.ShapeDtypeStruct((b, 1, S5_STATE), F32),
                   jax.ShapeDtypeStruct((b, 1, S5_STATE), F32)],
        scratch_shapes=[pltpu.VMEM((pad + rows, S5_STATE), F32),
                        pltpu.VMEM((pad + rows, S5_STATE), F32),
                        pltpu.VMEM((bt, 1, S5_STATE), F32), pltpu.VMEM((bt, 1, S5_STATE), F32)],
        compiler_params=_params(("parallel", "arbitrary")),
        name="s5",
    )(z, re0, im0, w["s5_ar"], w["s5_ai"], w["s5_bb"], w["s5_cc"], w["s5_d"], w["s5_glu_w"],
      w["s5_glu_b"], w["s5_norm"])


def _mlstm_kernel(q_ref, k_ref, v_ref, og_ref, ig_ref, fg_ref, c0_ref, n0_ref, m0_ref, bi_ref,
                  bf_ref, lng_ref, y_ref, c_ref, n_ref, m_ref, c_scr, n_scr, m_scr, *, bt, tc):
    c = pl.program_id(1)
    rows = bt * tc

    @pl.when(c == 0)
    def _():
        c_scr[...] = c0_ref[...]
        n_scr[...] = n0_ref[...]
        m_scr[...] = m0_ref[...]

    flat = lambda ref: ref[...].reshape(rows, ref.shape[-1])
    log_i = flat(ig_ref) + bi_ref[...]
    log_f = jax.nn.log_sigmoid(flat(fg_ref) + bf_ref[...])
    _, incl = _seq_masks(bt, tc)
    bcum = _dot_hi(incl.astype(F32), log_f)
    m3 = m_scr[...]
    bcum3 = bcum.reshape(bt, tc, LANES)
    b_last3 = bcum3[:, tc - 1:tc, :]
    inter_all = (bcum3 + m3).reshape(rows, LANES)
    wk3 = b_last3 - bcum3 + log_i.reshape(bt, tc, LANES)
    m_new3 = jnp.maximum(b_last3 + m3, jnp.max(wk3, axis=1, keepdims=True))
    sc_all = jnp.exp(wk3 - m_new3).reshape(rows, LANES)
    decay3 = jnp.exp(b_last3 + m3 - m_new3)
    m_scr[...] = m_new3
    m_ref[...] = m_new3

    eye = (lax.broadcasted_iota(jnp.int32, (SUBLANES, LANES), 0)
           == lax.broadcasted_iota(jnp.int32, (SUBLANES, LANES), 1)).astype(F32)
    li_rows = _dot_nt(eye, log_i, HI)
    b_rows = _dot_nt(eye, bcum, HI)

    q, k, v = flat(q_ref), flat(k_ref) * (HEAD_DIM ** -0.5), flat(v_ref)
    outs = []
    for h in range(HEADS):
        sl = slice(h * HEAD_DIM, (h + 1) * HEAD_DIM)
        hl = slice(h, h + 1)
        qh, kh, vh = q[:, sl], k[:, sl], v[:, sl]
        inter = inter_all[:, hl]
        dmat = jnp.where(incl, bcum[:, hl] - b_rows[hl, :] + li_rows[hl, :], -jnp.inf)
        mt = jnp.maximum(jnp.max(dmat, axis=-1, keepdims=True), inter)
        pm = jnp.exp(dmat - mt) * _dot_nt(qh.astype(BF16), kh.astype(BF16))
        ei = jnp.exp(inter - mt)
        c_old = [c_scr[b, h] for b in range(bt)]
        qc = jnp.concatenate(
            [_dot_nt(qh[b * tc:(b + 1) * tc].astype(BF16), c_old[b].astype(BF16))
             for b in range(bt)], axis=0)
        n_old3 = n_scr[:, hl, :]
        qn = jnp.sum(qh * _rows(n_old3, tc), axis=-1, keepdims=True)
        num = _dot(pm, vh) + ei * qc
        den = jnp.sum(pm, axis=-1, keepdims=True) + ei * qn
        hh = num / jnp.maximum(jnp.abs(den), jnp.exp(-mt))
        mu = jnp.mean(hh, axis=-1, keepdims=True)
        cen = hh - mu
        var = jnp.mean(cen * cen, axis=-1, keepdims=True)
        outs.append(cen * lax.rsqrt(var + EPS))

        sc = sc_all[:, hl]
        scv = sc * vh
        for b in range(bt):
            rs = slice(b * tc, (b + 1) * tc)
            c_scr[b, h] = (decay3[b, :, hl] * c_old[b]
                           + _dot_tn(scv[rs].astype(BF16), kh[rs].astype(BF16)))
        sck = (sc * kh).reshape(bt, tc, HEAD_DIM)
        n_scr[:, hl, :] = decay3[:, :, hl] * n_old3 + jnp.sum(sck, axis=1, keepdims=True)
    y = jnp.concatenate(outs, axis=-1) * lng_ref[...] * jax.nn.sigmoid(flat(og_ref))
    y_ref[...] = y.reshape(bt, tc, GROUP_W)
    c_ref[...] = c_scr[...]
    n_ref[...] = n_scr[...]


def _mlstm_call(z, c0, n0, m0, w, bt, tc):
    b, l, _ = z.shape
    cb = (RW_COLS + 3 * GROUP_W) // GROUP_W
    cst = _state_spec(bt, HEADS, HEAD_DIM, HEAD_DIM)
    nst = _state_spec(bt, HEADS, HEAD_DIM)
    mst = _state_spec(bt, 1, LANES)
    return pl.pallas_call(
        functools.partial(_mlstm_kernel, bt=bt, tc=tc),
        grid=(b // bt, l // tc),
        in_specs=[_seq_spec(bt, tc, GROUP_W, cb), _seq_spec(bt, tc, GROUP_W, cb + 1),
                  _seq_spec(bt, tc, GROUP_W, cb + 2), _seq_spec(bt, tc, GROUP_W, cb + 3),
                  _seq_spec(bt, tc, LANES, IGATE_COL_BLOCK),
                  _seq_spec(bt, tc, LANES, IGATE_COL_BLOCK + 1),
                  cst, nst, mst, _const_spec((1, LANES)), _const_spec((1, LANES)),
                  _const_spec((1, GROUP_W))],
        out_specs=[_seq_spec(bt, tc, GROUP_W, 0), cst, nst, mst],
        out_shape=[jax.ShapeDtypeStruct((b, l, GROUP_W), F32),
                   jax.ShapeDtypeStruct((b, HEADS, HEAD_DIM, HEAD_DIM), F32),
                   jax.ShapeDtypeStruct((b, HEADS, HEAD_DIM), F32),
                   jax.ShapeDtypeStruct((b, 1, LANES), F32)],
        scratch_shapes=[pltpu.VMEM((bt, HEADS, HEAD_DIM, HEAD_DIM), F32),
                        pltpu.VMEM((bt, HEADS, HEAD_DIM), F32),
                        pltpu.VMEM((bt, 1, LANES), F32)],
        compiler_params=_params(("parallel", "arbitrary")),
        name="mlstm",
    )(z, z, z, z, z, z, c0, n0, m0, w["ml_bi"], w["ml_bf"], w["ml_ln_g"])


def _block_diag(blocks):
    n, r, c = blocks.shape
    eye = jnp.eye(n, dtype=blocks.dtype)
    return (eye[:, None, :, None] * blocks[:, :, None, :]).reshape(n * r, n * c)


def _lane_block(vec):
    return jnp.pad(vec.astype(F32), (0, LANES - vec.shape[0])).reshape(1, LANES)


def _prep_layer(wl):
    row = lambda t: t.reshape(1, -1).astype(F32)
    pad_ff = D_FF_PAD - D_FF
    out = {}
    for f in ("ffn1", "ffn2"):
        out[f + "_pre"] = row(wl[f + "_pre"])
        out[f + "_post"] = row(wl[f + "_post"])
        out[f + "_wg"] = jnp.pad(wl[f + "_wg"], ((0, 0), (0, pad_ff))).astype(BF16)
        out[f + "_wu"] = jnp.pad(wl[f + "_wu"], ((0, 0), (0, pad_ff))).astype(BF16)
        out[f + "_wd"] = jnp.pad(wl[f + "_wd"], ((0, pad_ff), (0, 0))).astype(BF16)
    out["mix_pre"] = row(wl["mix_pre"])
    out["mix_post"] = row(wl["mix_post"])
    w_in = wl["w_in"]
    gate_pad = ((0, 0), (0, LANES - HEADS))
    out["w_in"] = jnp.concatenate(
        [w_in[:, :MAIN_COLS], jnp.pad(w_in[:, MAIN_COLS:MAIN_COLS + HEADS], gate_pad),
         jnp.pad(w_in[:, MAIN_COLS + HEADS:], gate_pad)], axis=1).astype(BF16)
    out["w_out"] = wl["w_out"].astype(BF16)

    for name in ("rw_mu", "rw_w0", "rw_a0", "rw_kk", "rw_ka", "rw_rk", "rw_ln_g", "rw_ln_b"):
        out[name] = row(wl[name])
    for name in ("rw_w2", "rw_a2", "rw_g2"):
        out[name] = wl[name].astype(BF16)
    out["head_sum"] = _block_diag(jnp.ones((HEADS, HEAD_DIM, HEAD_DIM), BF16))

    out["lru_conv_w"] = wl["lru_conv_w"].astype(F32)
    out["lru_conv_b"] = row(wl["lru_conv_b"])
    out["lru_wax"] = jnp.concatenate(
        [_block_diag(wl["lru_wa"]), _block_diag(wl["lru_wx"])], axis=1).astype(BF16)
    out["lru_bax"] = jnp.concatenate([row(wl["lru_ba"]), row(wl["lru_bx"])], axis=1)
    out["lru_lambda"] = row(wl["lru_lambda"])
    out["lru_norm"] = row(wl["lru_norm"])

    lr, li = wl["s5_a_re"].astype(F32), wl["s5_a_im"].astype(F32)
    dt = jnp.exp(wl["s5_log_dt"].astype(F32))[:, None]
    mag = jnp.exp(lr * dt)
    ar, ai = mag * jnp.cos(li * dt), mag * jnp.sin(li * dt)
    den = lr * lr + li * li
    zr = ((ar - 1.0) * lr + ai * li) / den
    zi = (ai * lr - (ar - 1.0) * li) / den
    b_re, b_im = wl["s5_b_re"].astype(F32), wl["s5_b_im"].astype(F32)
    bbr = zr[..., None] * b_re - zi[..., None] * b_im
    bbi = zr[..., None] * b_im + zi[..., None] * b_re
    to_in = lambda t: _block_diag(jnp.swapaxes(t, 1, 2))
    out["s5_ar"], out["s5_ai"] = row(ar), row(ai)
    out["s5_bb"] = jnp.concatenate([to_in(bbr), to_in(bbi)], axis=1).astype(BF16)
    to_out = lambda t: _block_diag(jnp.swapaxes(t, 1, 2))
    out["s5_cc"] = jnp.concatenate(
        [to_out(wl["s5_c_re"]), -to_out(wl["s5_c_im"])], axis=0).astype(BF16)
    out["s5_d"] = row(wl["s5_d"])
    out["s5_glu_w"] = wl["s5_glu_w"].astype(BF16)
    out["s5_glu_b"] = row(wl["s5_glu_b"])
    out["s5_norm"] = row(wl["s5_norm"])

    out["ml_bi"] = _lane_block(wl["ml_bi"])
    out["ml_bf"] = _lane_block(wl["ml_bf"])
    out["ml_ln_g"] = row(wl["ml_ln_g"])
    return out


def _tiling(b, l, rows_cap, tc_cap):
    tc = tc_cap
    while tc > SUBLANES and l % tc:
        tc //= 2
    if l % tc:
        raise ValueError(f"sequence length {l} is not a multiple of {SUBLANES}")
    bt = 1
    while bt * 2 * tc <= rows_cap and b % (bt * 2) == 0:
        bt *= 2
    return bt, tc


def _layer(x, st, w):
    b, l, d = x.shape
    n = b * l
    shift0, s0, buf0, h0, re0, im0, c0, n0, m0 = st
    x1, z = _ffn_in_call(x.reshape(n, d), w["ffn1_pre"], w["ffn1_wg"], w["ffn1_wu"],
                         w["ffn1_wd"], w["ffn1_post"], w["mix_pre"], w["w_in"])
    z = z.reshape(b, l, IN_COLS_PAD)

    y_rw, n_shift, n_s = _rwkv_call(z, shift0.reshape(b, 1, RW_COLS), s0, w,
                                    *_tiling(b, l, 256, 64))
    y_lru, n_buf, n_h = _lru_call(z, buf0, h0.reshape(b, 1, GROUP_W), w,
                                  *_tiling(b, l, 256, 256))
    y_s5, n_re, n_im = _s5_call(z, re0.reshape(b, 1, S5_STATE), im0.reshape(b, 1, S5_STATE), w,
                                *_tiling(b, l, 256, 128))
    m0p = jnp.pad(m0, ((0, 0), (0, LANES - HEADS))).reshape(b, 1, LANES)
    y_ml, n_c, n_n, n_m = _mlstm_call(z, c0, n0, m0p, w, *_tiling(b, l, 128, 128))

    flat = lambda t: t.reshape(n, GROUP_W)
    x3 = _out_ffn_call(x1, flat(y_rw), flat(y_lru), flat(y_s5), flat(y_ml), w["w_out"],
                       w["mix_post"], w["ffn2_pre"], w["ffn2_wg"], w["ffn2_wu"], w["ffn2_wd"],
                       w["ffn2_post"])
    new = (n_shift.reshape(b, RW_COLS), n_s, n_buf, n_h.reshape(b, GROUP_W),
           n_re.reshape(b, S5_GROUPS, S5_P), n_im.reshape(b, S5_GROUPS, S5_P), n_c, n_n,
           n_m.reshape(b, LANES)[:, :HEADS])
    return x3.reshape(b, l, d), new


def _zero_state(n):
    return (jnp.zeros((n, RW_COLS), F32), jnp.zeros((n, HEADS, HEAD_DIM, HEAD_DIM), F32),
            jnp.zeros((n, CONV_W - 1, GROUP_W), F32), jnp.zeros((n, GROUP_W), F32),
            jnp.zeros((n, S5_GROUPS, S5_P), F32), jnp.zeros((n, S5_GROUPS, S5_P), F32),
            jnp.zeros((n, HEADS, HEAD_DIM, HEAD_DIM), F32), jnp.zeros((n, HEADS, HEAD_DIM), F32),
            jnp.zeros((n, HEADS), F32))


def kernel(x_prompt, x_sample, state_rwkv_shift, state_rwkv_wkv, state_lru_conv, state_lru_h, state_s5_re, state_s5_im, state_mlstm_C, state_mlstm_n, state_mlstm_m, ffn1_pre, ffn1_wg, ffn1_wu, ffn1_wd, ffn1_post, mix_pre, w_in, w_out, mix_post, ffn2_pre, ffn2_wg, ffn2_wu, ffn2_wd, ffn2_post, rw_mu, rw_w0, rw_w2, rw_a0, rw_a2, rw_g2, rw_kk, rw_ka, rw_rk, rw_ln_g, rw_ln_b, lru_conv_w, lru_conv_b, lru_wa, lru_ba, lru_wx, lru_bx, lru_lambda, lru_norm, s5_a_re, s5_a_im, s5_log_dt, s5_b_re, s5_b_im, s5_c_re, s5_c_im, s5_d, s5_glu_w, s5_glu_b, s5_norm, ml_bi, ml_bf, ml_ln_g):
    weights = dict(
        ffn1_pre=ffn1_pre, ffn1_wg=ffn1_wg, ffn1_wu=ffn1_wu, ffn1_wd=ffn1_wd, ffn1_post=ffn1_post,
        mix_pre=mix_pre, w_in=w_in, w_out=w_out, mix_post=mix_post,
        ffn2_pre=ffn2_pre, ffn2_wg=ffn2_wg, ffn2_wu=ffn2_wu, ffn2_wd=ffn2_wd, ffn2_post=ffn2_post,
        rw_mu=rw_mu, rw_w0=rw_w0, rw_w2=rw_w2, rw_a0=rw_a0, rw_a2=rw_a2, rw_g2=rw_g2,
        rw_kk=rw_kk, rw_ka=rw_ka, rw_rk=rw_rk, rw_ln_g=rw_ln_g, rw_ln_b=rw_ln_b,
        lru_conv_w=lru_conv_w, lru_conv_b=lru_conv_b, lru_wa=lru_wa, lru_ba=lru_ba,
        lru_wx=lru_wx, lru_bx=lru_bx, lru_lambda=lru_lambda, lru_norm=lru_norm,
        s5_a_re=s5_a_re, s5_a_im=s5_a_im, s5_log_dt=s5_log_dt, s5_b_re=s5_b_re, s5_b_im=s5_b_im,
        s5_c_re=s5_c_re, s5_c_im=s5_c_im, s5_d=s5_d, s5_glu_w=s5_glu_w, s5_glu_b=s5_glu_b,
        s5_norm=s5_norm, ml_bi=ml_bi, ml_bf=ml_bf, ml_ln_g=ml_ln_g)
    sample_state = (state_rwkv_shift, state_rwkv_wkv, state_lru_conv, state_lru_h,
                    state_s5_re, state_s5_im, state_mlstm_C, state_mlstm_n, state_mlstm_m)
    depth = ffn1_pre.shape[0]
    y_p, y_s = x_prompt.astype(F32), x_sample.astype(F32)
    outs_p, outs_s = [], []
    for layer in range(depth):
        w = _prep_layer({name: arr[layer] for name, arr in weights.items()})
        y_p, st_p = _layer(y_p, _zero_state(x_prompt.shape[0]), w)
        y_s, st_s = _layer(y_s, tuple(s[layer].astype(F32) for s in sample_state), w)
        outs_p.append(st_p)
        outs_s.append(st_s)
    p_states = [jnp.stack([o[i] for o in outs_p]) for i in range(9)]
    s_states = [jnp.stack([o[i] for o in outs_s]) for i in range(9)]
    return (y_p, y_s, *p_states, *s_states)
```

```python
import functools

import jax
import jax.numpy as jnp
from jax import lax
from jax.experimental import pallas as pl
from jax.experimental.pallas import tpu as pltpu

F32 = jnp.float32
BF16 = jnp.bfloat16

LANES = 128
SUBLANES = 8
VMEM_LIMIT_BYTES = 56 * 1024 * 1024

D_MODEL = 1024
GROUP_W = D_MODEL // 4
HEAD_DIM = 64
HEADS = GROUP_W // HEAD_DIM
RW_W_RANK = 64
RW_A_RANK = 64
RW_G_RANK = 128
RW_COLS = 3 * GROUP_W + RW_W_RANK + RW_A_RANK + RW_G_RANK
RW_DECAY_SCALE = 0.606531
RW_GN_EPS = 64e-5
LRU_C = 8.0
CONV_W = 4
S5_GROUP = 16
S5_GROUPS = GROUP_W // S5_GROUP
S5_P = 64
S5_STATE = S5_GROUPS * S5_P
MAIN_COLS = RW_COLS + 7 * GROUP_W
IN_COLS = MAIN_COLS + 2 * HEADS
IN_COLS_PAD = MAIN_COLS + 2 * LANES
IGATE_COL_BLOCK = MAIN_COLS // LANES
D_FF = 2752
D_FF_PAD = -(-D_FF // LANES) * LANES
EPS = 1e-6
HI = lax.Precision.HIGHEST


def _dot(a, b):
    return jnp.dot(a.astype(BF16), b.astype(BF16), preferred_element_type=F32)


def _dot_nt(a, b, precision=None):
    return lax.dot_general(a, b, (((1,), (1,)), ((), ())), precision=precision,
                           preferred_element_type=F32)


def _dot_tn(a, b):
    return lax.dot_general(a, b, (((0,), (0,)), ((), ())), preferred_element_type=F32)


def _dot_hi(a, b):
    return jnp.dot(a, b, precision=HI, preferred_element_type=F32)


def _rms(x, g):
    return x * lax.rsqrt(jnp.mean(x * x, axis=-1, keepdims=True) + EPS) * g


def _seq_masks(bt, tc, reps=1):
    rows = bt * tc
    r = lax.broadcasted_iota(jnp.int32, (rows, reps * rows), 0)
    c = lax.broadcasted_iota(jnp.int32, (rows, reps * rows), 1) & (rows - 1)
    if bt == 1:
        return r > c, r >= c
    start = r - (r & (tc - 1))
    return (c < r) & (c >= start), (c <= r) & (c >= start)


def _rows(x3, tc):
    bt, _, w = x3.shape
    return jnp.broadcast_to(x3, (bt, tc, w)).reshape(bt * tc, w)


def _const_spec(shape):
    nd = len(shape)
    return pl.BlockSpec(shape, lambda *_: (0,) * nd, pipeline_mode=pl.Buffered(1))


def _layer_spec(layer, *shape):
    zeros = (0,) * len(shape)
    return pl.BlockSpec((None,) + shape, lambda *_: (layer,) + zeros,
                        pipeline_mode=pl.Buffered(1))


def _params(sem):
    return pltpu.CompilerParams(dimension_semantics=sem, vmem_limit_bytes=VMEM_LIMIT_BYTES)


def _cast_kernel(x_ref, o_ref, *, copies):
    rows = x_ref.shape[0]
    o_ref[...] = jnp.zeros(o_ref.shape, BF16)
    for src, dst, width in copies:
        o_ref[0:rows, dst:dst + width] = x_ref[:, src:src + width].astype(BF16)


def _cast_call(x, out_rows, out_cols, copies=None):
    depth, rows, cols = x.shape
    copies = copies or ((0, 0, cols),)
    return pl.pallas_call(
        functools.partial(_cast_kernel, copies=copies),
        grid=(depth,),
        in_specs=[pl.BlockSpec((None, rows, cols), lambda i: (i, 0, 0))],
        out_specs=pl.BlockSpec((None, out_rows, out_cols), lambda i: (i, 0, 0)),
        out_shape=jax.ShapeDtypeStruct((depth, out_rows, out_cols), BF16),
        compiler_params=_params(("parallel",)),
        name="cast_weights",
    )(x)


def _swiglu_half_step(x, pre, wg, wu, wd, post):
    h = _rms(x, pre).astype(BF16)
    g = jnp.dot(h, wg, preferred_element_type=F32)
    u = jnp.dot(h, wu, preferred_element_type=F32)
    act = (jax.nn.silu(g) * u).astype(BF16)
    f = jnp.dot(act, wd, preferred_element_type=F32)
    return x + 0.5 * _rms(f, post)


def _ffn_in_kernel(x_ref, pre_ref, wg_ref, wu_ref, wd_ref, post_ref, mpre_ref, win_ref,
                   xo_ref, z_ref):
    x1 = _swiglu_half_step(x_ref[...], pre_ref[...], wg_ref[...], wu_ref[...], wd_ref[...],
                           post_ref[...])
    xo_ref[...] = x1
    z_ref[...] = jnp.dot(_rms(x1, mpre_ref[...]).astype(BF16), win_ref[...],
                         preferred_element_type=F32)


def _out_ffn_kernel(x_ref, yrw_ref, ylru_ref, ys5_ref, yml_ref, wout_ref, mpost_ref,
                    pre_ref, wg_ref, wu_ref, wd_ref, post_ref, xo_ref):
    cat = jnp.concatenate([yrw_ref[...], ylru_ref[...], ys5_ref[...], yml_ref[...]], axis=-1)
    m = jnp.dot(cat.astype(BF16), wout_ref[...], preferred_element_type=F32)
    x2 = x_ref[...] + _rms(m, mpost_ref[...])
    xo_ref[...] = _swiglu_half_step(x2, pre_ref[...], wg_ref[...], wu_ref[...], wd_ref[...],
                                    post_ref[...])


def _token_tile(n):
    for tm in (512, 256, 128, 64, 32, 16, 8):
        if n % tm == 0:
            return tm
    raise ValueError(f"token count {n} is not a multiple of {SUBLANES}")


def _ffn_specs(ws):
    return [ws(1, D_MODEL), ws(D_MODEL, D_FF_PAD), ws(D_MODEL, D_FF_PAD), ws(D_FF_PAD, D_MODEL),
            ws(1, D_MODEL)]


def _ffn_in_call(x, w, layer):
    n = x.shape[0]
    tm = _token_tile(n)
    row = lambda width: pl.BlockSpec((tm, width), lambda i: (i, 0))
    ws = functools.partial(_layer_spec, layer)
    return pl.pallas_call(
        _ffn_in_kernel,
        grid=(n // tm,),
        in_specs=[row(D_MODEL)] + _ffn_specs(ws) + [ws(1, D_MODEL), ws(D_MODEL, IN_COLS_PAD)],
        out_specs=[row(D_MODEL), row(IN_COLS_PAD)],
        out_shape=[jax.ShapeDtypeStruct((n, D_MODEL), F32),
                   jax.ShapeDtypeStruct((n, IN_COLS_PAD), F32)],
        compiler_params=_params(("parallel",)),
        name="ffn1_inproj",
    )(x, w["ffn1_pre"], w["ffn1_wg"], w["ffn1_wu"], w["ffn1_wd"], w["ffn1_post"], w["mix_pre"],
      w["w_in"])


def _out_ffn_call(x, yrw, ylru, ys5, yml, w, layer):
    n = x.shape[0]
    tm = _token_tile(n)
    row = lambda width: pl.BlockSpec((tm, width), lambda i: (i, 0))
    ws = functools.partial(_layer_spec, layer)
    return pl.pallas_call(
        _out_ffn_kernel,
        grid=(n // tm,),
        in_specs=[row(D_MODEL), row(GROUP_W), row(GROUP_W), row(GROUP_W), row(GROUP_W),
                  ws(D_MODEL, D_MODEL), ws(1, D_MODEL)] + _ffn_specs(ws),
        out_specs=row(D_MODEL),
        out_shape=jax.ShapeDtypeStruct((n, D_MODEL), F32),
        compiler_params=_params(("parallel",)),
        name="outproj_ffn2",
    )(x, yrw, ylru, ys5, yml, w["w_out"], w["mix_post"], w["ffn2_pre"], w["ffn2_wg"],
      w["ffn2_wu"], w["ffn2_wd"], w["ffn2_post"])


def _seq_spec(bt, tc, w, col_block):
    return pl.BlockSpec((bt, tc, w), lambda i, c: (i, c, col_block))


def _state_spec(bt, *dims):
    zeros = (0,) * len(dims)
    return pl.BlockSpec((bt,) + dims, lambda i, c: (i,) + zeros)


def _rwkv_kernel(p_ref, shift0_ref, s0_ref, mu_ref, w0_ref, w2_ref, a0_ref, a2_ref, g2_ref,
                 kkw_ref, ka_ref, rk_ref, lng_ref, lnb_ref, hsum_ref,
                 y_ref, shift_ref, s_ref, pbuf, s_scr, *, bt, tc):
    c = pl.program_id(1)
    rows = bt * tc
    lo = SUBLANES - 1

    @pl.when(c == 0)
    def _():
        pbuf[:, lo:SUBLANES, :] = shift0_ref[...]
        s_scr[...] = s0_ref[...]

    p3 = p_ref[...]
    pbuf[:, SUBLANES:SUBLANES + tc, :] = p3
    prev = pbuf[:, lo:lo + tc, :].reshape(rows, RW_COLS)
    last = p3[:, tc - 1:tc, :]
    pbuf[:, lo:SUBLANES, :] = last
    shift_ref[...] = last
    p = p3.reshape(rows, RW_COLS)

    xm = p + (prev - p) * mu_ref[...]
    g = GROUP_W
    r, k, v = xm[:, 0:g], xm[:, g:2 * g], xm[:, 2 * g:3 * g]
    lw = xm[:, 3 * g:3 * g + RW_W_RANK]
    la = xm[:, 3 * g + RW_W_RANK:3 * g + RW_W_RANK + RW_A_RANK]
    lg = xm[:, 3 * g + RW_W_RANK + RW_A_RANK:]
    logw = -RW_DECAY_SCALE * jax.nn.sigmoid(w0_ref[...] + _dot(jnp.tanh(lw), w2_ref[...]))
    a = jax.nn.sigmoid(a0_ref[...] + _dot(la, a2_ref[...]))
    gate = _dot(jax.nn.sigmoid(lg), g2_ref[...])

    hsum = hsum_ref[...]
    kk = k * kkw_ref[...]
    kk = kk * lax.rsqrt(jnp.maximum(_dot(kk * kk, hsum), 1e-12))
    k2 = k * (1.0 + (a - 1.0) * ka_ref[...])

    _, incl = _seq_masks(bt, tc)
    strict2, incl2 = _seq_masks(bt, tc, reps=2)
    cum = _dot_hi(incl.astype(F32), logw)
    cum3 = cum.reshape(bt, tc, g)
    total3 = cum3[:, tc - 1:tc, :]
    rest = jnp.exp(total3 - cum3).reshape(rows, g)
    wtot3 = jnp.exp(total3)
    inv = jnp.exp(-cum)
    kka = kk * a
    at = -kk * jnp.exp(cum - logw)
    rt = r * jnp.exp(cum)
    bk_t = kka * inv, k2 * inv
    bk_w = kka * rest, k2 * rest

    n_double = max(1, (tc - 1).bit_length())
    outs = []
    for h in range(HEADS):
        sl = slice(h * HEAD_DIM, (h + 1) * HEAD_DIM)
        ah, rh, vh = at[:, sl], rt[:, sl], v[:, sl]
        bkh = jnp.concatenate([bk_t[0][:, sl], bk_t[1][:, sl]], axis=0).astype(BF16)
        bkw = jnp.concatenate([bk_w[0][:, sl], bk_w[1][:, sl]], axis=0)
        m_a = jnp.where(strict2, _dot_nt(ah.astype(BF16), bkh), 0.0)
        m_r = jnp.where(incl2, _dot_nt(rh.astype(BF16), bkh), 0.0)
        s_old = [s_scr[b, h] for b in range(bt)]
        from_state = []
        for b in range(bt):
            rs = slice(b * tc, (b + 1) * tc)
            ar = jnp.concatenate([ah[rs], rh[rs]], axis=0)
            from_state.append(_dot_nt(ar.astype(BF16), s_old[b].astype(BF16)))
        u = (jnp.concatenate([fs[:tc] for fs in from_state], axis=0)
             + _dot(m_a[:, rows:], vh))
        pw = m_a[:, :rows]
        for i in range(n_double):
            u = u + _dot(pw, u)
            if i + 1 < n_double:
                pw = _dot(pw, pw)
        uv = jnp.concatenate([u, vh], axis=0)
        outs.append(jnp.concatenate([fs[tc:] for fs in from_state], axis=0) + _dot(m_r, uv))
        for b in range(bt):
            rs = slice(b * tc, (b + 1) * tc)
            uv_b = jnp.concatenate([uv[rs], uv[rows + b * tc:rows + (b + 1) * tc]], axis=0)
            bkw_b = jnp.concatenate([bkw[rs], bkw[rows + b * tc:rows + (b + 1) * tc]], axis=0)
            s_scr[b, h] = (s_old[b] * wtot3[b, :, sl]
                           + _dot_tn(uv_b.astype(BF16), bkw_b.astype(BF16)))
    o = jnp.concatenate(outs, axis=-1)

    mean = _dot(o, hsum) * (1.0 / HEAD_DIM)
    cen = o - mean
    var = _dot(cen * cen, hsum) * (1.0 / HEAD_DIM)
    y = cen * lax.rsqrt(var + RW_GN_EPS) * lng_ref[...] + lnb_ref[...]
    bonus = _dot(r * k2 * rk_ref[...], hsum) * v
    y_ref[...] = ((y + bonus) * gate).reshape(bt, tc, g)
    s_ref[...] = s_scr[...]


def _rwkv_call(z, shift0, s0, w, layer, bt, tc):
    b, l, _ = z.shape
    ws = functools.partial(_layer_spec, layer)
    return pl.pallas_call(
        functools.partial(_rwkv_kernel, bt=bt, tc=tc),
        grid=(b // bt, l // tc),
        in_specs=[_seq_spec(bt, tc, RW_COLS, 0), _state_spec(bt, 1, RW_COLS),
                  _state_spec(bt, HEADS, HEAD_DIM, HEAD_DIM),
                  ws(1, RW_COLS), ws(1, GROUP_W), ws(RW_W_RANK, GROUP_W), ws(1, GROUP_W),
                  ws(RW_A_RANK, GROUP_W), ws(RW_G_RANK, GROUP_W),
                  ws(1, GROUP_W), ws(1, GROUP_W), ws(1, GROUP_W), ws(1, GROUP_W),
                  ws(1, GROUP_W), _const_spec((GROUP_W, GROUP_W))],
        out_specs=[_seq_spec(bt, tc, GROUP_W, 0), _state_spec(bt, 1, RW_COLS),
                   _state_spec(bt, HEADS, HEAD_DIM, HEAD_DIM)],
        out_shape=[jax.ShapeDtypeStruct((b, l, GROUP_W), F32),
                   jax.ShapeDtypeStruct((b, 1, RW_COLS), F32),
                   jax.ShapeDtypeStruct((b, HEADS, HEAD_DIM, HEAD_DIM), F32)],
        scratch_shapes=[pltpu.VMEM((bt, SUBLANES + tc, RW_COLS), F32),
                        pltpu.VMEM((bt, HEADS, HEAD_DIM, HEAD_DIM), F32)],
        compiler_params=_params(("parallel", "arbitrary")),
        name="rwkv7",
    )(z, shift0, s0, w["rw_mu"], w["rw_w0"], w["rw_w2"], w["rw_a0"], w["rw_a2"], w["rw_g2"],
      w["rw_kk"], w["rw_ka"], w["rw_rk"], w["rw_ln_g"], w["rw_ln_b"], w["head_sum"])


def _time_slab(t, g, tc):
    return pl.ds(g * SUBLANES * tc + t, SUBLANES, stride=tc)


def _lane_tiles(x):
    return [x[:, k * LANES:(k + 1) * LANES] for k in range(x.shape[-1] // LANES)]


def _lru_kernel(x_ref, gate_ref, buf0_ref, h0_ref, cw_ref, cb_ref, wax_ref, bax_ref, lam_ref,
                norm_ref, y_ref, buf_ref, h_ref, xbuf, abuf, bbuf, h_scr, *, bt, tc):
    c = pl.program_id(1)
    rows = bt * tc
    lo = SUBLANES - (CONV_W - 1)
    n_tiles = GROUP_W // LANES

    @pl.when(c == 0)
    def _():
        xbuf[:, lo:SUBLANES, :] = buf0_ref[...]
        h_scr[...] = h0_ref[...]

    x3 = x_ref[...]
    xbuf[:, SUBLANES:SUBLANES + tc, :] = x3
    cw = cw_ref[...]
    xc3 = cb_ref[...]
    for j in range(CONV_W):
        xc3 = xc3 + xbuf[:, lo + j:lo + j + tc, :] * cw[j:j + 1, :]
    tail = x3[:, tc - (CONV_W - 1):tc, :]
    xbuf[:, lo:SUBLANES, :] = tail
    buf_ref[...] = tail
    xc = xc3.reshape(rows, GROUP_W)

    gates = jax.nn.sigmoid(_dot(xc, wax_ref[...]) + bax_ref[...])
    gate_r, gate_i = gates[:, :GROUP_W], gates[:, GROUP_W:]
    log_a = -LRU_C * gate_r * jax.nn.softplus(-lam_ref[...])
    a = jnp.exp(log_a)
    bb = jnp.sqrt(-jnp.tanh(log_a) * (a * a + 1.0)) * (gate_i * xc)
    for k, (a_k, b_k) in enumerate(zip(_lane_tiles(a), _lane_tiles(bb))):
        abuf[k] = a_k
        bbuf[k] = b_k

    for g in range(bt // SUBLANES):
        seqs = slice(g * SUBLANES, (g + 1) * SUBLANES)
        h = _lane_tiles(h_scr[seqs, :])
        for t in range(tc):
            slab = _time_slab(t, g, tc)
            for k in range(n_tiles):
                h[k] = abuf[k, slab, :] * h[k] + bbuf[k, slab, :]
                bbuf[k, slab, :] = h[k]
        h_scr[seqs, :] = jnp.concatenate(h, axis=-1)
    h_ref[...] = h_scr[...]
    h_all = jnp.concatenate([bbuf[k] for k in range(n_tiles)], axis=-1)
    gate = gate_ref[...].reshape(rows, GROUP_W)
    y_ref[...] = _rms(h_all * jax.nn.gelu(gate), norm_ref[...]).reshape(bt, tc, GROUP_W)


def _lru_call(z, buf0, h0, w, layer, bt, tc):
    b, l, _ = z.shape
    rows = bt * tc
    ws = functools.partial(_layer_spec, layer)
    cb = RW_COLS // GROUP_W
    tiles = pltpu.VMEM((GROUP_W // LANES, rows, LANES), F32)
    return pl.pallas_call(
        functools.partial(_lru_kernel, bt=bt, tc=tc),
        grid=(b // bt, l // tc),
        in_specs=[_seq_spec(bt, tc, GROUP_W, cb), _seq_spec(bt, tc, GROUP_W, cb + 1),
                  _state_spec(bt, CONV_W - 1, GROUP_W), _state_spec(bt, GROUP_W),
                  ws(CONV_W, GROUP_W), ws(1, GROUP_W), ws(GROUP_W, 2 * GROUP_W),
                  ws(1, 2 * GROUP_W), ws(1, GROUP_W), ws(1, GROUP_W)],
        out_specs=[_seq_spec(bt, tc, GROUP_W, 0), _state_spec(bt, CONV_W - 1, GROUP_W),
                   _state_spec(bt, GROUP_W)],
        out_shape=[jax.ShapeDtypeStruct((b, l, GROUP_W), F32),
                   jax.ShapeDtypeStruct((b, CONV_W - 1, GROUP_W), F32),
                   jax.ShapeDtypeStruct((b, GROUP_W), F32)],
        scratch_shapes=[pltpu.VMEM((bt, SUBLANES + tc, GROUP_W), F32), tiles, tiles,
                        pltpu.VMEM((bt, GROUP_W), F32)],
        compiler_params=_params(("parallel", "arbitrary")),
        name="rglru",
    )(z, z, buf0, h0, w["lru_conv_w"], w["lru_conv_b"], w["lru_wax"], w["lru_bax"],
      w["lru_lambda"], w["lru_norm"])


def _s5_kernel(ulo_ref, uhi_ref, re0_ref, im0_ref, ar_ref, ai_ref, bb_ref, cc_ref, d_ref, gw_ref,
               gb_ref, norm_ref, y_ref, re_ref, im_ref, rbuf, ibuf, ybuf, re_scr, im_scr, *,
               bt, tc):
    c = pl.program_id(1)
    n_tiles = S5_STATE // LANES
    groups = bt // SUBLANES

    @pl.when(c == 0)
    def _():
        re_scr[...] = re0_ref[...]
        im_scr[...] = im0_ref[...]

    def time_major(ref):
        return jnp.concatenate([ref[g * SUBLANES:(g + 1) * SUBLANES, t, :]
                                for g in range(groups) for t in range(tc)], axis=0)

    u = jnp.concatenate([time_major(ulo_ref), time_major(uhi_ref)], axis=-1)
    bu = _dot(u, bb_ref[...])
    for k, (r_k, i_k) in enumerate(zip(_lane_tiles(bu[:, :S5_STATE]),
                                       _lane_tiles(bu[:, S5_STATE:]))):
        rbuf[k] = r_k
        ibuf[k] = i_k
    slab_shape = (SUBLANES, LANES)
    a_r = [jnp.broadcast_to(t, slab_shape) for t in _lane_tiles(ar_ref[...])]
    a_i = [jnp.broadcast_to(t, slab_shape) for t in _lane_tiles(ai_ref[...])]

    for g in range(groups):
        seqs = slice(g * SUBLANES, (g + 1) * SUBLANES)
        s_r, s_i = _lane_tiles(re_scr[seqs, :]), _lane_tiles(im_scr[seqs, :])
        for t in range(tc):
            slab = pl.ds((g * tc + t) * SUBLANES, SUBLANES)
            for k in range(n_tiles):
                n_r = a_r[k] * s_r[k] - a_i[k] * s_i[k] + rbuf[k, slab, :]
                n_i = a_r[k] * s_i[k] + a_i[k] * s_r[k] + ibuf[k, slab, :]
                s_r[k], s_i[k] = n_r, n_i
                rbuf[k, slab, :] = n_r
                ibuf[k, slab, :] = n_i
        re_scr[seqs, :] = jnp.concatenate(s_r, axis=-1)
        im_scr[seqs, :] = jnp.concatenate(s_i, axis=-1)
    re_ref[...] = re_scr[...]
    im_ref[...] = im_scr[...]

    x_all = jnp.concatenate([rbuf[k] for k in range(n_tiles)]
                            + [ibuf[k] for k in range(n_tiles)], axis=-1)
    y = _dot(x_all, cc_ref[...])
    y = jax.nn.gelu(y + d_ref[...] * u)
    y = y * jax.nn.sigmoid(_dot(y, gw_ref[...]) + gb_ref[...])
    for k, y_k in enumerate(_lane_tiles(_rms(y, norm_ref[...]))):
        ybuf[k] = y_k
    for g in range(groups):
        for j in range(SUBLANES):
            seq_rows = pl.ds(g * tc * SUBLANES + j, tc, stride=SUBLANES)
            y_ref[g * SUBLANES + j] = jnp.concatenate(
                [ybuf[k, seq_rows, :] for k in range(GROUP_W // LANES)], axis=-1)


def _s5_call(z, re0, im0, w, layer, bt, tc):
    b, l, _ = z.shape
    rows = bt * tc
    ws = functools.partial(_layer_spec, layer)
    cb = (RW_COLS + 2 * GROUP_W) // LANES
    state = _state_spec(bt, S5_STATE)
    tiles = pltpu.VMEM((S5_STATE // LANES, rows, LANES), F32)
    return pl.pallas_call(
        functools.partial(_s5_kernel, bt=bt, tc=tc),
        grid=(b // bt, l // tc),
        in_specs=[_seq_spec(bt, tc, LANES, cb), _seq_spec(bt, tc, LANES, cb + 1), state, state,
                  ws(1, S5_STATE), ws(1, S5_STATE), ws(GROUP_W, 2 * S5_STATE),
                  ws(2 * S5_STATE, GROUP_W), ws(1, GROUP_W), ws(GROUP_W, GROUP_W),
                  ws(1, GROUP_W), ws(1, GROUP_W)],
        out_specs=[_seq_spec(bt, tc, GROUP_W, 0), state, state],
        out_shape=[jax.ShapeDtypeStruct((b, l, GROUP_W), F32),
                   jax.ShapeDtypeStruct((b, S5_STATE), F32),
                   jax.ShapeDtypeStruct((b, S5_STATE), F32)],
        scratch_shapes=[tiles, tiles, pltpu.VMEM((GROUP_W // LANES, rows, LANES), F32),
                        pltpu.VMEM((bt, S5_STATE), F32), pltpu.VMEM((bt, S5_STATE), F32)],
        compiler_params=_params(("parallel", "arbitrary")),
        name="s5",
    )(z, z, re0, im0, w["s5_ar"], w["s5_ai"], w["s5_bb"], w["s5_cc"], w["s5_d"], w["s5_glu_w"],
      w["s5_glu_b"], w["s5_norm"])


def _mlstm_kernel(q_ref, k_ref, v_ref, og_ref, ig_ref, fg_ref, c0_ref, n0_ref, m0_ref, bi_ref,
                  bf_ref, lng_ref, y_ref, c_ref, n_ref, m_ref, c_scr, n_scr, m_scr, *, bt, tc):
    c = pl.program_id(1)
    rows = bt * tc

    @pl.when(c == 0)
    def _():
        c_scr[...] = c0_ref[...]
        n_scr[...] = n0_ref[...]
        m_scr[...] = m0_ref[...]

    flat = lambda ref: ref[...].reshape(rows, ref.shape[-1])
    log_i = flat(ig_ref) + bi_ref[...]
    log_f = jax.nn.log_sigmoid(flat(fg_ref) + bf_ref[...])
    _, incl = _seq_masks(bt, tc)
    bcum = _dot_hi(incl.astype(F32), log_f)
    m3 = m_scr[...]
    bcum3 = bcum.reshape(bt, tc, LANES)
    b_last3 = bcum3[:, tc - 1:tc, :]
    inter_all = (bcum3 + m3).reshape(rows, LANES)
    wk3 = b_last3 - bcum3 + log_i.reshape(bt, tc, LANES)
    m_new3 = jnp.maximum(b_last3 + m3, jnp.max(wk3, axis=1, keepdims=True))
    sc_all = jnp.exp(wk3 - m_new3).reshape(rows, LANES)
    decay3 = jnp.exp(b_last3 + m3 - m_new3)
    m_scr[...] = m_new3
    m_ref[...] = m_new3

    eye = (lax.broadcasted_iota(jnp.int32, (SUBLANES, LANES), 0)
           == lax.broadcasted_iota(jnp.int32, (SUBLANES, LANES), 1)).astype(F32)
    li_rows = _dot_nt(eye, log_i, HI)
    b_rows = _dot_nt(eye, bcum, HI)

    q, k, v = flat(q_ref), flat(k_ref) * (HEAD_DIM ** -0.5), flat(v_ref)
    outs = []
    for h in range(HEADS):
        sl = slice(h * HEAD_DIM, (h + 1) * HEAD_DIM)
        hl = slice(h, h + 1)
        qh, kh, vh = q[:, sl], k[:, sl], v[:, sl]
        inter = inter_all[:, hl]
        dmat = jnp.where(incl, bcum[:, hl] - b_rows[hl, :] + li_rows[hl, :], -jnp.inf)
        mt = jnp.maximum(jnp.max(dmat, axis=-1, keepdims=True), inter)
        pm = jnp.exp(dmat - mt) * _dot_nt(qh.astype(BF16), kh.astype(BF16))
        ei = jnp.exp(inter - mt)
        c_old = [c_scr[b, h] for b in range(bt)]
        qc = jnp.concatenate(
            [_dot_nt(qh[b * tc:(b + 1) * tc].astype(BF16), c_old[b].astype(BF16))
             for b in range(bt)], axis=0)
        n_old3 = n_scr[:, hl, :]
        qn = jnp.sum(qh * _rows(n_old3, tc), axis=-1, keepdims=True)
        num = _dot(pm, vh) + ei * qc
        den = jnp.sum(pm, axis=-1, keepdims=True) + ei * qn
        hh = num / jnp.maximum(jnp.abs(den), jnp.exp(-mt))
        mu = jnp.mean(hh, axis=-1, keepdims=True)
        cen = hh - mu
        var = jnp.mean(cen * cen, axis=-1, keepdims=True)
        outs.append(cen * lax.rsqrt(var + EPS))

        sc = sc_all[:, hl]
        scv = sc * vh
        for b in range(bt):
            rs = slice(b * tc, (b + 1) * tc)
            c_scr[b, h] = (decay3[b, :, hl] * c_old[b]
                           + _dot_tn(scv[rs].astype(BF16), kh[rs].astype(BF16)))
        sck = (sc * kh).reshape(bt, tc, HEAD_DIM)
        n_scr[:, hl, :] = decay3[:, :, hl] * n_old3 + jnp.sum(sck, axis=1, keepdims=True)
    y = jnp.concatenate(outs, axis=-1) * lng_ref[...] * jax.nn.sigmoid(flat(og_ref))
    y_ref[...] = y.reshape(bt, tc, GROUP_W)
    c_ref[...] = c_scr[...]
    n_ref[...] = n_scr[...]


def _mlstm_call(z, c0, n0, m0, w, layer, bt, tc):
    b, l, _ = z.shape
    ws = functools.partial(_layer_spec, layer)
    cb = (RW_COLS + 3 * GROUP_W) // GROUP_W
    cst = _state_spec(bt, HEADS, HEAD_DIM, HEAD_DIM)
    nst = _state_spec(bt, HEADS, HEAD_DIM)
    mst = _state_spec(bt, 1, LANES)
    return pl.pallas_call(
        functools.partial(_mlstm_kernel, bt=bt, tc=tc),
        grid=(b // bt, l // tc),
        in_specs=[_seq_spec(bt, tc, GROUP_W, cb), _seq_spec(bt, tc, GROUP_W, cb + 1),
                  _seq_spec(bt, tc, GROUP_W, cb + 2), _seq_spec(bt, tc, GROUP_W, cb + 3),
                  _seq_spec(bt, tc, LANES, IGATE_COL_BLOCK),
                  _seq_spec(bt, tc, LANES, IGATE_COL_BLOCK + 1),
                  cst, nst, mst, ws(1, LANES), ws(1, LANES), ws(1, GROUP_W)],
        out_specs=[_seq_spec(bt, tc, GROUP_W, 0), cst, nst, mst],
        out_shape=[jax.ShapeDtypeStruct((b, l, GROUP_W), F32),
                   jax.ShapeDtypeStruct((b, HEADS, HEAD_DIM, HEAD_DIM), F32),
                   jax.ShapeDtypeStruct((b, HEADS, HEAD_DIM), F32),
                   jax.ShapeDtypeStruct((b, 1, LANES), F32)],
        scratch_shapes=[pltpu.VMEM((bt, HEADS, HEAD_DIM, HEAD_DIM), F32),
                        pltpu.VMEM((bt, HEADS, HEAD_DIM), F32),
                        pltpu.VMEM((bt, 1, LANES), F32)],
        compiler_params=_params(("parallel", "arbitrary")),
        name="mlstm",
    )(z, z, z, z, z, z, c0, n0, m0, w["ml_bi"], w["ml_bf"], w["ml_ln_g"])


def _block_diag(blocks):
    *lead, n, r, c = blocks.shape
    eye = jnp.eye(n, dtype=blocks.dtype)
    out = eye[:, None, :, None] * blocks[..., :, :, None, :]
    return out.reshape(*lead, n * r, n * c)


def _prep(w):
    depth = w["ffn1_pre"].shape[0]
    row = lambda t: t.reshape(depth, 1, -1).astype(F32)
    out = {}
    for f in ("ffn1", "ffn2"):
        out[f + "_pre"] = row(w[f + "_pre"])
        out[f + "_post"] = row(w[f + "_post"])
        out[f + "_wg"] = _cast_call(w[f + "_wg"], D_MODEL, D_FF_PAD)
        out[f + "_wu"] = _cast_call(w[f + "_wu"], D_MODEL, D_FF_PAD)
        out[f + "_wd"] = _cast_call(w[f + "_wd"], D_FF_PAD, D_MODEL)
    out["mix_pre"] = row(w["mix_pre"])
    out["mix_post"] = row(w["mix_post"])
    out["w_in"] = _cast_call(
        w["w_in"], D_MODEL, IN_COLS_PAD,
        copies=((0, 0, MAIN_COLS), (MAIN_COLS, MAIN_COLS, HEADS),
                (MAIN_COLS + HEADS, MAIN_COLS + LANES, HEADS)))
    out["w_out"] = _cast_call(w["w_out"], D_MODEL, D_MODEL)

    for name in ("rw_mu", "rw_w0", "rw_a0", "rw_kk", "rw_ka", "rw_rk", "rw_ln_g", "rw_ln_b"):
        out[name] = row(w[name])
    for name in ("rw_w2", "rw_a2", "rw_g2"):
        out[name] = w[name].astype(BF16)
    out["head_sum"] = _block_diag(jnp.ones((HEADS, HEAD_DIM, HEAD_DIM), BF16))

    out["lru_conv_w"] = w["lru_conv_w"].astype(F32)
    out["lru_conv_b"] = row(w["lru_conv_b"])
    out["lru_wax"] = jnp.concatenate(
        [_block_diag(w["lru_wa"]), _block_diag(w["lru_wx"])], axis=-1).astype(BF16)
    out["lru_bax"] = jnp.concatenate([row(w["lru_ba"]), row(w["lru_bx"])], axis=-1)
    out["lru_lambda"] = row(w["lru_lambda"])
    out["lru_norm"] = row(w["lru_norm"])

    lr, li = w["s5_a_re"].astype(F32), w["s5_a_im"].astype(F32)
    dt = jnp.exp(w["s5_log_dt"].astype(F32))[..., None]
    mag = jnp.exp(lr * dt)
    ar, ai = mag * jnp.cos(li * dt), mag * jnp.sin(li * dt)
    den = lr * lr + li * li
    zr = ((ar - 1.0) * lr + ai * li) / den
    zi = (ai * lr - (ar - 1.0) * li) / den
    b_re, b_im = w["s5_b_re"].astype(F32), w["s5_b_im"].astype(F32)
    bbr = zr[..., None] * b_re - zi[..., None] * b_im
    bbi = zr[..., None] * b_im + zi[..., None] * b_re
    packed = lambda t: _block_diag(jnp.swapaxes(t, -1, -2))
    out["s5_ar"], out["s5_ai"] = row(ar), row(ai)
    out["s5_bb"] = jnp.concatenate([packed(bbr), packed(bbi)], axis=-1).astype(BF16)
    out["s5_cc"] = jnp.concatenate(
        [packed(w["s5_c_re"]), -packed(w["s5_c_im"])], axis=-2).astype(BF16)
    out["s5_d"] = row(w["s5_d"])
    out["s5_glu_w"] = w["s5_glu_w"].astype(BF16)
    out["s5_glu_b"] = row(w["s5_glu_b"])
    out["s5_norm"] = row(w["s5_norm"])

    lane_block = lambda t: jnp.pad(t.astype(F32), ((0, 0), (0, LANES - HEADS))).reshape(
        depth, 1, LANES)
    out["ml_bi"] = lane_block(w["ml_bi"])
    out["ml_bf"] = lane_block(w["ml_bf"])
    out["ml_ln_g"] = row(w["ml_ln_g"])
    return out


def _tiling(b, l, rows_cap, tc_cap, seq_multiple=1):
    tc = tc_cap
    while tc > SUBLANES and l % tc:
        tc //= 2
    if l % tc or b % seq_multiple:
        raise ValueError(f"unsupported batch {b} / sequence length {l}")
    bt = seq_multiple
    while bt * 2 * tc <= rows_cap and b % (bt * 2) == 0:
        bt *= 2
    return bt, tc


def _layer(x, st, w, layer):
    b, l, d = x.shape
    n = b * l
    shift0, s0, buf0, h0, re0, im0, c0, n0, m0 = st
    x1, z = _ffn_in_call(x.reshape(n, d), w, layer)
    z = z.reshape(b, l, IN_COLS_PAD)

    y_rw, n_shift, n_s = _rwkv_call(z, shift0.reshape(b, 1, RW_COLS), s0, w, layer,
                                    *_tiling(b, l, 256, 64))
    y_lru, n_buf, n_h = _lru_call(z, buf0, h0, w, layer,
                                  *_tiling(b, l, 1024, 128, SUBLANES))
    y_s5, n_re, n_im = _s5_call(z, re0.reshape(b, S5_STATE), im0.reshape(b, S5_STATE), w,
                                layer, *_tiling(b, l, 512, 64, SUBLANES))
    m0p = jnp.pad(m0, ((0, 0), (0, LANES - HEADS))).reshape(b, 1, LANES)
    y_ml, n_c, n_n, n_m = _mlstm_call(z, c0, n0, m0p, w, layer, *_tiling(b, l, 256, 128))

    flat = lambda t: t.reshape(n, GROUP_W)
    x3 = _out_ffn_call(x1, flat(y_rw), flat(y_lru), flat(y_s5), flat(y_ml), w, layer)
    new = (n_shift.reshape(b, RW_COLS), n_s, n_buf, n_h,
           n_re.reshape(b, S5_GROUPS, S5_P), n_im.reshape(b, S5_GROUPS, S5_P), n_c, n_n,
           n_m.reshape(b, LANES)[:, :HEADS])
    return x3.reshape(b, l, d), new


def _zero_state(n):
    return (jnp.zeros((n, RW_COLS), F32), jnp.zeros((n, HEADS, HEAD_DIM, HEAD_DIM), F32),
            jnp.zeros((n, CONV_W - 1, GROUP_W), F32), jnp.zeros((n, GROUP_W), F32),
            jnp.zeros((n, S5_GROUPS, S5_P), F32), jnp.zeros((n, S5_GROUPS, S5_P), F32),
            jnp.zeros((n, HEADS, HEAD_DIM, HEAD_DIM), F32), jnp.zeros((n, HEADS, HEAD_DIM), F32),
            jnp.zeros((n, HEADS), F32))


def kernel(x_prompt, x_sample, state_rwkv_shift, state_rwkv_wkv, state_lru_conv, state_lru_h, state_s5_re, state_s5_im, state_mlstm_C, state_mlstm_n, state_mlstm_m, ffn1_pre, ffn1_wg, ffn1_wu, ffn1_wd, ffn1_post, mix_pre, w_in, w_out, mix_post, ffn2_pre, ffn2_wg, ffn2_wu, ffn2_wd, ffn2_post, rw_mu, rw_w0, rw_w2, rw_a0, rw_a2, rw_g2, rw_kk, rw_ka, rw_rk, rw_ln_g, rw_ln_b, lru_conv_w, lru_conv_b, lru_wa, lru_ba, lru_wx, lru_bx, lru_lambda, lru_norm, s5_a_re, s5_a_im, s5_log_dt, s5_b_re, s5_b_im, s5_c_re, s5_c_im, s5_d, s5_glu_w, s5_glu_b, s5_norm, ml_bi, ml_bf, ml_ln_g):
    w = _prep(dict(
        ffn1_pre=ffn1_pre, ffn1_wg=ffn1_wg, ffn1_wu=ffn1_wu, ffn1_wd=ffn1_wd, ffn1_post=ffn1_post,
        mix_pre=mix_pre, w_in=w_in, w_out=w_out, mix_post=mix_post,
        ffn2_pre=ffn2_pre, ffn2_wg=ffn2_wg, ffn2_wu=ffn2_wu, ffn2_wd=ffn2_wd, ffn2_post=ffn2_post,
        rw_mu=rw_mu, rw_w0=rw_w0, rw_w2=rw_w2, rw_a0=rw_a0, rw_a2=rw_a2, rw_g2=rw_g2,
        rw_kk=rw_kk, rw_ka=rw_ka, rw_rk=rw_rk, rw_ln_g=rw_ln_g, rw_ln_b=rw_ln_b,
        lru_conv_w=lru_conv_w, lru_conv_b=lru_conv_b, lru_wa=lru_wa, lru_ba=lru_ba,
        lru_wx=lru_wx, lru_bx=lru_bx, lru_lambda=lru_lambda, lru_norm=lru_norm,
        s5_a_re=s5_a_re, s5_a_im=s5_a_im, s5_log_dt=s5_log_dt, s5_b_re=s5_b_re, s5_b_im=s5_b_im,
        s5_c_re=s5_c_re, s5_c_im=s5_c_im, s5_d=s5_d, s5_glu_w=s5_glu_w, s5_glu_b=s5_glu_b,
        s5_norm=s5_norm, ml_bi=ml_bi, ml_bf=ml_bf, ml_ln_g=ml_ln_g))
    sample_state = (state_rwkv_shift, state_rwkv_wkv, state_lru_conv, state_lru_h,
                    state_s5_re, state_s5_im, state_mlstm_C, state_mlstm_n, state_mlstm_m)
    depth = ffn1_pre.shape[0]
    y_p, y_s = x_prompt.astype(F32), x_sample.astype(F32)
    outs_p, outs_s = [], []
    for layer in range(depth):
        y_p, st_p = _layer(y_p, _zero_state(x_prompt.shape[0]), w, layer)
        y_s, st_s = _layer(y_s, tuple(s[layer].astype(F32) for s in sample_state), w, layer)
        outs_p.append(st_p)
        outs_s.append(st_s)
    p_states = [jnp.stack([o[i] for o in outs_p]) for i in range(9)]
    s_states = [jnp.stack([o[i] for o in outs_s]) for i in range(9)]
    return (y_p, y_s, *p_states, *s_states)
```

```python
import functools

import jax
import jax.numpy as jnp
from jax import lax
from jax.experimental import pallas as pl
from jax.experimental.pallas import tpu as pltpu

F32 = jnp.float32
BF16 = jnp.bfloat16

LANES = 128
SUBLANES = 8
VMEM_LIMIT_BYTES = 56 * 1024 * 1024

D_MODEL = 1024
GROUP_W = D_MODEL // 4
HEAD_DIM = 64
HEADS = GROUP_W // HEAD_DIM
RW_W_RANK = 64
RW_A_RANK = 64
RW_G_RANK = 128
RW_COLS = 3 * GROUP_W + RW_W_RANK + RW_A_RANK + RW_G_RANK
RW_DECAY_SCALE = 0.606531
RW_GN_EPS = 64e-5
LRU_C = 8.0
CONV_W = 4
S5_GROUP = 16
S5_GROUPS = GROUP_W // S5_GROUP
S5_P = 64
S5_STATE = S5_GROUPS * S5_P
MAIN_COLS = RW_COLS + 7 * GROUP_W
IN_COLS = MAIN_COLS + 2 * HEADS
IN_COLS_PAD = MAIN_COLS + 2 * LANES
IGATE_COL_BLOCK = MAIN_COLS // LANES
D_FF = 2752
D_FF_PAD = -(-D_FF // LANES) * LANES
EPS = 1e-6
HI = lax.Precision.HIGHEST


def _dot(a, b):
    return jnp.dot(a.astype(BF16), b.astype(BF16), preferred_element_type=F32)


def _dot_nt(a, b, precision=None):
    return lax.dot_general(a, b, (((1,), (1,)), ((), ())), precision=precision,
                           preferred_element_type=F32)


def _dot_tn(a, b):
    return lax.dot_general(a, b, (((0,), (0,)), ((), ())), preferred_element_type=F32)


def _dot_hi(a, b):
    return jnp.dot(a, b, precision=HI, preferred_element_type=F32)


def _rms(x, g):
    return x * lax.rsqrt(jnp.mean(x * x, axis=-1, keepdims=True) + EPS) * g


def _seq_masks(bt, tc, reps=1):
    rows = bt * tc
    r = lax.broadcasted_iota(jnp.int32, (rows, reps * rows), 0)
    c = lax.broadcasted_iota(jnp.int32, (rows, reps * rows), 1) & (rows - 1)
    if bt == 1:
        return r > c, r >= c
    start = r - (r & (tc - 1))
    return (c < r) & (c >= start), (c <= r) & (c >= start)


def _rows(x3, tc):
    bt, _, w = x3.shape
    return jnp.broadcast_to(x3, (bt, tc, w)).reshape(bt * tc, w)


def _const_spec(shape):
    nd = len(shape)
    return pl.BlockSpec(shape, lambda *_: (0,) * nd, pipeline_mode=pl.Buffered(1))


def _layer_spec(layer, *shape):
    zeros = (0,) * len(shape)
    return pl.BlockSpec((None,) + shape, lambda *_: (layer,) + zeros,
                        pipeline_mode=pl.Buffered(1))


def _params(sem):
    return pltpu.CompilerParams(dimension_semantics=sem, vmem_limit_bytes=VMEM_LIMIT_BYTES)


def _cast_kernel(x_ref, o_ref):
    rows, cols = x_ref.shape
    o_ref[...] = jnp.zeros(o_ref.shape, BF16)
    o_ref[0:rows, 0:cols] = x_ref[...].astype(BF16)


def _cast_call(x, out_rows, out_cols):
    depth, rows, cols = x.shape
    return pl.pallas_call(
        _cast_kernel,
        grid=(depth,),
        in_specs=[pl.BlockSpec((None, rows, cols), lambda i: (i, 0, 0))],
        out_specs=pl.BlockSpec((None, out_rows, out_cols), lambda i: (i, 0, 0)),
        out_shape=jax.ShapeDtypeStruct((depth, out_rows, out_cols), BF16),
        compiler_params=_params(("parallel",)),
        name="cast_weights",
    )(x)


def _cast_t_kernel(x_ref, o_ref, *, segments):
    cols = x_ref.shape[1]
    for src, n, dst in segments:
        chunk = x_ref[src:src + n, :]
        if n < LANES:
            chunk = jnp.concatenate([chunk, jnp.zeros((LANES - n, cols), chunk.dtype)], axis=0)
        o_ref[:, dst:dst + LANES] = chunk.T.astype(BF16)


def _row_chunks(n):
    return tuple((i, min(LANES, n - i), i) for i in range(0, n, LANES))


def _cast_t_call(x_t, out_cols, segments):
    depth, rows, cols = x_t.shape
    assert len(segments) * LANES == out_cols
    return pl.pallas_call(
        functools.partial(_cast_t_kernel, segments=segments),
        grid=(depth,),
        in_specs=[pl.BlockSpec((None, rows, cols), lambda i: (i, 0, 0))],
        out_specs=pl.BlockSpec((None, cols, out_cols), lambda i: (i, 0, 0)),
        out_shape=jax.ShapeDtypeStruct((depth, cols, out_cols), BF16),
        compiler_params=_params(("parallel",)),
        name="cast_weights_t",
    )(x_t)


def _swiglu_half_step(x, pre, wg, wu, wd, post):
    h = _rms(x, pre).astype(BF16)
    g = jnp.dot(h, wg, preferred_element_type=F32)
    u = jnp.dot(h, wu, preferred_element_type=F32)
    act = (jax.nn.silu(g) * u).astype(BF16)
    f = jnp.dot(act, wd, preferred_element_type=F32)
    return x + 0.5 * _rms(f, post)


def _ffn_in_kernel(x_ref, pre_ref, wg_ref, wu_ref, wd_ref, post_ref, mpre_ref, win_ref,
                   xo_ref, z_ref):
    x1 = _swiglu_half_step(x_ref[...], pre_ref[...], wg_ref[...], wu_ref[...], wd_ref[...],
                           post_ref[...])
    xo_ref[...] = x1
    z_ref[...] = jnp.dot(_rms(x1, mpre_ref[...]).astype(BF16), win_ref[...],
                         preferred_element_type=F32)


def _out_ffn_kernel(x_ref, yrw_ref, ylru_ref, ys5_ref, yml_ref, wout_ref, mpost_ref,
                    pre_ref, wg_ref, wu_ref, wd_ref, post_ref, xo_ref):
    cat = jnp.concatenate([yrw_ref[...], ylru_ref[...], ys5_ref[...], yml_ref[...]], axis=-1)
    m = jnp.dot(cat.astype(BF16), wout_ref[...], preferred_element_type=F32)
    x2 = x_ref[...] + _rms(m, mpost_ref[...])
    xo_ref[...] = _swiglu_half_step(x2, pre_ref[...], wg_ref[...], wu_ref[...], wd_ref[...],
                                    post_ref[...])


def _token_tile(n):
    for tm in (512, 256, 128, 64, 32, 16, 8):
        if n % tm == 0:
            return tm
    raise ValueError(f"token count {n} is not a multiple of {SUBLANES}")


def _ffn_specs(ws):
    return [ws(1, D_MODEL), ws(D_MODEL, D_FF_PAD), ws(D_MODEL, D_FF_PAD), ws(D_FF_PAD, D_MODEL),
            ws(1, D_MODEL)]


def _ffn_in_call(x, w, layer):
    n = x.shape[0]
    tm = _token_tile(n)
    row = lambda width: pl.BlockSpec((tm, width), lambda i: (i, 0))
    ws = functools.partial(_layer_spec, layer)
    return pl.pallas_call(
        _ffn_in_kernel,
        grid=(n // tm,),
        in_specs=[row(D_MODEL)] + _ffn_specs(ws) + [ws(1, D_MODEL), ws(D_MODEL, IN_COLS_PAD)],
        out_specs=[row(D_MODEL), row(IN_COLS_PAD)],
        out_shape=[jax.ShapeDtypeStruct((n, D_MODEL), F32),
                   jax.ShapeDtypeStruct((n, IN_COLS_PAD), F32)],
        compiler_params=_params(("parallel",)),
        name="ffn1_inproj",
    )(x, w["ffn1_pre"], w["ffn1_wg"], w["ffn1_wu"], w["ffn1_wd"], w["ffn1_post"], w["mix_pre"],
      w["w_in"])


def _out_ffn_call(x, yrw, ylru, ys5, yml, w, layer):
    n = x.shape[0]
    tm = _token_tile(n)
    row = lambda width: pl.BlockSpec((tm, width), lambda i: (i, 0))
    ws = functools.partial(_layer_spec, layer)
    return pl.pallas_call(
        _out_ffn_kernel,
        grid=(n // tm,),
        in_specs=[row(D_MODEL), row(GROUP_W), row(GROUP_W), row(GROUP_W), row(GROUP_W),
                  ws(D_MODEL, D_MODEL), ws(1, D_MODEL)] + _ffn_specs(ws),
        out_specs=row(D_MODEL),
        out_shape=jax.ShapeDtypeStruct((n, D_MODEL), F32),
        compiler_params=_params(("parallel",)),
        name="outproj_ffn2",
    )(x, yrw, ylru, ys5, yml, w["w_out"], w["mix_post"], w["ffn2_pre"], w["ffn2_wg"],
      w["ffn2_wu"], w["ffn2_wd"], w["ffn2_post"])


def _seq_spec(bt, tc, w, col_block):
    return pl.BlockSpec((bt, tc, w), lambda i, c: (i, c, col_block))


def _state_spec(bt, *dims):
    zeros = (0,) * len(dims)
    return pl.BlockSpec((bt,) + dims, lambda i, c: (i,) + zeros)


def _stacked_state_spec(layer, bt, *dims):
    zeros = (0,) * len(dims)
    return pl.BlockSpec((None, bt) + dims, lambda i, c: (layer, i) + zeros)


ANY_SPEC = pl.BlockSpec(memory_space=pl.ANY)


def _rwkv_kernel(p_ref, shift0_ref, s0_ref, mu_ref, w0_ref, w2_ref, a0_ref, a2_ref, g2_ref,
                 kkw_ref, ka_ref, rk_ref, lng_ref, lnb_ref, hsum_ref, _s_all_ref,
                 y_ref, shift_ref, s_ref, pbuf, s_scr, *, bt, tc):
    c = pl.program_id(1)
    rows = bt * tc
    lo = SUBLANES - 1

    @pl.when(c == 0)
    def _():
        pbuf[:, lo:SUBLANES, :] = shift0_ref[...]
        s_scr[...] = s0_ref[...]

    p3 = p_ref[...]
    pbuf[:, SUBLANES:SUBLANES + tc, :] = p3
    prev = pbuf[:, lo:lo + tc, :].reshape(rows, RW_COLS)
    last = p3[:, tc - 1:tc, :]
    pbuf[:, lo:SUBLANES, :] = last
    shift_ref[...] = last
    p = p3.reshape(rows, RW_COLS)

    xm = p + (prev - p) * mu_ref[...]
    g = GROUP_W
    r, k, v = xm[:, 0:g], xm[:, g:2 * g], xm[:, 2 * g:3 * g]
    lw = xm[:, 3 * g:3 * g + RW_W_RANK]
    la = xm[:, 3 * g + RW_W_RANK:3 * g + RW_W_RANK + RW_A_RANK]
    lg = xm[:, 3 * g + RW_W_RANK + RW_A_RANK:]
    logw = -RW_DECAY_SCALE * jax.nn.sigmoid(w0_ref[...] + _dot(jnp.tanh(lw), w2_ref[...]))
    a = jax.nn.sigmoid(a0_ref[...] + _dot(la, a2_ref[...]))
    gate = _dot(jax.nn.sigmoid(lg), g2_ref[...])

    hsum = hsum_ref[...]
    kk = k * kkw_ref[...]
    kk = kk * lax.rsqrt(jnp.maximum(_dot(kk * kk, hsum), 1e-12))
    k2 = k * (1.0 + (a - 1.0) * ka_ref[...])

    _, incl = _seq_masks(bt, tc)
    strict2, incl2 = _seq_masks(bt, tc, reps=2)
    cum = _dot_hi(incl.astype(F32), logw)
    cum3 = cum.reshape(bt, tc, g)
    total3 = cum3[:, tc - 1:tc, :]
    rest = jnp.exp(total3 - cum3).reshape(rows, g)
    wtot3 = jnp.exp(total3)
    inv = jnp.exp(-cum)
    kka = kk * a
    at = -kk * jnp.exp(cum - logw)
    rt = r * jnp.exp(cum)
    bk_t = kka * inv, k2 * inv
    bk_w = kka * rest, k2 * rest

    n_double = max(1, (tc - 1).bit_length())
    outs = []
    for h in range(HEADS):
        sl = slice(h * HEAD_DIM, (h + 1) * HEAD_DIM)
        ah, rh, vh = at[:, sl], rt[:, sl], v[:, sl]
        bkh = jnp.concatenate([bk_t[0][:, sl], bk_t[1][:, sl]], axis=0).astype(BF16)
        bkw = jnp.concatenate([bk_w[0][:, sl], bk_w[1][:, sl]], axis=0)
        m_a = jnp.where(strict2, _dot_nt(ah.astype(BF16), bkh), 0.0)
        m_r = jnp.where(incl2, _dot_nt(rh.astype(BF16), bkh), 0.0)
        s_old = [s_scr[b, h] for b in range(bt)]
        from_state = []
        for b in range(bt):
            rs = slice(b * tc, (b + 1) * tc)
            ar = jnp.concatenate([ah[rs], rh[rs]], axis=0)
            from_state.append(_dot_nt(ar.astype(BF16), s_old[b].astype(BF16)))
        u = (jnp.concatenate([fs[:tc] for fs in from_state], axis=0)
             + _dot(m_a[:, rows:], vh))
        pw = m_a[:, :rows]
        for i in range(n_double):
            u = u + _dot(pw, u)
            if i + 1 < n_double:
                pw = _dot(pw, pw)
        uv = jnp.concatenate([u, vh], axis=0)
        outs.append(jnp.concatenate([fs[tc:] for fs in from_state], axis=0) + _dot(m_r, uv))
        for b in range(bt):
            rs = slice(b * tc, (b + 1) * tc)
            uv_b = jnp.concatenate([uv[rs], uv[rows + b * tc:rows + (b + 1) * tc]], axis=0)
            bkw_b = jnp.concatenate([bkw[rs], bkw[rows + b * tc:rows + (b + 1) * tc]], axis=0)
            s_scr[b, h] = (s_old[b] * wtot3[b, :, sl]
                           + _dot_tn(uv_b.astype(BF16), bkw_b.astype(BF16)))
    o = jnp.concatenate(outs, axis=-1)

    mean = _dot(o, hsum) * (1.0 / HEAD_DIM)
    cen = o - mean
    var = _dot(cen * cen, hsum) * (1.0 / HEAD_DIM)
    y = cen * lax.rsqrt(var + RW_GN_EPS) * lng_ref[...] + lnb_ref[...]
    bonus = _dot(r * k2 * rk_ref[...], hsum) * v
    y_ref[...] = ((y + bonus) * gate).reshape(bt, tc, g)
    s_ref[...] = s_scr[...]


def _rwkv_call(z, shift0, s0, s_all, w, layer, state_layer, bt, tc):
    b, l, _ = z.shape
    ws = functools.partial(_layer_spec, layer)
    mat = (HEADS, HEAD_DIM, HEAD_DIM)
    return pl.pallas_call(
        functools.partial(_rwkv_kernel, bt=bt, tc=tc),
        grid=(b // bt, l // tc),
        in_specs=[_seq_spec(bt, tc, RW_COLS, 0),
                  _stacked_state_spec(state_layer, bt, 1, RW_COLS),
                  _stacked_state_spec(state_layer, bt, *mat),
                  ws(1, RW_COLS), ws(1, GROUP_W), ws(RW_W_RANK, GROUP_W), ws(1, GROUP_W),
                  ws(RW_A_RANK, GROUP_W), ws(RW_G_RANK, GROUP_W),
                  ws(1, GROUP_W), ws(1, GROUP_W), ws(1, GROUP_W), ws(1, GROUP_W),
                  ws(1, GROUP_W), _const_spec((GROUP_W, GROUP_W)), ANY_SPEC],
        out_specs=[_seq_spec(bt, tc, GROUP_W, 0), _state_spec(bt, 1, RW_COLS),
                   _stacked_state_spec(layer, bt, *mat)],
        out_shape=[jax.ShapeDtypeStruct((b, l, GROUP_W), F32),
                   jax.ShapeDtypeStruct((b, 1, RW_COLS), F32),
                   jax.ShapeDtypeStruct(s_all.shape, F32)],
        input_output_aliases={15: 2},
        scratch_shapes=[pltpu.VMEM((bt, SUBLANES + tc, RW_COLS), F32),
                        pltpu.VMEM((bt,) + mat, F32)],
        compiler_params=_params(("parallel", "arbitrary")),
        name="rwkv7",
    )(z, shift0, s0, w["rw_mu"], w["rw_w0"], w["rw_w2"], w["rw_a0"], w["rw_a2"], w["rw_g2"],
      w["rw_kk"], w["rw_ka"], w["rw_rk"], w["rw_ln_g"], w["rw_ln_b"], w["head_sum"], s_all)


def _time_slab(t, g, tc):
    return pl.ds(g * SUBLANES * tc + t, SUBLANES, stride=tc)


def _lane_tiles(x):
    return [x[:, k * LANES:(k + 1) * LANES] for k in range(x.shape[-1] // LANES)]


def _lru_kernel(x_ref, gate_ref, buf0_ref, h0_ref, cw_ref, cb_ref, wax_ref, bax_ref, lam_ref,
                norm_ref, y_ref, buf_ref, h_ref, xbuf, abuf, bbuf, h_scr, *, bt, tc):
    c = pl.program_id(1)
    rows = bt * tc
    lo = SUBLANES - (CONV_W - 1)
    n_tiles = GROUP_W // LANES

    @pl.when(c == 0)
    def _():
        xbuf[:, lo:SUBLANES, :] = buf0_ref[...]
        h_scr[...] = h0_ref[...]

    x3 = x_ref[...]
    xbuf[:, SUBLANES:SUBLANES + tc, :] = x3
    cw = cw_ref[...]
    xc3 = cb_ref[...]
    for j in range(CONV_W):
        xc3 = xc3 + xbuf[:, lo + j:lo + j + tc, :] * cw[j:j + 1, :]
    tail = x3[:, tc - (CONV_W - 1):tc, :]
    xbuf[:, lo:SUBLANES, :] = tail
    buf_ref[...] = tail
    xc = xc3.reshape(rows, GROUP_W)

    gates = jax.nn.sigmoid(_dot(xc, wax_ref[...]) + bax_ref[...])
    gate_r, gate_i = gates[:, :GROUP_W], gates[:, GROUP_W:]
    log_a = -LRU_C * gate_r * jax.nn.softplus(-lam_ref[...])
    a = jnp.exp(log_a)
    bb = jnp.sqrt(-jnp.tanh(log_a) * (a * a + 1.0)) * (gate_i * xc)
    for k, (a_k, b_k) in enumerate(zip(_lane_tiles(a), _lane_tiles(bb))):
        abuf[k] = a_k
        bbuf[k] = b_k

    for g in range(bt // SUBLANES):
        seqs = slice(g * SUBLANES, (g + 1) * SUBLANES)
        h = _lane_tiles(h_scr[seqs, :])
        for t in range(tc):
            slab = _time_slab(t, g, tc)
            for k in range(n_tiles):
                h[k] = abuf[k, slab, :] * h[k] + bbuf[k, slab, :]
                bbuf[k, slab, :] = h[k]
        h_scr[seqs, :] = jnp.concatenate(h, axis=-1)
    h_ref[...] = h_scr[...]
    h_all = jnp.concatenate([bbuf[k] for k in range(n_tiles)], axis=-1)
    gate = gate_ref[...].reshape(rows, GROUP_W)
    y_ref[...] = _rms(h_all * jax.nn.gelu(gate), norm_ref[...]).reshape(bt, tc, GROUP_W)


def _lru_call(z, buf0, h0, w, layer, state_layer, bt, tc):
    b, l, _ = z.shape
    rows = bt * tc
    ws = functools.partial(_layer_spec, layer)
    cb = RW_COLS // GROUP_W
    tiles = pltpu.VMEM((GROUP_W // LANES, rows, LANES), F32)
    return pl.pallas_call(
        functools.partial(_lru_kernel, bt=bt, tc=tc),
        grid=(b // bt, l // tc),
        in_specs=[_seq_spec(bt, tc, GROUP_W, cb), _seq_spec(bt, tc, GROUP_W, cb + 1),
                  _stacked_state_spec(state_layer, bt, CONV_W - 1, GROUP_W),
                  _stacked_state_spec(state_layer, bt, GROUP_W),
                  ws(CONV_W, GROUP_W), ws(1, GROUP_W), ws(GROUP_W, 2 * GROUP_W),
                  ws(1, 2 * GROUP_W), ws(1, GROUP_W), ws(1, GROUP_W)],
        out_specs=[_seq_spec(bt, tc, GROUP_W, 0), _state_spec(bt, CONV_W - 1, GROUP_W),
                   _state_spec(bt, GROUP_W)],
        out_shape=[jax.ShapeDtypeStruct((b, l, GROUP_W), F32),
                   jax.ShapeDtypeStruct((b, CONV_W - 1, GROUP_W), F32),
                   jax.ShapeDtypeStruct((b, GROUP_W), F32)],
        scratch_shapes=[pltpu.VMEM((bt, SUBLANES + tc, GROUP_W), F32), tiles, tiles,
                        pltpu.VMEM((bt, GROUP_W), F32)],
        compiler_params=_params(("parallel", "arbitrary")),
        name="rglru",
    )(z, z, buf0, h0, w["lru_conv_w"], w["lru_conv_b"], w["lru_wax"], w["lru_bax"],
      w["lru_lambda"], w["lru_norm"])


def _s5_kernel(ulo_ref, uhi_ref, re0_ref, im0_ref, ar_ref, ai_ref, bb_ref, cc_ref, d_ref, gw_ref,
               gb_ref, norm_ref, y_ref, re_ref, im_ref, rbuf, ibuf, ybuf, re_scr, im_scr, *,
               bt, tc):
    c = pl.program_id(1)
    n_tiles = S5_STATE // LANES
    groups = bt // SUBLANES

    @pl.when(c == 0)
    def _():
        re_scr[...] = re0_ref[...]
        im_scr[...] = im0_ref[...]

    def time_major(ref):
        return jnp.concatenate([ref[g * SUBLANES:(g + 1) * SUBLANES, t, :]
                                for g in range(groups) for t in range(tc)], axis=0)

    u = jnp.concatenate([time_major(ulo_ref), time_major(uhi_ref)], axis=-1)
    bu = _dot(u, bb_ref[...])
    for k, (r_k, i_k) in enumerate(zip(_lane_tiles(bu[:, :S5_STATE]),
                                       _lane_tiles(bu[:, S5_STATE:]))):
        rbuf[k] = r_k
        ibuf[k] = i_k
    slab_shape = (SUBLANES, LANES)
    a_r = [jnp.broadcast_to(t, slab_shape) for t in _lane_tiles(ar_ref[...])]
    a_i = [jnp.broadcast_to(t, slab_shape) for t in _lane_tiles(ai_ref[...])]

    for g in range(groups):
        seqs = slice(g * SUBLANES, (g + 1) * SUBLANES)
        s_r, s_i = _lane_tiles(re_scr[seqs, :]), _lane_tiles(im_scr[seqs, :])
        for t in range(tc):
            slab = pl.ds((g * tc + t) * SUBLANES, SUBLANES)
            for k in range(n_tiles):
                n_r = a_r[k] * s_r[k] - a_i[k] * s_i[k] + rbuf[k, slab, :]
                n_i = a_r[k] * s_i[k] + a_i[k] * s_r[k] + ibuf[k, slab, :]
                s_r[k], s_i[k] = n_r, n_i
                rbuf[k, slab, :] = n_r
                ibuf[k, slab, :] = n_i
        re_scr[seqs, :] = jnp.concatenate(s_r, axis=-1)
        im_scr[seqs, :] = jnp.concatenate(s_i, axis=-1)
    re_ref[...] = re_scr[...]
    im_ref[...] = im_scr[...]

    x_all = jnp.concatenate([rbuf[k] for k in range(n_tiles)]
                            + [ibuf[k] for k in range(n_tiles)], axis=-1)
    y = _dot(x_all, cc_ref[...])
    y = jax.nn.gelu(y + d_ref[...] * u)
    y = y * jax.nn.sigmoid(_dot(y, gw_ref[...]) + gb_ref[...])
    for k, y_k in enumerate(_lane_tiles(_rms(y, norm_ref[...]))):
        ybuf[k] = y_k
    for g in range(groups):
        for j in range(SUBLANES):
            seq_rows = pl.ds(g * tc * SUBLANES + j, tc, stride=SUBLANES)
            y_ref[g * SUBLANES + j] = jnp.concatenate(
                [ybuf[k, seq_rows, :] for k in range(GROUP_W // LANES)], axis=-1)


def _s5_call(z, re0, im0, w, layer, state_layer, bt, tc):
    b, l, _ = z.shape
    rows = bt * tc
    ws = functools.partial(_layer_spec, layer)
    cb = (RW_COLS + 2 * GROUP_W) // LANES
    state = _state_spec(bt, S5_STATE)
    state_in = _stacked_state_spec(state_layer, bt, S5_STATE)
    tiles = pltpu.VMEM((S5_STATE // LANES, rows, LANES), F32)
    return pl.pallas_call(
        functools.partial(_s5_kernel, bt=bt, tc=tc),
        grid=(b // bt, l // tc),
        in_specs=[_seq_spec(bt, tc, LANES, cb), _seq_spec(bt, tc, LANES, cb + 1),
                  state_in, state_in,
                  ws(1, S5_STATE), ws(1, S5_STATE), ws(GROUP_W, 2 * S5_STATE),
                  ws(2 * S5_STATE, GROUP_W), ws(1, GROUP_W), ws(GROUP_W, GROUP_W),
                  ws(1, GROUP_W), ws(1, GROUP_W)],
        out_specs=[_seq_spec(bt, tc, GROUP_W, 0), state, state],
        out_shape=[jax.ShapeDtypeStruct((b, l, GROUP_W), F32),
                   jax.ShapeDtypeStruct((b, S5_STATE), F32),
                   jax.ShapeDtypeStruct((b, S5_STATE), F32)],
        scratch_shapes=[tiles, tiles, pltpu.VMEM((GROUP_W // LANES, rows, LANES), F32),
                        pltpu.VMEM((bt, S5_STATE), F32), pltpu.VMEM((bt, S5_STATE), F32)],
        compiler_params=_params(("parallel", "arbitrary")),
        name="s5",
    )(z, z, re0, im0, w["s5_ar"], w["s5_ai"], w["s5_bb"], w["s5_cc"], w["s5_d"], w["s5_glu_w"],
      w["s5_glu_b"], w["s5_norm"])


def _seq_cummax(x, bt, tc, buf):
    rows = bt * tc
    rin = lax.broadcasted_iota(jnp.int32, (rows, 1), 0) & (tc - 1)
    d = 1
    while d < tc:
        if d < SUBLANES:
            buf[SUBLANES:SUBLANES + rows, :] = x
            moved = buf[SUBLANES - d:SUBLANES - d + rows, :]
        else:
            moved = jnp.concatenate([x[:d], x[:rows - d]], axis=0)
        x = jnp.maximum(x, jnp.where(rin >= d, moved, -jnp.inf))
        d *= 2
    return x


def _mlstm_kernel(q_ref, k_ref, v_ref, og_ref, ig_ref, fg_ref, c0_ref, n0_ref, m0_ref, bi_ref,
                  bf_ref, lng_ref, hsum_ref, _c_all_ref, y_ref, c_ref, n_ref, m_ref,
                  c_scr, n_scr, m_scr, gbuf, *, bt, tc):
    c = pl.program_id(1)
    rows = bt * tc

    @pl.when(c == 0)
    def _():
        c_scr[...] = c0_ref[...]
        n_scr[...] = n0_ref[...]
        m_scr[...] = m0_ref[...]
        gbuf[0:SUBLANES, :] = jnp.zeros((SUBLANES, LANES), F32)

    flat = lambda ref: ref[...].reshape(rows, ref.shape[-1])
    log_i = flat(ig_ref) + bi_ref[...]
    log_f = jax.nn.log_sigmoid(flat(fg_ref) + bf_ref[...])
    _, incl = _seq_masks(bt, tc)
    bcum = _dot_hi(incl.astype(F32), log_f)
    m3 = m_scr[...]
    bcum3 = bcum.reshape(bt, tc, LANES)
    b_last3 = bcum3[:, tc - 1:tc, :]
    inter_all = (bcum3 + m3).reshape(rows, LANES)
    wk3 = b_last3 - bcum3 + log_i.reshape(bt, tc, LANES)
    m_new3 = jnp.maximum(b_last3 + m3, jnp.max(wk3, axis=1, keepdims=True))
    sc_all = jnp.exp(wk3 - m_new3).reshape(rows, LANES)
    decay3 = jnp.exp(b_last3 + m3 - m_new3)
    m_scr[...] = m_new3
    m_ref[...] = m_new3

    src = log_i - bcum
    mt_all = jnp.maximum(bcum + _seq_cummax(src, bt, tc, gbuf), inter_all)
    col_all = bcum - mt_all
    ei_all = jnp.exp(inter_all - mt_all)
    floor_all = jnp.exp(-mt_all)
    eye = (lax.broadcasted_iota(jnp.int32, (SUBLANES, LANES), 0)
           == lax.broadcasted_iota(jnp.int32, (SUBLANES, LANES), 1)).astype(F32)
    src_rows = _dot_nt(eye, src, HI)

    q, k, v = flat(q_ref), flat(k_ref) * (HEAD_DIM ** -0.5), flat(v_ref)
    ones = jnp.ones((rows, HEAD_DIM), F32)
    pad = jnp.zeros((HEAD_DIM - SUBLANES, HEAD_DIM), F32)
    outs = []
    for h in range(HEADS):
        sl = slice(h * HEAD_DIM, (h + 1) * HEAD_DIM)
        hl = slice(h, h + 1)
        qh, kh, vh = q[:, sl], k[:, sl], v[:, sl]
        pm = (jnp.where(incl, jnp.exp(col_all[:, hl] + src_rows[hl, :]), 0.0)
              * _dot_nt(qh.astype(BF16), kh.astype(BF16)))
        intra = _dot(pm, jnp.concatenate([vh, ones], axis=-1))
        c_old = [c_scr[b, h] for b in range(bt)]
        from_state = []
        for b in range(bt):
            n_old = jnp.broadcast_to(n_scr[b, hl, :], (SUBLANES, HEAD_DIM))
            cn = jnp.concatenate([c_old[b], n_old, pad], axis=0)
            from_state.append(_dot_nt(qh[b * tc:(b + 1) * tc].astype(BF16), cn.astype(BF16)))
        nd = intra + ei_all[:, hl] * jnp.concatenate(from_state, axis=0)
        den = nd[:, HEAD_DIM:HEAD_DIM + 1]
        outs.append(nd[:, :HEAD_DIM] / jnp.maximum(jnp.abs(den), floor_all[:, hl]))

        sc = sc_all[:, hl]
        scv = sc * vh
        for b in range(bt):
            rs = slice(b * tc, (b + 1) * tc)
            c_scr[b, h] = (decay3[b, :, hl] * c_old[b]
                           + _dot_tn(scv[rs].astype(BF16), kh[rs].astype(BF16)))
        sck = (sc * kh).reshape(bt, tc, HEAD_DIM)
        n_scr[:, hl, :] = (decay3[:, :, hl] * n_scr[:, hl, :]
                           + jnp.sum(sck, axis=1, keepdims=True))
    o = jnp.concatenate(outs, axis=-1)
    hsum = hsum_ref[...]
    cen = o - _dot(o, hsum) * (1.0 / HEAD_DIM)
    var = _dot(cen * cen, hsum) * (1.0 / HEAD_DIM)
    y = cen * lax.rsqrt(var + EPS) * lng_ref[...] * jax.nn.sigmoid(flat(og_ref))
    y_ref[...] = y.reshape(bt, tc, GROUP_W)
    c_ref[...] = c_scr[...]
    n_ref[...] = n_scr[...]


def _mlstm_call(z, c0, n0, m0, c_all, w, layer, state_layer, bt, tc):
    b, l, _ = z.shape
    ws = functools.partial(_layer_spec, layer)
    cb = (RW_COLS + 3 * GROUP_W) // GROUP_W
    mat = (HEADS, HEAD_DIM, HEAD_DIM)
    stacked = functools.partial(_stacked_state_spec, state_layer, bt)
    return pl.pallas_call(
        functools.partial(_mlstm_kernel, bt=bt, tc=tc),
        grid=(b // bt, l // tc),
        in_specs=[_seq_spec(bt, tc, GROUP_W, cb), _seq_spec(bt, tc, GROUP_W, cb + 1),
                  _seq_spec(bt, tc, GROUP_W, cb + 2), _seq_spec(bt, tc, GROUP_W, cb + 3),
                  _seq_spec(bt, tc, LANES, IGATE_COL_BLOCK),
                  _seq_spec(bt, tc, LANES, IGATE_COL_BLOCK + 1),
                  stacked(*mat), stacked(HEADS, HEAD_DIM), stacked(1, LANES),
                  ws(1, LANES), ws(1, LANES), ws(1, GROUP_W),
                  _const_spec((GROUP_W, GROUP_W)), ANY_SPEC],
        out_specs=[_seq_spec(bt, tc, GROUP_W, 0), _stacked_state_spec(layer, bt, *mat),
                   _state_spec(bt, HEADS, HEAD_DIM), _state_spec(bt, 1, LANES)],
        out_shape=[jax.ShapeDtypeStruct((b, l, GROUP_W), F32),
                   jax.ShapeDtypeStruct(c_all.shape, F32),
                   jax.ShapeDtypeStruct((b, HEADS, HEAD_DIM), F32),
                   jax.ShapeDtypeStruct((b, 1, LANES), F32)],
        input_output_aliases={13: 1},
        scratch_shapes=[pltpu.VMEM((bt,) + mat, F32),
                        pltpu.VMEM((bt, HEADS, HEAD_DIM), F32),
                        pltpu.VMEM((bt, 1, LANES), F32),
                        pltpu.VMEM((SUBLANES + bt * tc, LANES), F32)],
        compiler_params=_params(("parallel", "arbitrary")),
        name="mlstm",
    )(z, z, z, z, z, z, c0, n0, m0, w["ml_bi"], w["ml_bf"], w["ml_ln_g"], w["head_sum"], c_all)


def _block_diag(blocks):
    *lead, n, r, c = blocks.shape
    eye = jnp.eye(n, dtype=blocks.dtype)
    out = eye[:, None, :, None] * blocks[..., :, :, None, :]
    return out.reshape(*lead, n * r, n * c)


def _prep(w):
    depth = w["ffn1_pre"].shape[0]
    row = lambda t: t.reshape(depth, 1, -1).astype(F32)
    out = {}
    for f in ("ffn1", "ffn2"):
        out[f + "_pre"] = row(w[f + "_pre"])
        out[f + "_post"] = row(w[f + "_post"])
        out[f + "_wg"] = _cast_t_call(jnp.swapaxes(w[f + "_wg"], 1, 2), D_FF_PAD,
                                      _row_chunks(D_FF))
        out[f + "_wu"] = _cast_t_call(jnp.swapaxes(w[f + "_wu"], 1, 2), D_FF_PAD,
                                      _row_chunks(D_FF))
        out[f + "_wd"] = _cast_call(w[f + "_wd"], D_FF_PAD, D_MODEL)
    out["mix_pre"] = row(w["mix_pre"])
    out["mix_post"] = row(w["mix_post"])
    out["w_in"] = _cast_t_call(
        jnp.swapaxes(w["w_in"], 1, 2), IN_COLS_PAD,
        _row_chunks(MAIN_COLS) + ((MAIN_COLS, HEADS, MAIN_COLS),
                                  (MAIN_COLS + HEADS, HEADS, MAIN_COLS + LANES)))
    out["w_out"] = _cast_call(w["w_out"], D_MODEL, D_MODEL)

    for name in ("rw_mu", "rw_w0", "rw_a0", "rw_kk", "rw_ka", "rw_rk", "rw_ln_g", "rw_ln_b"):
        out[name] = row(w[name])
    for name in ("rw_w2", "rw_a2", "rw_g2"):
        out[name] = w[name].astype(BF16)
    out["head_sum"] = _block_diag(jnp.ones((HEADS, HEAD_DIM, HEAD_DIM), BF16))

    out["lru_conv_w"] = w["lru_conv_w"].astype(F32)
    out["lru_conv_b"] = row(w["lru_conv_b"])
    out["lru_wax"] = jnp.concatenate(
        [_block_diag(w["lru_wa"]), _block_diag(w["lru_wx"])], axis=-1).astype(BF16)
    out["lru_bax"] = jnp.concatenate([row(w["lru_ba"]), row(w["lru_bx"])], axis=-1)
    out["lru_lambda"] = row(w["lru_lambda"])
    out["lru_norm"] = row(w["lru_norm"])

    lr, li = w["s5_a_re"].astype(F32), w["s5_a_im"].astype(F32)
    dt = jnp.exp(w["s5_log_dt"].astype(F32))[..., None]
    mag = jnp.exp(lr * dt)
    ar, ai = mag * jnp.cos(li * dt), mag * jnp.sin(li * dt)
    den = lr * lr + li * li
    zr = ((ar - 1.0) * lr + ai * li) / den
    zi = (ai * lr - (ar - 1.0) * li) / den
    b_re, b_im = w["s5_b_re"].astype(F32), w["s5_b_im"].astype(F32)
    bbr = zr[..., None] * b_re - zi[..., None] * b_im
    bbi = zr[..., None] * b_im + zi[..., None] * b_re
    packed = lambda t: _block_diag(jnp.swapaxes(t, -1, -2))
    out["s5_ar"], out["s5_ai"] = row(ar), row(ai)
    out["s5_bb"] = jnp.concatenate([packed(bbr), packed(bbi)], axis=-1).astype(BF16)
    out["s5_cc"] = jnp.concatenate(
        [packed(w["s5_c_re"]), -packed(w["s5_c_im"])], axis=-2).astype(BF16)
    out["s5_d"] = row(w["s5_d"])
    out["s5_glu_w"] = w["s5_glu_w"].astype(BF16)
    out["s5_glu_b"] = row(w["s5_glu_b"])
    out["s5_norm"] = row(w["s5_norm"])

    lane_block = lambda t: jnp.pad(t.astype(F32), ((0, 0), (0, LANES - HEADS))).reshape(
        depth, 1, LANES)
    out["ml_bi"] = lane_block(w["ml_bi"])
    out["ml_bf"] = lane_block(w["ml_bf"])
    out["ml_ln_g"] = row(w["ml_ln_g"])
    return out


def _tiling(b, l, rows_cap, tc_cap, seq_multiple=1):
    tc = tc_cap
    while tc > SUBLANES and l % tc:
        tc //= 2
    if l % tc or b % seq_multiple:
        raise ValueError(f"unsupported batch {b} / sequence length {l}")
    bt = seq_multiple
    while bt * 2 * tc <= rows_cap and b % (bt * 2) == 0:
        bt *= 2
    return bt, tc


def _stack_states(states):
    shift, wkv, conv, h, re, im, c, n, m = (t.astype(F32) for t in states)
    layers, b = shift.shape[:2]
    return (shift.reshape(layers, b, 1, RW_COLS), wkv, conv, h,
            re.reshape(layers, b, S5_STATE), im.reshape(layers, b, S5_STATE), c, n,
            jnp.pad(m, ((0, 0), (0, 0), (0, LANES - HEADS))).reshape(layers, b, 1, LANES))


def _zero_states(b):
    mat = (1, b, HEADS, HEAD_DIM, HEAD_DIM)
    return _stack_states((
        jnp.zeros((1, b, RW_COLS), F32), jnp.zeros(mat, F32),
        jnp.zeros((1, b, CONV_W - 1, GROUP_W), F32), jnp.zeros((1, b, GROUP_W), F32),
        jnp.zeros((1, b, S5_GROUPS, S5_P), F32), jnp.zeros((1, b, S5_GROUPS, S5_P), F32),
        jnp.zeros(mat, F32), jnp.zeros((1, b, HEADS, HEAD_DIM), F32),
        jnp.zeros((1, b, HEADS), F32)))


def _layer(x, st, state_layer, mats, w, layer):
    b, l, d = x.shape
    n = b * l
    shift0, s0, buf0, h0, re0, im0, c0, n0, m0 = st
    s_all, c_all = mats
    x1, z = _ffn_in_call(x.reshape(n, d), w, layer)
    z = z.reshape(b, l, IN_COLS_PAD)

    y_rw, n_shift, s_all = _rwkv_call(z, shift0, s0, s_all, w, layer, state_layer,
                                      *_tiling(b, l, 256, 32))
    y_lru, n_buf, n_h = _lru_call(z, buf0, h0, w, layer, state_layer,
                                  *_tiling(b, l, 1024, 128, SUBLANES))
    y_s5, n_re, n_im = _s5_call(z, re0, im0, w, layer, state_layer,
                                *_tiling(b, l, 512, 64, SUBLANES))
    y_ml, c_all, n_n, n_m = _mlstm_call(z, c0, n0, m0, c_all, w, layer, state_layer,
                                        *_tiling(b, l, 256, 128))

    flat = lambda t: t.reshape(n, GROUP_W)
    x3 = _out_ffn_call(x1, flat(y_rw), flat(y_lru), flat(y_s5), flat(y_ml), w, layer)
    new = (n_shift.reshape(b, RW_COLS), n_buf, n_h,
           n_re.reshape(b, S5_GROUPS, S5_P), n_im.reshape(b, S5_GROUPS, S5_P), n_n,
           n_m.reshape(b, LANES)[:, :HEADS])
    return x3.reshape(b, l, d), new, (s_all, c_all)


def kernel(x_prompt, x_sample, state_rwkv_shift, state_rwkv_wkv, state_lru_conv, state_lru_h, state_s5_re, state_s5_im, state_mlstm_C, state_mlstm_n, state_mlstm_m, ffn1_pre, ffn1_wg, ffn1_wu, ffn1_wd, ffn1_post, mix_pre, w_in, w_out, mix_post, ffn2_pre, ffn2_wg, ffn2_wu, ffn2_wd, ffn2_post, rw_mu, rw_w0, rw_w2, rw_a0, rw_a2, rw_g2, rw_kk, rw_ka, rw_rk, rw_ln_g, rw_ln_b, lru_conv_w, lru_conv_b, lru_wa, lru_ba, lru_wx, lru_bx, lru_lambda, lru_norm, s5_a_re, s5_a_im, s5_log_dt, s5_b_re, s5_b_im, s5_c_re, s5_c_im, s5_d, s5_glu_w, s5_glu_b, s5_norm, ml_bi, ml_bf, ml_ln_g):
    w = _prep(dict(
        ffn1_pre=ffn1_pre, ffn1_wg=ffn1_wg, ffn1_wu=ffn1_wu, ffn1_wd=ffn1_wd, ffn1_post=ffn1_post,
        mix_pre=mix_pre, w_in=w_in, w_out=w_out, mix_post=mix_post,
        ffn2_pre=ffn2_pre, ffn2_wg=ffn2_wg, ffn2_wu=ffn2_wu, ffn2_wd=ffn2_wd, ffn2_post=ffn2_post,
        rw_mu=rw_mu, rw_w0=rw_w0, rw_w2=rw_w2, rw_a0=rw_a0, rw_a2=rw_a2, rw_g2=rw_g2,
        rw_kk=rw_kk, rw_ka=rw_ka, rw_rk=rw_rk, rw_ln_g=rw_ln_g, rw_ln_b=rw_ln_b,
        lru_conv_w=lru_conv_w, lru_conv_b=lru_conv_b, lru_wa=lru_wa, lru_ba=lru_ba,
        lru_wx=lru_wx, lru_bx=lru_bx, lru_lambda=lru_lambda, lru_norm=lru_norm,
        s5_a_re=s5_a_re, s5_a_im=s5_a_im, s5_log_dt=s5_log_dt, s5_b_re=s5_b_re, s5_b_im=s5_b_im,
        s5_c_re=s5_c_re, s5_c_im=s5_c_im, s5_d=s5_d, s5_glu_w=s5_glu_w, s5_glu_b=s5_glu_b,
        s5_norm=s5_norm, ml_bi=ml_bi, ml_bf=ml_bf, ml_ln_g=ml_ln_g))
    sample_state = _stack_states((state_rwkv_shift, state_rwkv_wkv, state_lru_conv, state_lru_h,
                                  state_s5_re, state_s5_im, state_mlstm_C, state_mlstm_n,
                                  state_mlstm_m))
    depth = ffn1_pre.shape[0]
    b_p = x_prompt.shape[0]
    prompt_state = _zero_states(b_p)
    mat_zeros = jnp.zeros((depth, b_p, HEADS, HEAD_DIM, HEAD_DIM), F32)
    mats_p = (mat_zeros, mat_zeros)
    mats_s = (sample_state[1], sample_state[6])
    y_p, y_s = x_prompt.astype(F32), x_sample.astype(F32)
    outs_p, outs_s = [], []
    for layer in range(depth):
        y_p, st_p, mats_p = _layer(y_p, prompt_state, 0, mats_p, w, layer)
        y_s, st_s, mats_s = _layer(y_s, sample_state, layer, mats_s, w, layer)
        outs_p.append(st_p)
        outs_s.append(st_s)

    def gather(outs, mats):
        shift, conv, h, re, im, n, m = (jnp.stack([o[i] for o in outs]) for i in range(7))
        return (shift, mats[0], conv, h, re, im, mats[1], n, m)

    return (y_p, y_s, *gather(outs_p, mats_p), *gather(outs_s, mats_s))
```

```python
import functools

import jax
import jax.numpy as jnp
from jax import lax
from jax.experimental import pallas as pl
from jax.experimental.pallas import tpu as pltpu

F32 = jnp.float32
BF16 = jnp.bfloat16

LANES = 128
SUBLANES = 8
VMEM_LIMIT_BYTES = 56 * 1024 * 1024

D_MODEL = 1024
GROUP_W = D_MODEL // 4
HEAD_DIM = 64
HEADS = GROUP_W // HEAD_DIM
RW_W_RANK = 64
RW_A_RANK = 64
RW_G_RANK = 128
RW_COLS = 3 * GROUP_W + RW_W_RANK + RW_A_RANK + RW_G_RANK
RW_DECAY_SCALE = 0.606531
RW_GN_EPS = 64e-5
LRU_C = 8.0
CONV_W = 4
S5_GROUP = 16
S5_GROUPS = GROUP_W // S5_GROUP
S5_P = 64
S5_STATE = S5_GROUPS * S5_P
MAIN_COLS = RW_COLS + 7 * GROUP_W
IN_COLS = MAIN_COLS + 2 * HEADS
IN_COLS_PAD = MAIN_COLS + 2 * LANES
IGATE_COL_BLOCK = MAIN_COLS // LANES
D_FF = 2752
D_FF_PAD = -(-D_FF // LANES) * LANES
EPS = 1e-6
HI = lax.Precision.HIGHEST


def _dot(a, b):
    return jnp.dot(a.astype(BF16), b.astype(BF16), preferred_element_type=F32)


def _dot_nt(a, b, precision=None):
    return lax.dot_general(a, b, (((1,), (1,)), ((), ())), precision=precision,
                           preferred_element_type=F32)


def _dot_tn(a, b):
    return lax.dot_general(a, b, (((0,), (0,)), ((), ())), preferred_element_type=F32)


def _dot_hi(a, b):
    return jnp.dot(a, b, precision=HI, preferred_element_type=F32)


def _rms(x, g):
    return x * lax.rsqrt(jnp.mean(x * x, axis=-1, keepdims=True) + EPS) * g


def _seq_masks(bt, tc, reps=1):
    rows = bt * tc
    r = lax.broadcasted_iota(jnp.int32, (rows, reps * rows), 0)
    c = lax.broadcasted_iota(jnp.int32, (rows, reps * rows), 1) & (rows - 1)
    if bt == 1:
        return r > c, r >= c
    start = r - (r & (tc - 1))
    return (c < r) & (c >= start), (c <= r) & (c >= start)


def _rows(x3, tc):
    bt, _, w = x3.shape
    return jnp.broadcast_to(x3, (bt, tc, w)).reshape(bt * tc, w)


def _const_spec(shape):
    nd = len(shape)
    return pl.BlockSpec(shape, lambda *_: (0,) * nd, pipeline_mode=pl.Buffered(1))


def _layer_spec(layer, *shape):
    zeros = (0,) * len(shape)
    return pl.BlockSpec((None,) + shape, lambda *_: (layer,) + zeros,
                        pipeline_mode=pl.Buffered(1))


def _params(sem):
    return pltpu.CompilerParams(dimension_semantics=sem, vmem_limit_bytes=VMEM_LIMIT_BYTES)


def _cast_kernel(x_ref, o_ref):
    rows, cols = x_ref.shape
    o_ref[...] = jnp.zeros(o_ref.shape, BF16)
    o_ref[0:rows, 0:cols] = x_ref[...].astype(BF16)


def _cast_call(x, out_rows, out_cols):
    depth, rows, cols = x.shape
    return pl.pallas_call(
        _cast_kernel,
        grid=(depth,),
        in_specs=[pl.BlockSpec((None, rows, cols), lambda i: (i, 0, 0))],
        out_specs=pl.BlockSpec((None, out_rows, out_cols), lambda i: (i, 0, 0)),
        out_shape=jax.ShapeDtypeStruct((depth, out_rows, out_cols), BF16),
        compiler_params=_params(("parallel",)),
        name="cast_weights",
    )(x)


def _cast_t_kernel(x_ref, o_ref, *, segments):
    cols = x_ref.shape[1]
    for src, n, dst in segments:
        chunk = x_ref[src:src + n, :]
        if n < LANES:
            chunk = jnp.concatenate([chunk, jnp.zeros((LANES - n, cols), chunk.dtype)], axis=0)
        o_ref[:, dst:dst + LANES] = chunk.T.astype(BF16)


def _row_chunks(n):
    return tuple((i, min(LANES, n - i), i) for i in range(0, n, LANES))


def _cast_t_call(x_t, out_cols, segments):
    depth, rows, cols = x_t.shape
    assert len(segments) * LANES == out_cols
    return pl.pallas_call(
        functools.partial(_cast_t_kernel, segments=segments),
        grid=(depth,),
        in_specs=[pl.BlockSpec((None, rows, cols), lambda i: (i, 0, 0))],
        out_specs=pl.BlockSpec((None, cols, out_cols), lambda i: (i, 0, 0)),
        out_shape=jax.ShapeDtypeStruct((depth, cols, out_cols), BF16),
        compiler_params=_params(("parallel",)),
        name="cast_weights_t",
    )(x_t)


def _swiglu_half_step(x, pre, wg, wu, wd, post):
    h = _rms(x, pre).astype(BF16)
    g = jnp.dot(h, wg, preferred_element_type=F32)
    u = jnp.dot(h, wu, preferred_element_type=F32)
    act = (jax.nn.silu(g) * u).astype(BF16)
    f = jnp.dot(act, wd, preferred_element_type=F32)
    return x + 0.5 * _rms(f, post)


def _ffn_in_kernel(x_ref, pre_ref, wg_ref, wu_ref, wd_ref, post_ref, mpre_ref, win_ref,
                   xo_ref, z_ref):
    x1 = _swiglu_half_step(x_ref[...], pre_ref[...], wg_ref[...], wu_ref[...], wd_ref[...],
                           post_ref[...])
    xo_ref[...] = x1
    z_ref[...] = jnp.dot(_rms(x1, mpre_ref[...]).astype(BF16), win_ref[...],
                         preferred_element_type=F32)


def _out_ffn_kernel(x_ref, yrw_ref, ylru_ref, ys5_ref, yml_ref, wout_ref, mpost_ref,
                    pre_ref, wg_ref, wu_ref, wd_ref, post_ref, xo_ref):
    cat = jnp.concatenate([yrw_ref[...], ylru_ref[...], ys5_ref[...], yml_ref[...]], axis=-1)
    m = jnp.dot(cat.astype(BF16), wout_ref[...], preferred_element_type=F32)
    x2 = x_ref[...] + _rms(m, mpost_ref[...])
    xo_ref[...] = _swiglu_half_step(x2, pre_ref[...], wg_ref[...], wu_ref[...], wd_ref[...],
                                    post_ref[...])


def _token_tile(n):
    for tm in (512, 256, 128, 64, 32, 16, 8):
        if n % tm == 0:
            return tm
    raise ValueError(f"token count {n} is not a multiple of {SUBLANES}")


def _ffn_specs(ws):
    return [ws(1, D_MODEL), ws(D_MODEL, D_FF_PAD), ws(D_MODEL, D_FF_PAD), ws(D_FF_PAD, D_MODEL),
            ws(1, D_MODEL)]


def _ffn_in_call(x, w, layer):
    n = x.shape[0]
    tm = _token_tile(n)
    row = lambda width: pl.BlockSpec((tm, width), lambda i: (i, 0))
    ws = functools.partial(_layer_spec, layer)
    return pl.pallas_call(
        _ffn_in_kernel,
        grid=(n // tm,),
        in_specs=[row(D_MODEL)] + _ffn_specs(ws) + [ws(1, D_MODEL), ws(D_MODEL, IN_COLS_PAD)],
        out_specs=[row(D_MODEL), row(IN_COLS_PAD)],
        out_shape=[jax.ShapeDtypeStruct((n, D_MODEL), F32),
                   jax.ShapeDtypeStruct((n, IN_COLS_PAD), F32)],
        compiler_params=_params(("parallel",)),
        name="ffn1_inproj",
    )(x, w["ffn1_pre"], w["ffn1_wg"], w["ffn1_wu"], w["ffn1_wd"], w["ffn1_post"], w["mix_pre"],
      w["w_in"])


def _out_ffn_call(x, yrw, ylru, ys5, yml, w, layer):
    n = x.shape[0]
    tm = _token_tile(n)
    row = lambda width: pl.BlockSpec((tm, width), lambda i: (i, 0))
    ws = functools.partial(_layer_spec, layer)
    return pl.pallas_call(
        _out_ffn_kernel,
        grid=(n // tm,),
        in_specs=[row(D_MODEL), row(GROUP_W), row(GROUP_W), row(GROUP_W), row(GROUP_W),
                  ws(D_MODEL, D_MODEL), ws(1, D_MODEL)] + _ffn_specs(ws),
        out_specs=row(D_MODEL),
        out_shape=jax.ShapeDtypeStruct((n, D_MODEL), F32),
        compiler_params=_params(("parallel",)),
        name="outproj_ffn2",
    )(x, yrw, ylru, ys5, yml, w["w_out"], w["mix_post"], w["ffn2_pre"], w["ffn2_wg"],
      w["ffn2_wu"], w["ffn2_wd"], w["ffn2_post"])


def _seq_spec(bt, tc, w, col_block):
    return pl.BlockSpec((bt, tc, w), lambda i, c: (i, c, col_block))


def _state_spec(bt, *dims):
    zeros = (0,) * len(dims)
    return pl.BlockSpec((bt,) + dims, lambda i, c: (i,) + zeros)


def _stacked_state_spec(layer, bt, *dims):
    zeros = (0,) * len(dims)
    return pl.BlockSpec((None, bt) + dims, lambda i, c: (layer, i) + zeros)


ANY_SPEC = pl.BlockSpec(memory_space=pl.ANY)


def _rwkv_kernel(p_ref, shift0_ref, s0_ref, mu_ref, w0_ref, w2_ref, a0_ref, a2_ref, g2_ref,
                 kkw_ref, ka_ref, rk_ref, lng_ref, lnb_ref, hsum_ref, _s_all_ref,
                 y_ref, shift_ref, s_ref, pbuf, s_scr, *, bt, tc):
    c = pl.program_id(1)
    rows = bt * tc
    lo = SUBLANES - 1

    @pl.when(c == 0)
    def _():
        pbuf[:, lo:SUBLANES, :] = shift0_ref[...]
        s_scr[...] = s0_ref[...]

    p3 = p_ref[...]
    pbuf[:, SUBLANES:SUBLANES + tc, :] = p3
    prev = pbuf[:, lo:lo + tc, :].reshape(rows, RW_COLS)
    last = p3[:, tc - 1:tc, :]
    pbuf[:, lo:SUBLANES, :] = last
    shift_ref[...] = last
    p = p3.reshape(rows, RW_COLS)

    xm = p + (prev - p) * mu_ref[...]
    g = GROUP_W
    r, k, v = xm[:, 0:g], xm[:, g:2 * g], xm[:, 2 * g:3 * g]
    lw = xm[:, 3 * g:3 * g + RW_W_RANK]
    la = xm[:, 3 * g + RW_W_RANK:3 * g + RW_W_RANK + RW_A_RANK]
    lg = xm[:, 3 * g + RW_W_RANK + RW_A_RANK:]
    logw = -RW_DECAY_SCALE * jax.nn.sigmoid(w0_ref[...] + _dot(jnp.tanh(lw), w2_ref[...]))
    a = jax.nn.sigmoid(a0_ref[...] + _dot(la, a2_ref[...]))
    gate = _dot(jax.nn.sigmoid(lg), g2_ref[...])

    hsum = hsum_ref[...]
    kk = k * kkw_ref[...]
    kk = kk * lax.rsqrt(jnp.maximum(_dot(kk * kk, hsum), 1e-12))
    k2 = k * (1.0 + (a - 1.0) * ka_ref[...])

    _, incl = _seq_masks(bt, tc)
    strict2, incl2 = _seq_masks(bt, tc, reps=2)
    cum = _dot_hi(incl.astype(F32), logw)
    cum3 = cum.reshape(bt, tc, g)
    total3 = cum3[:, tc - 1:tc, :]
    rest = jnp.exp(total3 - cum3).reshape(rows, g)
    wtot3 = jnp.exp(total3)
    inv = jnp.exp(-cum)
    kka = kk * a
    at = -kk * jnp.exp(cum - logw)
    rt = r * jnp.exp(cum)
    bk_t = kka * inv, k2 * inv
    bk_w = kka * rest, k2 * rest

    n_double = max(1, (tc - 1).bit_length())
    heads = range(HEADS)
    lanes = [slice(h * HEAD_DIM, (h + 1) * HEAD_DIM) for h in heads]
    seqs = [slice(b * tc, (b + 1) * tc) for b in range(bt)]
    a_h, r_h, v_h = ([x[:, sl] for sl in lanes] for x in (at, rt, v))
    bk_h = [jnp.concatenate([bk_t[0][:, sl], bk_t[1][:, sl]], axis=0).astype(BF16)
            for sl in lanes]
    m_a = [jnp.where(strict2, _dot_nt(a_h[h].astype(BF16), bk_h[h]), 0.0) for h in heads]
    m_r = [jnp.where(incl2, _dot_nt(r_h[h].astype(BF16), bk_h[h]), 0.0) for h in heads]
    s_old = [[s_scr[b, h] for b in range(bt)] for h in heads]
    from_state = [[_dot_nt(jnp.concatenate([a_h[h][rs], r_h[h][rs]], axis=0).astype(BF16),
                           s_old[h][b].astype(BF16)) for b, rs in enumerate(seqs)]
                  for h in heads]
    u = [jnp.concatenate([fs[:tc] for fs in from_state[h]], axis=0)
         + _dot(m_a[h][:, rows:], v_h[h]) for h in heads]
    pw = [m_a[h][:, :rows] for h in heads]
    for i in range(n_double):
        u = [u[h] + _dot(pw[h], u[h]) for h in heads]
        if i + 1 < n_double:
            pw = [_dot(pw[h], pw[h]) for h in heads]
    uv = [jnp.concatenate([u[h], v_h[h]], axis=0) for h in heads]
    o = jnp.concatenate(
        [jnp.concatenate([fs[tc:] for fs in from_state[h]], axis=0) + _dot(m_r[h], uv[h])
         for h in heads], axis=-1)
    for h, sl in enumerate(lanes):
        bkw = jnp.concatenate([bk_w[0][:, sl], bk_w[1][:, sl]], axis=0)
        for b, rs in enumerate(seqs):
            rs2 = slice(rows + rs.start, rows + rs.stop)
            uv_b = jnp.concatenate([uv[h][rs], uv[h][rs2]], axis=0)
            bkw_b = jnp.concatenate([bkw[rs], bkw[rs2]], axis=0)
            s_scr[b, h] = (s_old[h][b] * wtot3[b, :, sl]
                           + _dot_tn(uv_b.astype(BF16), bkw_b.astype(BF16)))

    mean = _dot(o, hsum) * (1.0 / HEAD_DIM)
    cen = o - mean
    var = _dot(cen * cen, hsum) * (1.0 / HEAD_DIM)
    y = cen * lax.rsqrt(var + RW_GN_EPS) * lng_ref[...] + lnb_ref[...]
    bonus = _dot(r * k2 * rk_ref[...], hsum) * v
    y_ref[...] = ((y + bonus) * gate).reshape(bt, tc, g)
    s_ref[...] = s_scr[...]


def _rwkv_call(z, shift0, s0, s_all, w, layer, state_layer, bt, tc):
    b, l, _ = z.shape
    ws = functools.partial(_layer_spec, layer)
    mat = (HEADS, HEAD_DIM, HEAD_DIM)
    return pl.pallas_call(
        functools.partial(_rwkv_kernel, bt=bt, tc=tc),
        grid=(b // bt, l // tc),
        in_specs=[_seq_spec(bt, tc, RW_COLS, 0),
                  _stacked_state_spec(state_layer, bt, 1, RW_COLS),
                  _stacked_state_spec(state_layer, bt, *mat),
                  ws(1, RW_COLS), ws(1, GROUP_W), ws(RW_W_RANK, GROUP_W), ws(1, GROUP_W),
                  ws(RW_A_RANK, GROUP_W), ws(RW_G_RANK, GROUP_W),
                  ws(1, GROUP_W), ws(1, GROUP_W), ws(1, GROUP_W), ws(1, GROUP_W),
                  ws(1, GROUP_W), _const_spec((GROUP_W, GROUP_W)), ANY_SPEC],
        out_specs=[_seq_spec(bt, tc, GROUP_W, 0), _state_spec(bt, 1, RW_COLS),
                   _stacked_state_spec(layer, bt, *mat)],
        out_shape=[jax.ShapeDtypeStruct((b, l, GROUP_W), F32),
                   jax.ShapeDtypeStruct((b, 1, RW_COLS), F32),
                   jax.ShapeDtypeStruct(s_all.shape, F32)],
        input_output_aliases={15: 2},
        scratch_shapes=[pltpu.VMEM((bt, SUBLANES + tc, RW_COLS), F32),
                        pltpu.VMEM((bt,) + mat, F32)],
        compiler_params=_params(("parallel", "arbitrary")),
        name="rwkv7",
    )(z, shift0, s0, w["rw_mu"], w["rw_w0"], w["rw_w2"], w["rw_a0"], w["rw_a2"], w["rw_g2"],
      w["rw_kk"], w["rw_ka"], w["rw_rk"], w["rw_ln_g"], w["rw_ln_b"], w["head_sum"], s_all)


def _time_slab(t, g, tc):
    return pl.ds(g * SUBLANES * tc + t, SUBLANES, stride=tc)


def _lane_tiles(x):
    return [x[:, k * LANES:(k + 1) * LANES] for k in range(x.shape[-1] // LANES)]


def _lru_kernel(x_ref, gate_ref, buf0_ref, h0_ref, cw_ref, cb_ref, wax_ref, bax_ref, lam_ref,
                norm_ref, y_ref, buf_ref, h_ref, xbuf, abuf, bbuf, h_scr, *, bt, tc):
    c = pl.program_id(1)
    rows = bt * tc
    lo = SUBLANES - (CONV_W - 1)
    n_tiles = GROUP_W // LANES

    @pl.when(c == 0)
    def _():
        xbuf[:, lo:SUBLANES, :] = buf0_ref[...]
        h_scr[...] = h0_ref[...]

    x3 = x_ref[...]
    xbuf[:, SUBLANES:SUBLANES + tc, :] = x3
    cw = cw_ref[...]
    xc3 = cb_ref[...]
    for j in range(CONV_W):
        xc3 = xc3 + xbuf[:, lo + j:lo + j + tc, :] * cw[j:j + 1, :]
    tail = x3[:, tc - (CONV_W - 1):tc, :]
    xbuf[:, lo:SUBLANES, :] = tail
    buf_ref[...] = tail
    xc = xc3.reshape(rows, GROUP_W)

    gates = jax.nn.sigmoid(_dot(xc, wax_ref[...]) + bax_ref[...])
    gate_r, gate_i = gates[:, :GROUP_W], gates[:, GROUP_W:]
    log_a = -LRU_C * gate_r * jax.nn.softplus(-lam_ref[...])
    a = jnp.exp(log_a)
    bb = jnp.sqrt(-jnp.tanh(log_a) * (a * a + 1.0)) * (gate_i * xc)
    for k, (a_k, b_k) in enumerate(zip(_lane_tiles(a), _lane_tiles(bb))):
        abuf[k] = a_k
        bbuf[k] = b_k

    for g in range(bt // SUBLANES):
        seqs = slice(g * SUBLANES, (g + 1) * SUBLANES)
        h = _lane_tiles(h_scr[seqs, :])
        for t in range(tc):
            slab = _time_slab(t, g, tc)
            for k in range(n_tiles):
                h[k] = abuf[k, slab, :] * h[k] + bbuf[k, slab, :]
                bbuf[k, slab, :] = h[k]
        h_scr[seqs, :] = jnp.concatenate(h, axis=-1)
    h_ref[...] = h_scr[...]
    h_all = jnp.concatenate([bbuf[k] for k in range(n_tiles)], axis=-1)
    gate = gate_ref[...].reshape(rows, GROUP_W)
    y_ref[...] = _rms(h_all * jax.nn.gelu(gate), norm_ref[...]).reshape(bt, tc, GROUP_W)


def _lru_call(z, buf0, h0, w, layer, state_layer, bt, tc):
    b, l, _ = z.shape
    rows = bt * tc
    ws = functools.partial(_layer_spec, layer)
    cb = RW_COLS // GROUP_W
    tiles = pltpu.VMEM((GROUP_W // LANES, rows, LANES), F32)
    return pl.pallas_call(
        functools.partial(_lru_kernel, bt=bt, tc=tc),
        grid=(b // bt, l // tc),
        in_specs=[_seq_spec(bt, tc, GROUP_W, cb), _seq_spec(bt, tc, GROUP_W, cb + 1),
                  _stacked_state_spec(state_layer, bt, CONV_W - 1, GROUP_W),
                  _stacked_state_spec(state_layer, bt, GROUP_W),
                  ws(CONV_W, GROUP_W), ws(1, GROUP_W), ws(GROUP_W, 2 * GROUP_W),
                  ws(1, 2 * GROUP_W), ws(1, GROUP_W), ws(1, GROUP_W)],
        out_specs=[_seq_spec(bt, tc, GROUP_W, 0), _state_spec(bt, CONV_W - 1, GROUP_W),
                   _state_spec(bt, GROUP_W)],
        out_shape=[jax.ShapeDtypeStruct((b, l, GROUP_W), F32),
                   jax.ShapeDtypeStruct((b, CONV_W - 1, GROUP_W), F32),
                   jax.ShapeDtypeStruct((b, GROUP_W), F32)],
        scratch_shapes=[pltpu.VMEM((bt, SUBLANES + tc, GROUP_W), F32), tiles, tiles,
                        pltpu.VMEM((bt, GROUP_W), F32)],
        compiler_params=_params(("parallel", "arbitrary")),
        name="rglru",
    )(z, z, buf0, h0, w["lru_conv_w"], w["lru_conv_b"], w["lru_wax"], w["lru_bax"],
      w["lru_lambda"], w["lru_norm"])


def _s5_kernel(ulo_ref, uhi_ref, re0_ref, im0_ref, ar_ref, ai_ref, bb_ref, cc_ref, d_ref, gw_ref,
               gb_ref, norm_ref, y_ref, re_ref, im_ref, rbuf, ibuf, ybuf, re_scr, im_scr, *,
               bt, tc):
    c = pl.program_id(1)
    n_tiles = S5_STATE // LANES
    groups = bt // SUBLANES

    @pl.when(c == 0)
    def _():
        re_scr[...] = re0_ref[...]
        im_scr[...] = im0_ref[...]

    def time_major(ref):
        return jnp.concatenate([ref[g * SUBLANES:(g + 1) * SUBLANES, t, :]
                                for g in range(groups) for t in range(tc)], axis=0)

    u = jnp.concatenate([time_major(ulo_ref), time_major(uhi_ref)], axis=-1)
    bu = _dot(u, bb_ref[...])
    for k, (r_k, i_k) in enumerate(zip(_lane_tiles(bu[:, :S5_STATE]),
                                       _lane_tiles(bu[:, S5_STATE:]))):
        rbuf[k] = r_k
        ibuf[k] = i_k
    slab_shape = (SUBLANES, LANES)
    a_r = [jnp.broadcast_to(t, slab_shape) for t in _lane_tiles(ar_ref[...])]
    a_i = [jnp.broadcast_to(t, slab_shape) for t in _lane_tiles(ai_ref[...])]

    for g in range(groups):
        seqs = slice(g * SUBLANES, (g + 1) * SUBLANES)
        s_r, s_i = _lane_tiles(re_scr[seqs, :]), _lane_tiles(im_scr[seqs, :])
        for t in range(tc):
            slab = pl.ds((g * tc + t) * SUBLANES, SUBLANES)
            for k in range(n_tiles):
                n_r = a_r[k] * s_r[k] - a_i[k] * s_i[k] + rbuf[k, slab, :]
                n_i = a_r[k] * s_i[k] + a_i[k] * s_r[k] + ibuf[k, slab, :]
                s_r[k], s_i[k] = n_r, n_i
                rbuf[k, slab, :] = n_r
                ibuf[k, slab, :] = n_i
        re_scr[seqs, :] = jnp.concatenate(s_r, axis=-1)
        im_scr[seqs, :] = jnp.concatenate(s_i, axis=-1)
    re_ref[...] = re_scr[...]
    im_ref[...] = im_scr[...]

    x_all = jnp.concatenate([rbuf[k] for k in range(n_tiles)]
                            + [ibuf[k] for k in range(n_tiles)], axis=-1)
    y = _dot(x_all, cc_ref[...])
    y = jax.nn.gelu(y + d_ref[...] * u)
    y = y * jax.nn.sigmoid(_dot(y, gw_ref[...]) + gb_ref[...])
    for k, y_k in enumerate(_lane_tiles(_rms(y, norm_ref[...]))):
        ybuf[k] = y_k
    for g in range(groups):
        for j in range(SUBLANES):
            seq_rows = pl.ds(g * tc * SUBLANES + j, tc, stride=SUBLANES)
            y_ref[g * SUBLANES + j] = jnp.concatenate(
                [ybuf[k, seq_rows, :] for k in range(GROUP_W // LANES)], axis=-1)


def _s5_call(z, re0, im0, w, layer, state_layer, bt, tc):
    b, l, _ = z.shape
    rows = bt * tc
    ws = functools.partial(_layer_spec, layer)
    cb = (RW_COLS + 2 * GROUP_W) // LANES
    state = _state_spec(bt, S5_STATE)
    state_in = _stacked_state_spec(state_layer, bt, S5_STATE)
    tiles = pltpu.VMEM((S5_STATE // LANES, rows, LANES), F32)
    return pl.pallas_call(
        functools.partial(_s5_kernel, bt=bt, tc=tc),
        grid=(b // bt, l // tc),
        in_specs=[_seq_spec(bt, tc, LANES, cb), _seq_spec(bt, tc, LANES, cb + 1),
                  state_in, state_in,
                  ws(1, S5_STATE), ws(1, S5_STATE), ws(GROUP_W, 2 * S5_STATE),
                  ws(2 * S5_STATE, GROUP_W), ws(1, GROUP_W), ws(GROUP_W, GROUP_W),
                  ws(1, GROUP_W), ws(1, GROUP_W)],
        out_specs=[_seq_spec(bt, tc, GROUP_W, 0), state, state],
        out_shape=[jax.ShapeDtypeStruct((b, l, GROUP_W), F32),
                   jax.ShapeDtypeStruct((b, S5_STATE), F32),
                   jax.ShapeDtypeStruct((b, S5_STATE), F32)],
        scratch_shapes=[tiles, tiles, pltpu.VMEM((GROUP_W // LANES, rows, LANES), F32),
                        pltpu.VMEM((bt, S5_STATE), F32), pltpu.VMEM((bt, S5_STATE), F32)],
        compiler_params=_params(("parallel", "arbitrary")),
        name="s5",
    )(z, z, re0, im0, w["s5_ar"], w["s5_ai"], w["s5_bb"], w["s5_cc"], w["s5_d"], w["s5_glu_w"],
      w["s5_glu_b"], w["s5_norm"])


def _seq_cummax(x, bt, tc, buf):
    rows = bt * tc
    rin = lax.broadcasted_iota(jnp.int32, (rows, 1), 0) & (tc - 1)
    d = 1
    while d < tc:
        if d < SUBLANES:
            buf[SUBLANES:SUBLANES + rows, :] = x
            moved = buf[SUBLANES - d:SUBLANES - d + rows, :]
        else:
            moved = jnp.concatenate([x[:d], x[:rows - d]], axis=0)
        x = jnp.maximum(x, jnp.where(rin >= d, moved, -jnp.inf))
        d *= 2
    return x


def _mlstm_kernel(q_ref, k_ref, v_ref, og_ref, ig_ref, fg_ref, c0_ref, n0_ref, m0_ref, bi_ref,
                  bf_ref, lng_ref, hsum_ref, _c_all_ref, y_ref, c_ref, n_ref, m_ref,
                  c_scr, n_scr, m_scr, gbuf, *, bt, tc):
    c = pl.program_id(1)
    rows = bt * tc

    @pl.when(c == 0)
    def _():
        c_scr[...] = c0_ref[...]
        n_scr[...] = n0_ref[...]
        m_scr[...] = m0_ref[...]
        gbuf[0:SUBLANES, :] = jnp.zeros((SUBLANES, LANES), F32)

    flat = lambda ref: ref[...].reshape(rows, ref.shape[-1])
    log_i = flat(ig_ref) + bi_ref[...]
    log_f = jax.nn.log_sigmoid(flat(fg_ref) + bf_ref[...])
    _, incl = _seq_masks(bt, tc)
    bcum = _dot_hi(incl.astype(F32), log_f)
    m3 = m_scr[...]
    bcum3 = bcum.reshape(bt, tc, LANES)
    b_last3 = bcum3[:, tc - 1:tc, :]
    inter_all = (bcum3 + m3).reshape(rows, LANES)
    wk3 = b_last3 - bcum3 + log_i.reshape(bt, tc, LANES)
    m_new3 = jnp.maximum(b_last3 + m3, jnp.max(wk3, axis=1, keepdims=True))
    sc_all = jnp.exp(wk3 - m_new3).reshape(rows, LANES)
    decay3 = jnp.exp(b_last3 + m3 - m_new3)
    m_scr[...] = m_new3
    m_ref[...] = m_new3

    src = log_i - bcum
    mt_all = jnp.maximum(bcum + _seq_cummax(src, bt, tc, gbuf), inter_all)
    col_all = bcum - mt_all
    ei_all = jnp.exp(inter_all - mt_all)
    floor_all = jnp.exp(-mt_all)
    eye = (lax.broadcasted_iota(jnp.int32, (SUBLANES, LANES), 0)
           == lax.broadcasted_iota(jnp.int32, (SUBLANES, LANES), 1)).astype(F32)
    src_rows = _dot_nt(eye, src, HI)

    q, k, v = flat(q_ref), flat(k_ref) * (HEAD_DIM ** -0.5), flat(v_ref)
    ones = jnp.ones((rows, HEAD_DIM), F32)
    pad = jnp.zeros((HEAD_DIM - SUBLANES, HEAD_DIM), F32)
    heads = range(HEADS)
    lanes = [slice(h * HEAD_DIM, (h + 1) * HEAD_DIM) for h in heads]
    cols = [slice(h, h + 1) for h in heads]
    seqs = [slice(b * tc, (b + 1) * tc) for b in range(bt)]
    q_h, k_h, v_h = ([x[:, sl] for sl in lanes] for x in (q, k, v))
    pm = [jnp.where(incl, jnp.exp(col_all[:, cols[h]] + src_rows[cols[h], :]), 0.0)
          * _dot_nt(q_h[h].astype(BF16), k_h[h].astype(BF16)) for h in heads]
    intra = [_dot(pm[h], jnp.concatenate([v_h[h], ones], axis=-1)) for h in heads]
    c_old = [[c_scr[b, h] for b in range(bt)] for h in heads]
    n_old = [[jnp.broadcast_to(n_scr[b, cols[h], :], (SUBLANES, HEAD_DIM)) for b in range(bt)]
             for h in heads]
    from_state = [jnp.concatenate(
        [_dot_nt(q_h[h][rs].astype(BF16),
                 jnp.concatenate([c_old[h][b], n_old[h][b], pad], axis=0).astype(BF16))
         for b, rs in enumerate(seqs)], axis=0) for h in heads]
    nd = [intra[h] + ei_all[:, cols[h]] * from_state[h] for h in heads]
    o = jnp.concatenate(
        [nd[h][:, :HEAD_DIM] / jnp.maximum(jnp.abs(nd[h][:, HEAD_DIM:HEAD_DIM + 1]),
                                           floor_all[:, cols[h]]) for h in heads], axis=-1)
    for h in heads:
        sc = sc_all[:, cols[h]]
        scv = sc * v_h[h]
        for b, rs in enumerate(seqs):
            c_scr[b, h] = (decay3[b, :, cols[h]] * c_old[h][b]
                           + _dot_tn(scv[rs].astype(BF16), k_h[h][rs].astype(BF16)))
        sck = (sc * k_h[h]).reshape(bt, tc, HEAD_DIM)
        n_scr[:, cols[h], :] = (decay3[:, :, cols[h]] * n_scr[:, cols[h], :]
                                + jnp.sum(sck, axis=1, keepdims=True))
    hsum = hsum_ref[...]
    cen = o - _dot(o, hsum) * (1.0 / HEAD_DIM)
    var = _dot(cen * cen, hsum) * (1.0 / HEAD_DIM)
    y = cen * lax.rsqrt(var + EPS) * lng_ref[...] * jax.nn.sigmoid(flat(og_ref))
    y_ref[...] = y.reshape(bt, tc, GROUP_W)
    c_ref[...] = c_scr[...]
    n_ref[...] = n_scr[...]


def _mlstm_call(z, c0, n0, m0, c_all, w, layer, state_layer, bt, tc):
    b, l, _ = z.shape
    ws = functools.partial(_layer_spec, layer)
    cb = (RW_COLS + 3 * GROUP_W) // GROUP_W
    mat = (HEADS, HEAD_DIM, HEAD_DIM)
    stacked = functools.partial(_stacked_state_spec, state_layer, bt)
    return pl.pallas_call(
        functools.partial(_mlstm_kernel, bt=bt, tc=tc),
        grid=(b // bt, l // tc),
        in_specs=[_seq_spec(bt, tc, GROUP_W, cb), _seq_spec(bt, tc, GROUP_W, cb + 1),
                  _seq_spec(bt, tc, GROUP_W, cb + 2), _seq_spec(bt, tc, GROUP_W, cb + 3),
                  _seq_spec(bt, tc, LANES, IGATE_COL_BLOCK),
                  _seq_spec(bt, tc, LANES, IGATE_COL_BLOCK + 1),
                  stacked(*mat), stacked(HEADS, HEAD_DIM), stacked(1, LANES),
                  ws(1, LANES), ws(1, LANES), ws(1, GROUP_W),
                  _const_spec((GROUP_W, GROUP_W)), ANY_SPEC],
        out_specs=[_seq_spec(bt, tc, GROUP_W, 0), _stacked_state_spec(layer, bt, *mat),
                   _state_spec(bt, HEADS, HEAD_DIM), _state_spec(bt, 1, LANES)],
        out_shape=[jax.ShapeDtypeStruct((b, l, GROUP_W), F32),
                   jax.ShapeDtypeStruct(c_all.shape, F32),
                   jax.ShapeDtypeStruct((b, HEADS, HEAD_DIM), F32),
                   jax.ShapeDtypeStruct((b, 1, LANES), F32)],
        input_output_aliases={13: 1},
        scratch_shapes=[pltpu.VMEM((bt,) + mat, F32),
                        pltpu.VMEM((bt, HEADS, HEAD_DIM), F32),
                        pltpu.VMEM((bt, 1, LANES), F32),
                        pltpu.VMEM((SUBLANES + bt * tc, LANES), F32)],
        compiler_params=_params(("parallel", "arbitrary")),
        name="mlstm",
    )(z, z, z, z, z, z, c0, n0, m0, w["ml_bi"], w["ml_bf"], w["ml_ln_g"], w["head_sum"], c_all)


def _block_diag(blocks):
    *lead, n, r, c = blocks.shape
    eye = jnp.eye(n, dtype=blocks.dtype)
    out = eye[:, None, :, None] * blocks[..., :, :, None, :]
    return out.reshape(*lead, n * r, n * c)


def _prep(w):
    depth = w["ffn1_pre"].shape[0]
    row = lambda t: t.reshape(depth, 1, -1).astype(F32)
    out = {}
    for f in ("ffn1", "ffn2"):
        out[f + "_pre"] = row(w[f + "_pre"])
        out[f + "_post"] = row(w[f + "_post"])
        out[f + "_wg"] = _cast_t_call(jnp.swapaxes(w[f + "_wg"], 1, 2), D_FF_PAD,
                                      _row_chunks(D_FF))
        out[f + "_wu"] = _cast_t_call(jnp.swapaxes(w[f + "_wu"], 1, 2), D_FF_PAD,
                                      _row_chunks(D_FF))
        out[f + "_wd"] = _cast_call(w[f + "_wd"], D_FF_PAD, D_MODEL)
    out["mix_pre"] = row(w["mix_pre"])
    out["mix_post"] = row(w["mix_post"])
    out["w_in"] = _cast_t_call(
        jnp.swapaxes(w["w_in"], 1, 2), IN_COLS_PAD,
        _row_chunks(MAIN_COLS) + ((MAIN_COLS, HEADS, MAIN_COLS),
                                  (MAIN_COLS + HEADS, HEADS, MAIN_COLS + LANES)))
    out["w_out"] = _cast_call(w["w_out"], D_MODEL, D_MODEL)

    for name in ("rw_mu", "rw_w0", "rw_a0", "rw_kk", "rw_ka", "rw_rk", "rw_ln_g", "rw_ln_b"):
        out[name] = row(w[name])
    for name in ("rw_w2", "rw_a2", "rw_g2"):
        out[name] = w[name].astype(BF16)
    out["head_sum"] = _block_diag(jnp.ones((HEADS, HEAD_DIM, HEAD_DIM), BF16))

    out["lru_conv_w"] = w["lru_conv_w"].astype(F32)
    out["lru_conv_b"] = row(w["lru_conv_b"])
    out["lru_wax"] = jnp.concatenate(
        [_block_diag(w["lru_wa"]), _block_diag(w["lru_wx"])], axis=-1).astype(BF16)
    out["lru_bax"] = jnp.concatenate([row(w["lru_ba"]), row(w["lru_bx"])], axis=-1)
    out["lru_lambda"] = row(w["lru_lambda"])
    out["lru_norm"] = row(w["lru_norm"])

    lr, li = w["s5_a_re"].astype(F32), w["s5_a_im"].astype(F32)
    dt = jnp.exp(w["s5_log_dt"].astype(F32))[..., None]
    mag = jnp.exp(lr * dt)
    ar, ai = mag * jnp.cos(li * dt), mag * jnp.sin(li * dt)
    den = lr * lr + li * li
    zr = ((ar - 1.0) * lr + ai * li) / den
    zi = (ai * lr - (ar - 1.0) * li) / den
    b_re, b_im = w["s5_b_re"].astype(F32), w["s5_b_im"].astype(F32)
    bbr = zr[..., None] * b_re - zi[..., None] * b_im
    bbi = zr[..., None] * b_im + zi[..., None] * b_re
    packed = lambda t: _block_diag(jnp.swapaxes(t, -1, -2))
    out["s5_ar"], out["s5_ai"] = row(ar), row(ai)
    out["s5_bb"] = jnp.concatenate([packed(bbr), packed(bbi)], axis=-1).astype(BF16)
    out["s5_cc"] = jnp.concatenate(
        [packed(w["s5_c_re"]), -packed(w["s5_c_im"])], axis=-2).astype(BF16)
    out["s5_d"] = row(w["s5_d"])
    out["s5_glu_w"] = w["s5_glu_w"].astype(BF16)
    out["s5_glu_b"] = row(w["s5_glu_b"])
    out["s5_norm"] = row(w["s5_norm"])

    lane_block = lambda t: jnp.pad(t.astype(F32), ((0, 0), (0, LANES - HEADS))).reshape(
        depth, 1, LANES)
    out["ml_bi"] = lane_block(w["ml_bi"])
    out["ml_bf"] = lane_block(w["ml_bf"])
    out["ml_ln_g"] = row(w["ml_ln_g"])
    return out


def _tiling(b, l, rows_cap, tc_cap, seq_multiple=1):
    tc = tc_cap
    while tc > SUBLANES and l % tc:
        tc //= 2
    if l % tc or b % seq_multiple:
        raise ValueError(f"unsupported batch {b} / sequence length {l}")
    bt = seq_multiple
    while bt * 2 * tc <= rows_cap and b % (bt * 2) == 0:
        bt *= 2
    return bt, tc


def _stack_states(states):
    shift, wkv, conv, h, re, im, c, n, m = (t.astype(F32) for t in states)
    layers, b = shift.shape[:2]
    return (shift.reshape(layers, b, 1, RW_COLS), wkv, conv, h,
            re.reshape(layers, b, S5_STATE), im.reshape(layers, b, S5_STATE), c, n,
            jnp.pad(m, ((0, 0), (0, 0), (0, LANES - HEADS))).reshape(layers, b, 1, LANES))


def _zero_states(b):
    mat = (1, b, HEADS, HEAD_DIM, HEAD_DIM)
    return _stack_states((
        jnp.zeros((1, b, RW_COLS), F32), jnp.zeros(mat, F32),
        jnp.zeros((1, b, CONV_W - 1, GROUP_W), F32), jnp.zeros((1, b, GROUP_W), F32),
        jnp.zeros((1, b, S5_GROUPS, S5_P), F32), jnp.zeros((1, b, S5_GROUPS, S5_P), F32),
        jnp.zeros(mat, F32), jnp.zeros((1, b, HEADS, HEAD_DIM), F32),
        jnp.zeros((1, b, HEADS), F32)))


def _layer(x, st, state_layer, mats, w, layer):
    b, l, d = x.shape
    n = b * l
    shift0, s0, buf0, h0, re0, im0, c0, n0, m0 = st
    s_all, c_all = mats
    x1, z = _ffn_in_call(x.reshape(n, d), w, layer)
    z = z.reshape(b, l, IN_COLS_PAD)

    y_rw, n_shift, s_all = _rwkv_call(z, shift0, s0, s_all, w, layer, state_layer,
                                      *_tiling(b, l, 256, 32))
    y_lru, n_buf, n_h = _lru_call(z, buf0, h0, w, layer, state_layer,
                                  *_tiling(b, l, 1024, 128, SUBLANES))
    y_s5, n_re, n_im = _s5_call(z, re0, im0, w, layer, state_layer,
                                *_tiling(b, l, 512, 64, SUBLANES))
    y_ml, c_all, n_n, n_m = _mlstm_call(z, c0, n0, m0, c_all, w, layer, state_layer,
                                        *_tiling(b, l, 256, 128))

    flat = lambda t: t.reshape(n, GROUP_W)
    x3 = _out_ffn_call(x1, flat(y_rw), flat(y_lru), flat(y_s5), flat(y_ml), w, layer)
    new = (n_shift.reshape(b, RW_COLS), n_buf, n_h,
           n_re.reshape(b, S5_GROUPS, S5_P), n_im.reshape(b, S5_GROUPS, S5_P), n_n,
           n_m.reshape(b, LANES)[:, :HEADS])
    return x3.reshape(b, l, d), new, (s_all, c_all)


def kernel(x_prompt, x_sample, state_rwkv_shift, state_rwkv_wkv, state_lru_conv, state_lru_h, state_s5_re, state_s5_im, state_mlstm_C, state_mlstm_n, state_mlstm_m, ffn1_pre, ffn1_wg, ffn1_wu, ffn1_wd, ffn1_post, mix_pre, w_in, w_out, mix_post, ffn2_pre, ffn2_wg, ffn2_wu, ffn2_wd, ffn2_post, rw_mu, rw_w0, rw_w2, rw_a0, rw_a2, rw_g2, rw_kk, rw_ka, rw_rk, rw_ln_g, rw_ln_b, lru_conv_w, lru_conv_b, lru_wa, lru_ba, lru_wx, lru_bx, lru_lambda, lru_norm, s5_a_re, s5_a_im, s5_log_dt, s5_b_re, s5_b_im, s5_c_re, s5_c_im, s5_d, s5_glu_w, s5_glu_b, s5_norm, ml_bi, ml_bf, ml_ln_g):
    w = _prep(dict(
        ffn1_pre=ffn1_pre, ffn1_wg=ffn1_wg, ffn1_wu=ffn1_wu, ffn1_wd=ffn1_wd, ffn1_post=ffn1_post,
        mix_pre=mix_pre, w_in=w_in, w_out=w_out, mix_post=mix_post,
        ffn2_pre=ffn2_pre, ffn2_wg=ffn2_wg, ffn2_wu=ffn2_wu, ffn2_wd=ffn2_wd, ffn2_post=ffn2_post,
        rw_mu=rw_mu, rw_w0=rw_w0, rw_w2=rw_w2, rw_a0=rw_a0, rw_a2=rw_a2, rw_g2=rw_g2,
        rw_kk=rw_kk, rw_ka=rw_ka, rw_rk=rw_rk, rw_ln_g=rw_ln_g, rw_ln_b=rw_ln_b,
        lru_conv_w=lru_conv_w, lru_conv_b=lru_conv_b, lru_wa=lru_wa, lru_ba=lru_ba,
        lru_wx=lru_wx, lru_bx=lru_bx, lru_lambda=lru_lambda, lru_norm=lru_norm,
        s5_a_re=s5_a_re, s5_a_im=s5_a_im, s5_log_dt=s5_log_dt, s5_b_re=s5_b_re, s5_b_im=s5_b_im,
        s5_c_re=s5_c_re, s5_c_im=s5_c_im, s5_d=s5_d, s5_glu_w=s5_glu_w, s5_glu_b=s5_glu_b,
        s5_norm=s5_norm, ml_bi=ml_bi, ml_bf=ml_bf, ml_ln_g=ml_ln_g))
    sample_state = _stack_states((state_rwkv_shift, state_rwkv_wkv, state_lru_conv, state_lru_h,
                                  state_s5_re, state_s5_im, state_mlstm_C, state_mlstm_n,
                                  state_mlstm_m))
    depth = ffn1_pre.shape[0]
    b_p = x_prompt.shape[0]
    prompt_state = _zero_states(b_p)
    mat_zeros = lambda b: jnp.zeros((depth, b, HEADS, HEAD_DIM, HEAD_DIM), F32)
    mats_p = (mat_zeros(b_p), mat_zeros(b_p))
    mats_s = (mat_zeros(x_sample.shape[0]), mat_zeros(x_sample.shape[0]))
    y_p, y_s = x_prompt.astype(F32), x_sample.astype(F32)
    outs_p, outs_s = [], []
    for layer in range(depth):
        y_p, st_p, mats_p = _layer(y_p, prompt_state, 0, mats_p, w, layer)
        y_s, st_s, mats_s = _layer(y_s, sample_state, layer, mats_s, w, layer)
        outs_p.append(st_p)
        outs_s.append(st_s)

    def gather(outs, mats):
        shift, conv, h, re, im, n, m = (jnp.stack([o[i] for o in outs]) for i in range(7))
        return (shift, mats[0], conv, h, re, im, mats[1], n, m)

    return (y_p, y_s, *gather(outs_p, mats_p), *gather(outs_s, mats_s))
```

```python
import functools

import jax
import jax.numpy as jnp
from jax import lax
from jax.experimental import pallas as pl
from jax.experimental.pallas import tpu as pltpu

F32 = jnp.float32
BF16 = jnp.bfloat16

LANES = 128
SUBLANES = 8
VMEM_LIMIT_BYTES = 56 * 1024 * 1024

D_MODEL = 1024
GROUP_W = D_MODEL // 4
HEAD_DIM = 64
HEADS = GROUP_W // HEAD_DIM
RW_W_RANK = 64
RW_A_RANK = 64
RW_G_RANK = 128
RW_COLS = 3 * GROUP_W + RW_W_RANK + RW_A_RANK + RW_G_RANK
RW_DECAY_SCALE = 0.606531
RW_GN_EPS = 64e-5
LRU_C = 8.0
CONV_W = 4
S5_GROUP = 16
S5_GROUPS = GROUP_W // S5_GROUP
S5_P = 64
S5_STATE = S5_GROUPS * S5_P
MAIN_COLS = RW_COLS + 7 * GROUP_W
IN_COLS = MAIN_COLS + 2 * HEADS
IN_COLS_PAD = MAIN_COLS + 2 * LANES
IGATE_COL_BLOCK = MAIN_COLS // LANES
D_FF = 2752
D_FF_PAD = -(-D_FF // LANES) * LANES
EPS = 1e-6
HI = lax.Precision.HIGHEST


def _dot(a, b):
    return jnp.dot(a.astype(BF16), b.astype(BF16), preferred_element_type=F32)


def _dot_nt(a, b, precision=None):
    return lax.dot_general(a, b, (((1,), (1,)), ((), ())), precision=precision,
                           preferred_element_type=F32)


def _dot_tn(a, b):
    return lax.dot_general(a, b, (((0,), (0,)), ((), ())), preferred_element_type=F32)


def _rms(x, g):
    return x * lax.rsqrt(jnp.mean(x * x, axis=-1, keepdims=True) + EPS) * g


def _seq_masks(bt, tc, reps=1):
    rows = bt * tc
    r = lax.broadcasted_iota(jnp.int32, (rows, reps * rows), 0)
    c = lax.broadcasted_iota(jnp.int32, (rows, reps * rows), 1) & (rows - 1)
    if bt == 1:
        return r > c, r >= c
    start = r - (r & (tc - 1))
    return (c < r) & (c >= start), (c <= r) & (c >= start)


def _later_mask(bt, tc, reps=1):
    rows = bt * tc
    r = lax.broadcasted_iota(jnp.int32, (reps * rows, rows), 0) & (rows - 1)
    c = lax.broadcasted_iota(jnp.int32, (reps * rows, rows), 1)
    if bt == 1:
        return r < c
    return (r < c) & (r >= c - (c & (tc - 1)))


def _masked_sum(mask, x):
    m = mask.astype(BF16)
    hi = x.astype(BF16)
    rest = x - hi.astype(F32)
    mid = rest.astype(BF16)
    lo = (rest - mid.astype(F32)).astype(BF16)
    dot = lambda piece: jnp.dot(m, piece, preferred_element_type=F32)
    return dot(hi) + dot(mid) + dot(lo)


def _rows(x3, tc):
    bt, _, w = x3.shape
    return jnp.broadcast_to(x3, (bt, tc, w)).reshape(bt * tc, w)


def _const_spec(shape):
    nd = len(shape)
    return pl.BlockSpec(shape, lambda *_: (0,) * nd, pipeline_mode=pl.Buffered(1))


def _layer_spec(layer, *shape):
    zeros = (0,) * len(shape)
    return pl.BlockSpec((None,) + shape, lambda *_: (layer,) + zeros,
                        pipeline_mode=pl.Buffered(1))


def _params(sem):
    return pltpu.CompilerParams(dimension_semantics=sem, vmem_limit_bytes=VMEM_LIMIT_BYTES)


def _cast_kernel(x_ref, o_ref):
    rows, cols = x_ref.shape
    o_ref[...] = jnp.zeros(o_ref.shape, BF16)
    o_ref[0:rows, 0:cols] = x_ref[...].astype(BF16)


def _cast_call(x, out_rows, out_cols):
    depth, rows, cols = x.shape
    return pl.pallas_call(
        _cast_kernel,
        grid=(depth,),
        in_specs=[pl.BlockSpec((None, rows, cols), lambda i: (i, 0, 0))],
        out_specs=pl.BlockSpec((None, out_rows, out_cols), lambda i: (i, 0, 0)),
        out_shape=jax.ShapeDtypeStruct((depth, out_rows, out_cols), BF16),
        compiler_params=_params(("parallel",)),
        name="cast_weights",
    )(x)


def _cast_t_kernel(x_ref, o_ref, *, segments):
    cols = x_ref.shape[1]
    for src, n, dst in segments:
        chunk = x_ref[src:src + n, :]
        if n < LANES:
            chunk = jnp.concatenate([chunk, jnp.zeros((LANES - n, cols), chunk.dtype)], axis=0)
        o_ref[:, dst:dst + LANES] = chunk.T.astype(BF16)


def _row_chunks(n):
    return tuple((i, min(LANES, n - i), i) for i in range(0, n, LANES))


def _cast_t_call(x_t, out_cols, segments):
    depth, rows, cols = x_t.shape
    assert len(segments) * LANES == out_cols
    return pl.pallas_call(
        functools.partial(_cast_t_kernel, segments=segments),
        grid=(depth,),
        in_specs=[pl.BlockSpec((None, rows, cols), lambda i: (i, 0, 0))],
        out_specs=pl.BlockSpec((None, cols, out_cols), lambda i: (i, 0, 0)),
        out_shape=jax.ShapeDtypeStruct((depth, cols, out_cols), BF16),
        compiler_params=_params(("parallel",)),
        name="cast_weights_t",
    )(x_t)


def _swiglu_half_step(x, pre, wg, wu, wd, post):
    h = _rms(x, pre).astype(BF16)
    g = jnp.dot(h, wg, preferred_element_type=F32)
    u = jnp.dot(h, wu, preferred_element_type=F32)
    act = (jax.nn.silu(g) * u).astype(BF16)
    f = jnp.dot(act, wd, preferred_element_type=F32)
    return x + 0.5 * _rms(f, post)


def _ffn_in_kernel(x_ref, pre_ref, wg_ref, wu_ref, wd_ref, post_ref, mpre_ref, win_ref,
                   xo_ref, z_ref):
    x1 = _swiglu_half_step(x_ref[...], pre_ref[...], wg_ref[...], wu_ref[...], wd_ref[...],
                           post_ref[...])
    xo_ref[...] = x1
    z_ref[...] = jnp.dot(_rms(x1, mpre_ref[...]).astype(BF16), win_ref[...],
                         preferred_element_type=F32)


def _out_ffn_kernel(x_ref, yrw_ref, ylru_ref, ys5_ref, yml_ref, wout_ref, mpost_ref,
                    pre_ref, wg_ref, wu_ref, wd_ref, post_ref, xo_ref):
    cat = jnp.concatenate([yrw_ref[...], ylru_ref[...], ys5_ref[...], yml_ref[...]], axis=-1)
    m = jnp.dot(cat.astype(BF16), wout_ref[...], preferred_element_type=F32)
    x2 = x_ref[...] + _rms(m, mpost_ref[...])
    xo_ref[...] = _swiglu_half_step(x2, pre_ref[...], wg_ref[...], wu_ref[...], wd_ref[...],
                                    post_ref[...])


def _token_tile(n):
    for tm in (512, 256, 128, 64, 32, 16, 8):
        if n % tm == 0:
            return tm
    raise ValueError(f"token count {n} is not a multiple of {SUBLANES}")


def _ffn_specs(ws):
    return [ws(1, D_MODEL), ws(D_MODEL, D_FF_PAD), ws(D_MODEL, D_FF_PAD), ws(D_FF_PAD, D_MODEL),
            ws(1, D_MODEL)]


def _ffn_in_call(x, w, layer):
    n = x.shape[0]
    tm = _token_tile(n)
    row = lambda width: pl.BlockSpec((tm, width), lambda i: (i, 0))
    ws = functools.partial(_layer_spec, layer)
    return pl.pallas_call(
        _ffn_in_kernel,
        grid=(n // tm,),
        in_specs=[row(D_MODEL)] + _ffn_specs(ws) + [ws(1, D_MODEL), ws(D_MODEL, IN_COLS_PAD)],
        out_specs=[row(D_MODEL), row(IN_COLS_PAD)],
        out_shape=[jax.ShapeDtypeStruct((n, D_MODEL), F32),
                   jax.ShapeDtypeStruct((n, IN_COLS_PAD), F32)],
        compiler_params=_params(("parallel",)),
        name="ffn1_inproj",
    )(x, w["ffn1_pre"], w["ffn1_wg"], w["ffn1_wu"], w["ffn1_wd"], w["ffn1_post"], w["mix_pre"],
      w["w_in"])


def _out_ffn_call(x, yrw, ylru, ys5, yml, w, layer):
    n = x.shape[0]
    tm = _token_tile(n)
    row = lambda width: pl.BlockSpec((tm, width), lambda i: (i, 0))
    ws = functools.partial(_layer_spec, layer)
    return pl.pallas_call(
        _out_ffn_kernel,
        grid=(n // tm,),
        in_specs=[row(D_MODEL), row(GROUP_W), row(GROUP_W), row(GROUP_W), row(GROUP_W),
                  ws(D_MODEL, D_MODEL), ws(1, D_MODEL)] + _ffn_specs(ws),
        out_specs=row(D_MODEL),
        out_shape=jax.ShapeDtypeStruct((n, D_MODEL), F32),
        compiler_params=_params(("parallel",)),
        name="outproj_ffn2",
    )(x, yrw, ylru, ys5, yml, w["w_out"], w["mix_post"], w["ffn2_pre"], w["ffn2_wg"],
      w["ffn2_wu"], w["ffn2_wd"], w["ffn2_post"])


def _seq_spec(bt, tc, w, col_block):
    return pl.BlockSpec((bt, tc, w), lambda i, c: (i, c, col_block))


def _state_spec(bt, *dims):
    zeros = (0,) * len(dims)
    return pl.BlockSpec((bt,) + dims, lambda i, c: (i,) + zeros)


def _stacked_state_spec(layer, bt, *dims):
    zeros = (0,) * len(dims)
    return pl.BlockSpec((None, bt) + dims, lambda i, c: (layer, i) + zeros)


ANY_SPEC = pl.BlockSpec(memory_space=pl.ANY)


def _rwkv_kernel(p_ref, shift0_ref, s0_ref, mu_ref, w0_ref, w2_ref, a0_ref, a2_ref, g2_ref,
                 kkw_ref, ka_ref, rk_ref, lng_ref, lnb_ref, hsum_ref, _s_all_ref,
                 y_ref, shift_ref, s_ref, pbuf, s_scr, *, bt, tc):
    c = pl.program_id(1)
    rows = bt * tc
    lo = SUBLANES - 1

    @pl.when(c == 0)
    def _():
        pbuf[:, lo:SUBLANES, :] = shift0_ref[...]
        s_scr[...] = s0_ref[...]

    p3 = p_ref[...]
    pbuf[:, SUBLANES:SUBLANES + tc, :] = p3
    prev = pbuf[:, lo:lo + tc, :].reshape(rows, RW_COLS)
    last = p3[:, tc - 1:tc, :]
    pbuf[:, lo:SUBLANES, :] = last
    shift_ref[...] = last
    p = p3.reshape(rows, RW_COLS)

    xm = p + (prev - p) * mu_ref[...]
    g = GROUP_W
    r, k, v = xm[:, 0:g], xm[:, g:2 * g], xm[:, 2 * g:3 * g]
    lw = xm[:, 3 * g:3 * g + RW_W_RANK]
    la = xm[:, 3 * g + RW_W_RANK:3 * g + RW_W_RANK + RW_A_RANK]
    lg = xm[:, 3 * g + RW_W_RANK + RW_A_RANK:]
    logw = -RW_DECAY_SCALE * jax.nn.sigmoid(w0_ref[...] + _dot(jnp.tanh(lw), w2_ref[...]))
    a = jax.nn.sigmoid(a0_ref[...] + _dot(la, a2_ref[...]))
    gate = _dot(jax.nn.sigmoid(lg), g2_ref[...])

    hsum = hsum_ref[...]
    kk = k * kkw_ref[...]
    kk = kk * lax.rsqrt(jnp.maximum(_dot(kk * kk, hsum), 1e-12))
    k2 = k * (1.0 + (a - 1.0) * ka_ref[...])

    _, incl = _seq_masks(bt, tc)
    _, incl2 = _seq_masks(bt, tc, reps=2)
    later2 = _later_mask(bt, tc, reps=2)
    cum = _masked_sum(incl, logw)
    cum3 = cum.reshape(bt, tc, g)
    total3 = cum3[:, tc - 1:tc, :]
    rest = jnp.exp(total3 - cum3).reshape(rows, g)
    wtot3 = jnp.exp(total3)
    inv = jnp.exp(-cum)
    kka = kk * a
    at = -kk * jnp.exp(cum - logw)
    rt = r * jnp.exp(cum)
    bk_t = kka * inv, k2 * inv
    bk_w = kka * rest, k2 * rest

    n_double = max(1, (tc - 1).bit_length())
    heads = range(HEADS)
    lanes = [slice(h * HEAD_DIM, (h + 1) * HEAD_DIM) for h in heads]
    seqs = [slice(b * tc, (b + 1) * tc) for b in range(bt)]
    a_h, r_h, v_h = ([x[:, sl] for sl in lanes] for x in (at, rt, v))
    bk_h = [jnp.concatenate([bk_t[0][:, sl], bk_t[1][:, sl]], axis=0).astype(BF16)
            for sl in lanes]
    m_at = [jnp.where(later2, _dot_nt(bk_h[h], a_h[h].astype(BF16)), 0.0) for h in heads]
    m_r = [jnp.where(incl2, _dot_nt(r_h[h].astype(BF16), bk_h[h]), 0.0) for h in heads]
    s_old = [[s_scr[b, h] for b in range(bt)] for h in heads]
    from_state = [[_dot_nt(jnp.concatenate([a_h[h][rs], r_h[h][rs]], axis=0).astype(BF16),
                           s_old[h][b].astype(BF16)) for b, rs in enumerate(seqs)]
                  for h in heads]
    u0_t = jnp.concatenate([jnp.concatenate([fs[:tc] for fs in from_state[h]], axis=0)
                            for h in heads], axis=-1).T
    v_t = v.T
    u_t = [u0_t[sl] + _dot(v_t[sl], m_at[h][rows:]) for h, sl in enumerate(lanes)]
    pw = [m_at[h][:rows] for h in heads]
    for i in range(n_double):
        u_t = [u_t[h] + _dot(u_t[h], pw[h]) for h in heads]
        if i + 1 < n_double:
            pw = [_dot(pw[h], pw[h]) for h in heads]
    u_all = jnp.concatenate(u_t, axis=0).T
    uv = [jnp.concatenate([u_all[:, sl], v_h[h]], axis=0) for h, sl in enumerate(lanes)]
    o = jnp.concatenate(
        [jnp.concatenate([fs[tc:] for fs in from_state[h]], axis=0) + _dot(m_r[h], uv[h])
         for h in heads], axis=-1)
    for h, sl in enumerate(lanes):
        bkw = jnp.concatenate([bk_w[0][:, sl], bk_w[1][:, sl]], axis=0)
        for b, rs in enumerate(seqs):
            rs2 = slice(rows + rs.start, rows + rs.stop)
            uv_b = jnp.concatenate([uv[h][rs], uv[h][rs2]], axis=0)
            bkw_b = jnp.concatenate([bkw[rs], bkw[rs2]], axis=0)
            s_scr[b, h] = (s_old[h][b] * wtot3[b, :, sl]
                           + _dot_tn(uv_b.astype(BF16), bkw_b.astype(BF16)))

    mean = _dot(o, hsum) * (1.0 / HEAD_DIM)
    cen = o - mean
    var = _dot(cen * cen, hsum) * (1.0 / HEAD_DIM)
    y = cen * lax.rsqrt(var + RW_GN_EPS) * lng_ref[...] + lnb_ref[...]
    bonus = _dot(r * k2 * rk_ref[...], hsum) * v
    y_ref[...] = ((y + bonus) * gate).reshape(bt, tc, g)

    @pl.when(c == pl.num_programs(1) - 1)
    def _():
        s_ref[...] = s_scr[...]


def _rwkv_call(z, shift0, s0, s_all, w, layer, state_layer, bt, tc):
    b, l, _ = z.shape
    ws = functools.partial(_layer_spec, layer)
    mat = (HEADS, HEAD_DIM, HEAD_DIM)
    return pl.pallas_call(
        functools.partial(_rwkv_kernel, bt=bt, tc=tc),
        grid=(b // bt, l // tc),
        in_specs=[_seq_spec(bt, tc, RW_COLS, 0),
                  _stacked_state_spec(state_layer, bt, 1, RW_COLS),
                  _stacked_state_spec(state_layer, bt, *mat),
                  ws(1, RW_COLS), ws(1, GROUP_W), ws(RW_W_RANK, GROUP_W), ws(1, GROUP_W),
                  ws(RW_A_RANK, GROUP_W), ws(RW_G_RANK, GROUP_W),
                  ws(1, GROUP_W), ws(1, GROUP_W), ws(1, GROUP_W), ws(1, GROUP_W),
                  ws(1, GROUP_W), _const_spec((GROUP_W, GROUP_W)), ANY_SPEC],
        out_specs=[_seq_spec(bt, tc, GROUP_W, 0), _state_spec(bt, 1, RW_COLS),
                   _stacked_state_spec(layer, bt, *mat)],
        out_shape=[jax.ShapeDtypeStruct((b, l, GROUP_W), F32),
                   jax.ShapeDtypeStruct((b, 1, RW_COLS), F32),
                   jax.ShapeDtypeStruct(s_all.shape, F32)],
        input_output_aliases={15: 2},
        scratch_shapes=[pltpu.VMEM((bt, SUBLANES + tc, RW_COLS), F32),
                        pltpu.VMEM((bt,) + mat, F32)],
        compiler_params=_params(("parallel", "arbitrary")),
        name="rwkv7",
    )(z, shift0, s0, w["rw_mu"], w["rw_w0"], w["rw_w2"], w["rw_a0"], w["rw_a2"], w["rw_g2"],
      w["rw_kk"], w["rw_ka"], w["rw_rk"], w["rw_ln_g"], w["rw_ln_b"], w["head_sum"], s_all)


def _time_slab(t, g, tc):
    return pl.ds(g * SUBLANES * tc + t, SUBLANES, stride=tc)


def _lane_tiles(x):
    return [x[:, k * LANES:(k + 1) * LANES] for k in range(x.shape[-1] // LANES)]


def _lru_kernel(x_ref, gate_ref, buf0_ref, h0_ref, cw_ref, cb_ref, wax_ref, bax_ref, lam_ref,
                norm_ref, y_ref, buf_ref, h_ref, xbuf, abuf, bbuf, h_scr, *, bt, tc):
    c = pl.program_id(1)
    rows = bt * tc
    lo = SUBLANES - (CONV_W - 1)
    n_tiles = GROUP_W // LANES

    @pl.when(c == 0)
    def _():
        xbuf[:, lo:SUBLANES, :] = buf0_ref[...]
        h_scr[...] = h0_ref[...]

    x3 = x_ref[...]
    xbuf[:, SUBLANES:SUBLANES + tc, :] = x3
    cw = cw_ref[...]
    xc3 = cb_ref[...]
    for j in range(CONV_W):
        xc3 = xc3 + xbuf[:, lo + j:lo + j + tc, :] * cw[j:j + 1, :]
    tail = x3[:, tc - (CONV_W - 1):tc, :]
    xbuf[:, lo:SUBLANES, :] = tail
    buf_ref[...] = tail
    xc = xc3.reshape(rows, GROUP_W)

    gates = jax.nn.sigmoid(_dot(xc, wax_ref[...]) + bax_ref[...])
    gate_r, gate_i = gates[:, :GROUP_W], gates[:, GROUP_W:]
    log_a = -LRU_C * gate_r * jax.nn.softplus(-lam_ref[...])
    a = jnp.exp(log_a)
    bb = jnp.sqrt(-jnp.tanh(log_a) * (a * a + 1.0)) * (gate_i * xc)
    for k, (a_k, b_k) in enumerate(zip(_lane_tiles(a), _lane_tiles(bb))):
        abuf[k] = a_k
        bbuf[k] = b_k

    for g in range(bt // SUBLANES):
        seqs = slice(g * SUBLANES, (g + 1) * SUBLANES)
        h = _lane_tiles(h_scr[seqs, :])
        for t in range(tc):
            slab = _time_slab(t, g, tc)
            for k in range(n_tiles):
                h[k] = abuf[k, slab, :] * h[k] + bbuf[k, slab, :]
                bbuf[k, slab, :] = h[k]
        h_scr[seqs, :] = jnp.concatenate(h, axis=-1)
    h_ref[...] = h_scr[...]
    h_all = jnp.concatenate([bbuf[k] for k in range(n_tiles)], axis=-1)
    gate = gate_ref[...].reshape(rows, GROUP_W)
    y_ref[...] = _rms(h_all * jax.nn.gelu(gate), norm_ref[...]).reshape(bt, tc, GROUP_W)


def _lru_call(z, buf0, h0, w, layer, state_layer, bt, tc):
    b, l, _ = z.shape
    rows = bt * tc
    ws = functools.partial(_layer_spec, layer)
    cb = RW_COLS // GROUP_W
    tiles = pltpu.VMEM((GROUP_W // LANES, rows, LANES), F32)
    return pl.pallas_call(
        functools.partial(_lru_kernel, bt=bt, tc=tc),
        grid=(b // bt, l // tc),
        in_specs=[_seq_spec(bt, tc, GROUP_W, cb), _seq_spec(bt, tc, GROUP_W, cb + 1),
                  _stacked_state_spec(state_layer, bt, CONV_W - 1, GROUP_W),
                  _stacked_state_spec(state_layer, bt, GROUP_W),
                  ws(CONV_W, GROUP_W), ws(1, GROUP_W), ws(GROUP_W, 2 * GROUP_W),
                  ws(1, 2 * GROUP_W), ws(1, GROUP_W), ws(1, GROUP_W)],
        out_specs=[_seq_spec(bt, tc, GROUP_W, 0), _state_spec(bt, CONV_W - 1, GROUP_W),
                   _state_spec(bt, GROUP_W)],
        out_shape=[jax.ShapeDtypeStruct((b, l, GROUP_W), F32),
                   jax.ShapeDtypeStruct((b, CONV_W - 1, GROUP_W), F32),
                   jax.ShapeDtypeStruct((b, GROUP_W), F32)],
        scratch_shapes=[pltpu.VMEM((bt, SUBLANES + tc, GROUP_W), F32), tiles, tiles,
                        pltpu.VMEM((bt, GROUP_W), F32)],
        compiler_params=_params(("parallel", "arbitrary")),
        name="rglru",
    )(z, z, buf0, h0, w["lru_conv_w"], w["lru_conv_b"], w["lru_wax"], w["lru_bax"],
      w["lru_lambda"], w["lru_norm"])


def _s5_kernel(ulo_ref, uhi_ref, re0_ref, im0_ref, ar_ref, ai_ref, bb_ref, cc_ref, d_ref, gw_ref,
               gb_ref, norm_ref, y_ref, re_ref, im_ref, rbuf, ibuf, ybuf, re_scr, im_scr, *,
               bt, tc):
    c = pl.program_id(1)
    n_tiles = S5_STATE // LANES
    groups = bt // SUBLANES

    @pl.when(c == 0)
    def _():
        re_scr[...] = re0_ref[...]
        im_scr[...] = im0_ref[...]

    def time_major(ref):
        return jnp.concatenate([ref[g * SUBLANES:(g + 1) * SUBLANES, t, :]
                                for g in range(groups) for t in range(tc)], axis=0)

    u = jnp.concatenate([time_major(ulo_ref), time_major(uhi_ref)], axis=-1)
    bu = _dot(u, bb_ref[...])
    for k, (r_k, i_k) in enumerate(zip(_lane_tiles(bu[:, :S5_STATE]),
                                       _lane_tiles(bu[:, S5_STATE:]))):
        rbuf[k] = r_k
        ibuf[k] = i_k
    slab_shape = (SUBLANES, LANES)
    a_r = [jnp.broadcast_to(t, slab_shape) for t in _lane_tiles(ar_ref[...])]
    a_i = [jnp.broadcast_to(t, slab_shape) for t in _lane_tiles(ai_ref[...])]

    for g in range(groups):
        seqs = slice(g * SUBLANES, (g + 1) * SUBLANES)
        s_r, s_i = _lane_tiles(re_scr[seqs, :]), _lane_tiles(im_scr[seqs, :])
        for t in range(tc):
            slab = pl.ds((g * tc + t) * SUBLANES, SUBLANES)
            for k in range(n_tiles):
                n_r = a_r[k] * s_r[k] - a_i[k] * s_i[k] + rbuf[k, slab, :]
                n_i = a_r[k] * s_i[k] + a_i[k] * s_r[k] + ibuf[k, slab, :]
                s_r[k], s_i[k] = n_r, n_i
                rbuf[k, slab, :] = n_r
                ibuf[k, slab, :] = n_i
        re_scr[seqs, :] = jnp.concatenate(s_r, axis=-1)
        im_scr[seqs, :] = jnp.concatenate(s_i, axis=-1)
    re_ref[...] = re_scr[...]
    im_ref[...] = im_scr[...]

    x_all = jnp.concatenate([rbuf[k] for k in range(n_tiles)]
                            + [ibuf[k] for k in range(n_tiles)], axis=-1)
    y = _dot(x_all, cc_ref[...])
    y = jax.nn.gelu(y + d_ref[...] * u)
    y = y * jax.nn.sigmoid(_dot(y, gw_ref[...]) + gb_ref[...])
    for k, y_k in enumerate(_lane_tiles(_rms(y, norm_ref[...]))):
        ybuf[k] = y_k
    for g in range(groups):
        for j in range(SUBLANES):
            seq_rows = pl.ds(g * tc * SUBLANES + j, tc, stride=SUBLANES)
            y_ref[g * SUBLANES + j] = jnp.concatenate(
                [ybuf[k, seq_rows, :] for k in range(GROUP_W // LANES)], axis=-1)


def _s5_call(z, re0, im0, w, layer, state_layer, bt, tc):
    b, l, _ = z.shape
    rows = bt * tc
    ws = functools.partial(_layer_spec, layer)
    cb = (RW_COLS + 2 * GROUP_W) // LANES
    state = _state_spec(bt, S5_STATE)
    state_in = _stacked_state_spec(state_layer, bt, S5_STATE)
    tiles = pltpu.VMEM((S5_STATE // LANES, rows, LANES), F32)
    return pl.pallas_call(
        functools.partial(_s5_kernel, bt=bt, tc=tc),
        grid=(b // bt, l // tc),
        in_specs=[_seq_spec(bt, tc, LANES, cb), _seq_spec(bt, tc, LANES, cb + 1),
                  state_in, state_in,
                  ws(1, S5_STATE), ws(1, S5_STATE), ws(GROUP_W, 2 * S5_STATE),
                  ws(2 * S5_STATE, GROUP_W), ws(1, GROUP_W), ws(GROUP_W, GROUP_W),
                  ws(1, GROUP_W), ws(1, GROUP_W)],
        out_specs=[_seq_spec(bt, tc, GROUP_W, 0), state, state],
        out_shape=[jax.ShapeDtypeStruct((b, l, GROUP_W), F32),
                   jax.ShapeDtypeStruct((b, S5_STATE), F32),
                   jax.ShapeDtypeStruct((b, S5_STATE), F32)],
        scratch_shapes=[tiles, tiles, pltpu.VMEM((GROUP_W // LANES, rows, LANES), F32),
                        pltpu.VMEM((bt, S5_STATE), F32), pltpu.VMEM((bt, S5_STATE), F32)],
        compiler_params=_params(("parallel", "arbitrary")),
        name="s5",
    )(z, z, re0, im0, w["s5_ar"], w["s5_ai"], w["s5_bb"], w["s5_cc"], w["s5_d"], w["s5_glu_w"],
      w["s5_glu_b"], w["s5_norm"])


def _seq_cummax(x, bt, tc, buf):
    rows = bt * tc
    rin = lax.broadcasted_iota(jnp.int32, (rows, 1), 0) & (tc - 1)
    d = 1
    while d < tc:
        if d < SUBLANES:
            buf[SUBLANES:SUBLANES + rows, :] = x
            moved = buf[SUBLANES - d:SUBLANES - d + rows, :]
        else:
            moved = jnp.concatenate([x[:d], x[:rows - d]], axis=0)
        x = jnp.maximum(x, jnp.where(rin >= d, moved, -jnp.inf))
        d *= 2
    return x


def _mlstm_kernel(q_ref, k_ref, v_ref, og_ref, ig_ref, fg_ref, c0_ref, n0_ref, m0_ref, bi_ref,
                  bf_ref, lng_ref, hsum_ref, _c_all_ref, y_ref, c_ref, n_ref, m_ref,
                  c_scr, n_scr, m_scr, gbuf, *, bt, tc):
    c = pl.program_id(1)
    rows = bt * tc

    @pl.when(c == 0)
    def _():
        c_scr[...] = c0_ref[...]
        n_scr[...] = n0_ref[...]
        m_scr[...] = m0_ref[...]
        gbuf[0:SUBLANES, :] = jnp.zeros((SUBLANES, LANES), F32)

    flat = lambda ref: ref[...].reshape(rows, ref.shape[-1])
    log_i = flat(ig_ref) + bi_ref[...]
    log_f = jax.nn.log_sigmoid(flat(fg_ref) + bf_ref[...])
    _, incl = _seq_masks(bt, tc)
    bcum = _masked_sum(incl, log_f)
    m3 = m_scr[...]
    bcum3 = bcum.reshape(bt, tc, LANES)
    b_last3 = bcum3[:, tc - 1:tc, :]
    inter_all = (bcum3 + m3).reshape(rows, LANES)
    wk3 = b_last3 - bcum3 + log_i.reshape(bt, tc, LANES)
    m_new3 = jnp.maximum(b_last3 + m3, jnp.max(wk3, axis=1, keepdims=True))
    sc_all = jnp.exp(wk3 - m_new3).reshape(rows, LANES)
    decay3 = jnp.exp(b_last3 + m3 - m_new3)
    m_scr[...] = m_new3
    m_ref[...] = m_new3

    src = log_i - bcum
    mt_all = jnp.maximum(bcum + _seq_cummax(src, bt, tc, gbuf), inter_all)
    col_all = bcum - mt_all
    ei_all = jnp.exp(inter_all - mt_all)
    floor_all = jnp.exp(-mt_all)
    eye = (lax.broadcasted_iota(jnp.int32, (SUBLANES, LANES), 0)
           == lax.broadcasted_iota(jnp.int32, (SUBLANES, LANES), 1)).astype(F32)
    src_rows = _dot_nt(eye, src, HI)

    q, k, v = flat(q_ref), flat(k_ref) * (HEAD_DIM ** -0.5), flat(v_ref)
    ones = jnp.ones((rows, HEAD_DIM), F32)
    pad = jnp.zeros((HEAD_DIM - SUBLANES, HEAD_DIM), F32)
    heads = range(HEADS)
    lanes = [slice(h * HEAD_DIM, (h + 1) * HEAD_DIM) for h in heads]
    cols = [slice(h, h + 1) for h in heads]
    seqs = [slice(b * tc, (b + 1) * tc) for b in range(bt)]
    q_h, k_h, v_h = ([x[:, sl] for sl in lanes] for x in (q, k, v))
    pm = [jnp.where(incl, jnp.exp(col_all[:, cols[h]] + src_rows[cols[h], :]), 0.0)
          * _dot_nt(q_h[h].astype(BF16), k_h[h].astype(BF16)) for h in heads]
    intra = [_dot(pm[h], jnp.concatenate([v_h[h], ones], axis=-1)) for h in heads]
    c_old = [[c_scr[b, h] for b in range(bt)] for h in heads]
    n_old = [[jnp.broadcast_to(n_scr[b, cols[h], :], (SUBLANES, HEAD_DIM)) for b in range(bt)]
             for h in heads]
    from_state = [jnp.concatenate(
        [_dot_nt(q_h[h][rs].astype(BF16),
                 jnp.concatenate([c_old[h][b], n_old[h][b], pad], axis=0).astype(BF16))
         for b, rs in enumerate(seqs)], axis=0) for h in heads]
    nd = [intra[h] + ei_all[:, cols[h]] * from_state[h] for h in heads]
    o = jnp.concatenate(
        [nd[h][:, :HEAD_DIM] / jnp.maximum(jnp.abs(nd[h][:, HEAD_DIM:HEAD_DIM + 1]),
                                           floor_all[:, cols[h]]) for h in heads], axis=-1)
    for h in heads:
        sc = sc_all[:, cols[h]]
        scv = sc * v_h[h]
        for b, rs in enumerate(seqs):
            c_scr[b, h] = (decay3[b, :, cols[h]] * c_old[h][b]
                           + _dot_tn(scv[rs].astype(BF16), k_h[h][rs].astype(BF16)))
        sck = (sc * k_h[h]).reshape(bt, tc, HEAD_DIM)
        n_scr[:, cols[h], :] = (decay3[:, :, cols[h]] * n_scr[:, cols[h], :]
                                + jnp.sum(sck, axis=1, keepdims=True))
    hsum = hsum_ref[...]
    cen = o - _dot(o, hsum) * (1.0 / HEAD_DIM)
    var = _dot(cen * cen, hsum) * (1.0 / HEAD_DIM)
    y = cen * lax.rsqrt(var + EPS) * lng_ref[...] * jax.nn.sigmoid(flat(og_ref))
    y_ref[...] = y.reshape(bt, tc, GROUP_W)

    @pl.when(c == pl.num_programs(1) - 1)
    def _():
        c_ref[...] = c_scr[...]
        n_ref[...] = n_scr[...]


def _mlstm_call(z, c0, n0, m0, c_all, w, layer, state_layer, bt, tc):
    b, l, _ = z.shape
    ws = functools.partial(_layer_spec, layer)
    cb = (RW_COLS + 3 * GROUP_W) // GROUP_W
    mat = (HEADS, HEAD_DIM, HEAD_DIM)
    stacked = functools.partial(_stacked_state_spec, state_layer, bt)
    return pl.pallas_call(
        functools.partial(_mlstm_kernel, bt=bt, tc=tc),
        grid=(b // bt, l // tc),
        in_specs=[_seq_spec(bt, tc, GROUP_W, cb), _seq_spec(bt, tc, GROUP_W, cb + 1),
                  _seq_spec(bt, tc, GROUP_W, cb + 2), _seq_spec(bt, tc, GROUP_W, cb + 3),
                  _seq_spec(bt, tc, LANES, IGATE_COL_BLOCK),
                  _seq_spec(bt, tc, LANES, IGATE_COL_BLOCK + 1),
                  stacked(*mat), stacked(HEADS, HEAD_DIM), stacked(1, LANES),
                  ws(1, LANES), ws(1, LANES), ws(1, GROUP_W),
                  _const_spec((GROUP_W, GROUP_W)), ANY_SPEC],
        out_specs=[_seq_spec(bt, tc, GROUP_W, 0), _stacked_state_spec(layer, bt, *mat),
                   _state_spec(bt, HEADS, HEAD_DIM), _state_spec(bt, 1, LANES)],
        out_shape=[jax.ShapeDtypeStruct((b, l, GROUP_W), F32),
                   jax.ShapeDtypeStruct(c_all.shape, F32),
                   jax.ShapeDtypeStruct((b, HEADS, HEAD_DIM), F32),
                   jax.ShapeDtypeStruct((b, 1, LANES), F32)],
        input_output_aliases={13: 1},
        scratch_shapes=[pltpu.VMEM((bt,) + mat, F32),
                        pltpu.VMEM((bt, HEADS, HEAD_DIM), F32),
                        pltpu.VMEM((bt, 1, LANES), F32),
                        pltpu.VMEM((SUBLANES + bt * tc, LANES), F32)],
        compiler_params=_params(("parallel", "arbitrary")),
        name="mlstm",
    )(z, z, z, z, z, z, c0, n0, m0, w["ml_bi"], w["ml_bf"], w["ml_ln_g"], w["head_sum"], c_all)


def _block_diag(blocks):
    *lead, n, r, c = blocks.shape
    eye = jnp.eye(n, dtype=blocks.dtype)
    out = eye[:, None, :, None] * blocks[..., :, :, None, :]
    return out.reshape(*lead, n * r, n * c)


def _prep(w):
    depth = w["ffn1_pre"].shape[0]
    row = lambda t: t.reshape(depth, 1, -1).astype(F32)
    out = {}
    for f in ("ffn1", "ffn2"):
        out[f + "_pre"] = row(w[f + "_pre"])
        out[f + "_post"] = row(w[f + "_post"])
        out[f + "_wg"] = _cast_t_call(jnp.swapaxes(w[f + "_wg"], 1, 2), D_FF_PAD,
                                      _row_chunks(D_FF))
        out[f + "_wu"] = _cast_t_call(jnp.swapaxes(w[f + "_wu"], 1, 2), D_FF_PAD,
                                      _row_chunks(D_FF))
        out[f + "_wd"] = _cast_call(w[f + "_wd"], D_FF_PAD, D_MODEL)
    out["mix_pre"] = row(w["mix_pre"])
    out["mix_post"] = row(w["mix_post"])
    out["w_in"] = _cast_t_call(
        jnp.swapaxes(w["w_in"], 1, 2), IN_COLS_PAD,
        _row_chunks(MAIN_COLS) + ((MAIN_COLS, HEADS, MAIN_COLS),
                                  (MAIN_COLS + HEADS, HEADS, MAIN_COLS + LANES)))
    out["w_out"] = _cast_call(w["w_out"], D_MODEL, D_MODEL)

    for name in ("rw_mu", "rw_w0", "rw_a0", "rw_kk", "rw_ka", "rw_rk", "rw_ln_g", "rw_ln_b"):
        out[name] = row(w[name])
    for name in ("rw_w2", "rw_a2", "rw_g2"):
        out[name] = w[name].astype(BF16)
    out["head_sum"] = _block_diag(jnp.ones((HEADS, HEAD_DIM, HEAD_DIM), BF16))

    out["lru_conv_w"] = w["lru_conv_w"].astype(F32)
    out["lru_conv_b"] = row(w["lru_conv_b"])
    out["lru_wax"] = jnp.concatenate(
        [_block_diag(w["lru_wa"]), _block_diag(w["lru_wx"])], axis=-1).astype(BF16)
    out["lru_bax"] = jnp.concatenate([row(w["lru_ba"]), row(w["lru_bx"])], axis=-1)
    out["lru_lambda"] = row(w["lru_lambda"])
    out["lru_norm"] = row(w["lru_norm"])

    lr, li = w["s5_a_re"].astype(F32), w["s5_a_im"].astype(F32)
    dt = jnp.exp(w["s5_log_dt"].astype(F32))[..., None]
    mag = jnp.exp(lr * dt)
    ar, ai = mag * jnp.cos(li * dt), mag * jnp.sin(li * dt)
    den = lr * lr + li * li
    zr = ((ar - 1.0) * lr + ai * li) / den
    zi = (ai * lr - (ar - 1.0) * li) / den
    b_re, b_im = w["s5_b_re"].astype(F32), w["s5_b_im"].astype(F32)
    bbr = zr[..., None] * b_re - zi[..., None] * b_im
    bbi = zr[..., None] * b_im + zi[..., None] * b_re
    packed = lambda t: _block_diag(jnp.swapaxes(t, -1, -2))
    out["s5_ar"], out["s5_ai"] = row(ar), row(ai)
    out["s5_bb"] = jnp.concatenate([packed(bbr), packed(bbi)], axis=-1).astype(BF16)
    out["s5_cc"] = jnp.concatenate(
        [packed(w["s5_c_re"]), -packed(w["s5_c_im"])], axis=-2).astype(BF16)
    out["s5_d"] = row(w["s5_d"])
    out["s5_glu_w"] = w["s5_glu_w"].astype(BF16)
    out["s5_glu_b"] = row(w["s5_glu_b"])
    out["s5_norm"] = row(w["s5_norm"])

    lane_block = lambda t: jnp.pad(t.astype(F32), ((0, 0), (0, LANES - HEADS))).reshape(
        depth, 1, LANES)
    out["ml_bi"] = lane_block(w["ml_bi"])
    out["ml_bf"] = lane_block(w["ml_bf"])
    out["ml_ln_g"] = row(w["ml_ln_g"])
    return out


def _tiling(b, l, rows_cap, tc_cap, seq_multiple=1):
    tc = tc_cap
    while tc > SUBLANES and l % tc:
        tc //= 2
    if l % tc or b % seq_multiple:
        raise ValueError(f"unsupported batch {b} / sequence length {l}")
    bt = seq_multiple
    while bt * 2 * tc <= rows_cap and b % (bt * 2) == 0:
        bt *= 2
    return bt, tc


def _stack_states(states):
    shift, wkv, conv, h, re, im, c, n, m = (t.astype(F32) for t in states)
    layers, b = shift.shape[:2]
    return (shift.reshape(layers, b, 1, RW_COLS), wkv, conv, h,
            re.reshape(layers, b, S5_STATE), im.reshape(layers, b, S5_STATE), c, n,
            jnp.pad(m, ((0, 0), (0, 0), (0, LANES - HEADS))).reshape(layers, b, 1, LANES))


def _zero_states(b):
    mat = (1, b, HEADS, HEAD_DIM, HEAD_DIM)
    return _stack_states((
        jnp.zeros((1, b, RW_COLS), F32), jnp.zeros(mat, F32),
        jnp.zeros((1, b, CONV_W - 1, GROUP_W), F32), jnp.zeros((1, b, GROUP_W), F32),
        jnp.zeros((1, b, S5_GROUPS, S5_P), F32), jnp.zeros((1, b, S5_GROUPS, S5_P), F32),
        jnp.zeros(mat, F32), jnp.zeros((1, b, HEADS, HEAD_DIM), F32),
        jnp.zeros((1, b, HEADS), F32)))


def _layer(x, st, state_layer, mats, w, layer):
    b, l, d = x.shape
    n = b * l
    shift0, s0, buf0, h0, re0, im0, c0, n0, m0 = st
    s_all, c_all = mats
    x1, z = _ffn_in_call(x.reshape(n, d), w, layer)
    z = z.reshape(b, l, IN_COLS_PAD)

    y_rw, n_shift, s_all = _rwkv_call(z, shift0, s0, s_all, w, layer, state_layer,
                                      *_tiling(b, l, 256, 32))
    y_lru, n_buf, n_h = _lru_call(z, buf0, h0, w, layer, state_layer,
                                  *_tiling(b, l, 1024, 128, SUBLANES))
    y_s5, n_re, n_im = _s5_call(z, re0, im0, w, layer, state_layer,
                                *_tiling(b, l, 512, 64, SUBLANES))
    y_ml, c_all, n_n, n_m = _mlstm_call(z, c0, n0, m0, c_all, w, layer, state_layer,
                                        *_tiling(b, l, 256, 128))

    flat = lambda t: t.reshape(n, GROUP_W)
    x3 = _out_ffn_call(x1, flat(y_rw), flat(y_lru), flat(y_s5), flat(y_ml), w, layer)
    new = (n_shift.reshape(b, RW_COLS), n_buf, n_h,
           n_re.reshape(b, S5_GROUPS, S5_P), n_im.reshape(b, S5_GROUPS, S5_P), n_n,
           n_m.reshape(b, LANES)[:, :HEADS])
    return x3.reshape(b, l, d), new, (s_all, c_all)


def kernel(x_prompt, x_sample, state_rwkv_shift, state_rwkv_wkv, state_lru_conv, state_lru_h, state_s5_re, state_s5_im, state_mlstm_C, state_mlstm_n, state_mlstm_m, ffn1_pre, ffn1_wg, ffn1_wu, ffn1_wd, ffn1_post, mix_pre, w_in, w_out, mix_post, ffn2_pre, ffn2_wg, ffn2_wu, ffn2_wd, ffn2_post, rw_mu, rw_w0, rw_w2, rw_a0, rw_a2, rw_g2, rw_kk, rw_ka, rw_rk, rw_ln_g, rw_ln_b, lru_conv_w, lru_conv_b, lru_wa, lru_ba, lru_wx, lru_bx, lru_lambda, lru_norm, s5_a_re, s5_a_im, s5_log_dt, s5_b_re, s5_b_im, s5_c_re, s5_c_im, s5_d, s5_glu_w, s5_glu_b, s5_norm, ml_bi, ml_bf, ml_ln_g):
    w = _prep(dict(
        ffn1_pre=ffn1_pre, ffn1_wg=ffn1_wg, ffn1_wu=ffn1_wu, ffn1_wd=ffn1_wd, ffn1_post=ffn1_post,
        mix_pre=mix_pre, w_in=w_in, w_out=w_out, mix_post=mix_post,
        ffn2_pre=ffn2_pre, ffn2_wg=ffn2_wg, ffn2_wu=ffn2_wu, ffn2_wd=ffn2_wd, ffn2_post=ffn2_post,
        rw_mu=rw_mu, rw_w0=rw_w0, rw_w2=rw_w2, rw_a0=rw_a0, rw_a2=rw_a2, rw_g2=rw_g2,
        rw_kk=rw_kk, rw_ka=rw_ka, rw_rk=rw_rk, rw_ln_g=rw_ln_g, rw_ln_b=rw_ln_b,
        lru_conv_w=lru_conv_w, lru_conv_b=lru_conv_b, lru_wa=lru_wa, lru_ba=lru_ba,
        lru_wx=lru_wx, lru_bx=lru_bx, lru_lambda=lru_lambda, lru_norm=lru_norm,
        s5_a_re=s5_a_re, s5_a_im=s5_a_im, s5_log_dt=s5_log_dt, s5_b_re=s5_b_re, s5_b_im=s5_b_im,
        s5_c_re=s5_c_re, s5_c_im=s5_c_im, s5_d=s5_d, s5_glu_w=s5_glu_w, s5_glu_b=s5_glu_b,
        s5_norm=s5_norm, ml_bi=ml_bi, ml_bf=ml_bf, ml_ln_g=ml_ln_g))
    sample_state = _stack_states((state_rwkv_shift, state_rwkv_wkv, state_lru_conv, state_lru_h,
                                  state_s5_re, state_s5_im, state_mlstm_C, state_mlstm_n,
                                  state_mlstm_m))
    depth = ffn1_pre.shape[0]
    b_p = x_prompt.shape[0]
    prompt_state = _zero_states(b_p)
    mat_zeros = lambda b: jnp.zeros((depth, b, HEADS, HEAD_DIM, HEAD_DIM), F32)
    mats_p = (mat_zeros(b_p), mat_zeros(b_p))
    mats_s = (mat_zeros(x_sample.shape[0]), mat_zeros(x_sample.shape[0]))
    y_p, y_s = x_prompt.astype(F32), x_sample.astype(F32)
    outs_p, outs_s = [], []
    for layer in range(depth):
        y_p, st_p, mats_p = _layer(y_p, prompt_state, 0, mats_p, w, layer)
        y_s, st_s, mats_s = _layer(y_s, sample_state, layer, mats_s, w, layer)
        outs_p.append(st_p)
        outs_s.append(st_s)

    def gather(outs, mats):
        shift, conv, h, re, im, n, m = (jnp.stack([o[i] for o in outs]) for i in range(7))
        return (shift, mats[0], conv, h, re, im, mats[1], n, m)

    return (y_p, y_s, *gather(outs_p, mats_p), *gather(outs_s, mats_s))
```

```python
import functools

import jax
import jax.numpy as jnp
from jax import lax
from jax.experimental import pallas as pl
from jax.experimental.pallas import tpu as pltpu

F32 = jnp.float32
BF16 = jnp.bfloat16

LANES = 128
SUBLANES = 8
VMEM_LIMIT_BYTES = 56 * 1024 * 1024

D_MODEL = 1024
GROUP_W = D_MODEL // 4
HEAD_DIM = 64
HEADS = GROUP_W // HEAD_DIM
RW_W_RANK = 64
RW_A_RANK = 64
RW_G_RANK = 128
RW_COLS = 3 * GROUP_W + RW_W_RANK + RW_A_RANK + RW_G_RANK
RW_DECAY_SCALE = 0.606531
RW_GN_EPS = 64e-5
LRU_C = 8.0
CONV_W = 4
S5_GROUP = 16
S5_GROUPS = GROUP_W // S5_GROUP
S5_P = 64
S5_STATE = S5_GROUPS * S5_P
MAIN_COLS = RW_COLS + 7 * GROUP_W
IN_COLS = MAIN_COLS + 2 * HEADS
IN_COLS_PAD = MAIN_COLS + 2 * LANES
IGATE_COL_BLOCK = MAIN_COLS // LANES
D_FF = 2752
D_FF_PAD = -(-D_FF // LANES) * LANES
EPS = 1e-6
HI = lax.Precision.HIGHEST


def _dot(a, b):
    return jnp.dot(a.astype(BF16), b.astype(BF16), preferred_element_type=F32)


def _dot_nt(a, b, precision=None):
    return lax.dot_general(a, b, (((1,), (1,)), ((), ())), precision=precision,
                           preferred_element_type=F32)


def _dot_tn(a, b):
    return lax.dot_general(a, b, (((0,), (0,)), ((), ())), preferred_element_type=F32)


def _rms(x, g):
    return x * lax.rsqrt(jnp.mean(x * x, axis=-1, keepdims=True) + EPS) * g


def _seq_masks(bt, tc, reps=1):
    rows = bt * tc
    r = lax.broadcasted_iota(jnp.int32, (rows, reps * rows), 0)
    c = lax.broadcasted_iota(jnp.int32, (rows, reps * rows), 1) & (rows - 1)
    if bt == 1:
        return r > c, r >= c
    start = r - (r & (tc - 1))
    return (c < r) & (c >= start), (c <= r) & (c >= start)


def _later_mask(bt, tc, reps=1):
    rows = bt * tc
    r = lax.broadcasted_iota(jnp.int32, (reps * rows, rows), 0) & (rows - 1)
    c = lax.broadcasted_iota(jnp.int32, (reps * rows, rows), 1)
    if bt == 1:
        return r < c
    return (r < c) & (r >= c - (c & (tc - 1)))


def _masked_sum(mask, x):
    m = mask.astype(BF16)
    hi = x.astype(BF16)
    rest = x - hi.astype(F32)
    mid = rest.astype(BF16)
    lo = (rest - mid.astype(F32)).astype(BF16)
    dot = lambda piece: jnp.dot(m, piece, preferred_element_type=F32)
    return dot(hi) + dot(mid) + dot(lo)


def _rows(x3, tc):
    bt, _, w = x3.shape
    return jnp.broadcast_to(x3, (bt, tc, w)).reshape(bt * tc, w)


def _const_spec(shape):
    nd = len(shape)
    return pl.BlockSpec(shape, lambda *_: (0,) * nd, pipeline_mode=pl.Buffered(1))


def _layer_spec(layer, *shape):
    zeros = (0,) * len(shape)
    return pl.BlockSpec((None,) + shape, lambda *_: (layer,) + zeros,
                        pipeline_mode=pl.Buffered(1))


def _params(sem):
    return pltpu.CompilerParams(dimension_semantics=sem, vmem_limit_bytes=VMEM_LIMIT_BYTES)


def _cast_kernel(x_ref, o_ref):
    rows, cols = x_ref.shape
    o_ref[...] = jnp.zeros(o_ref.shape, BF16)
    o_ref[0:rows, 0:cols] = x_ref[...].astype(BF16)


def _cast_call(x, out_rows, out_cols):
    depth, rows, cols = x.shape
    return pl.pallas_call(
        _cast_kernel,
        grid=(depth,),
        in_specs=[pl.BlockSpec((None, rows, cols), lambda i: (i, 0, 0))],
        out_specs=pl.BlockSpec((None, out_rows, out_cols), lambda i: (i, 0, 0)),
        out_shape=jax.ShapeDtypeStruct((depth, out_rows, out_cols), BF16),
        compiler_params=_params(("parallel",)),
        name="cast_weights",
    )(x)


def _cast_t_kernel(x_ref, o_ref, *, segments):
    cols = x_ref.shape[1]
    for src, n, dst in segments:
        chunk = x_ref[src:src + n, :]
        if n < LANES:
            chunk = jnp.concatenate([chunk, jnp.zeros((LANES - n, cols), chunk.dtype)], axis=0)
        o_ref[:, dst:dst + LANES] = chunk.T.astype(BF16)


def _row_chunks(n):
    return tuple((i, min(LANES, n - i), i) for i in range(0, n, LANES))


def _cast_t_call(x_t, out_cols, segments):
    depth, rows, cols = x_t.shape
    assert len(segments) * LANES == out_cols
    return pl.pallas_call(
        functools.partial(_cast_t_kernel, segments=segments),
        grid=(depth,),
        in_specs=[pl.BlockSpec((None, rows, cols), lambda i: (i, 0, 0))],
        out_specs=pl.BlockSpec((None, cols, out_cols), lambda i: (i, 0, 0)),
        out_shape=jax.ShapeDtypeStruct((depth, cols, out_cols), BF16),
        compiler_params=_params(("parallel",)),
        name="cast_weights_t",
    )(x_t)


FFN_ROW_SPLIT = 2


def _row_parts(n):
    step = n // FFN_ROW_SPLIT
    return [slice(i * step, (i + 1) * step) for i in range(FFN_ROW_SPLIT)]


def _swiglu_half_step(xs, pre, wg, wu, wd, post):
    hs = [_rms(x, pre).astype(BF16) for x in xs]
    gs = [jnp.dot(h, wg, preferred_element_type=F32) for h in hs]
    us = [jnp.dot(h, wu, preferred_element_type=F32) for h in hs]
    acts = [(jax.nn.silu(g) * u).astype(BF16) for g, u in zip(gs, us)]
    fs = [jnp.dot(act, wd, preferred_element_type=F32) for act in acts]
    return [x + 0.5 * _rms(f, post) for x, f in zip(xs, fs)]


def _ffn_in_kernel(x_ref, pre_ref, wg_ref, wu_ref, wd_ref, post_ref, mpre_ref, win_ref,
                   xo_ref, z_ref):
    parts = _row_parts(x_ref.shape[0])
    x1 = _swiglu_half_step([x_ref[rs, :] for rs in parts], pre_ref[...], wg_ref[...],
                           wu_ref[...], wd_ref[...], post_ref[...])
    hs = [_rms(x, mpre_ref[...]).astype(BF16) for x in x1]
    zs = [jnp.dot(h, win_ref[...], preferred_element_type=F32) for h in hs]
    for rs, x, z in zip(parts, x1, zs):
        xo_ref[rs, :] = x
        z_ref[rs, :] = z


def _out_ffn_kernel(x_ref, yrw_ref, ylru_ref, ys5_ref, yml_ref, wout_ref, mpost_ref,
                    pre_ref, wg_ref, wu_ref, wd_ref, post_ref, xo_ref):
    parts = _row_parts(x_ref.shape[0])
    cats = [jnp.concatenate([yrw_ref[rs, :], ylru_ref[rs, :], ys5_ref[rs, :], yml_ref[rs, :]],
                            axis=-1).astype(BF16) for rs in parts]
    ms = [jnp.dot(cat, wout_ref[...], preferred_element_type=F32) for cat in cats]
    x2 = [x_ref[rs, :] + _rms(m, mpost_ref[...]) for rs, m in zip(parts, ms)]
    x3 = _swiglu_half_step(x2, pre_ref[...], wg_ref[...], wu_ref[...], wd_ref[...],
                           post_ref[...])
    for rs, x in zip(parts, x3):
        xo_ref[rs, :] = x


def _token_tile(n):
    for tm in (512, 256, 128, 64, 32, 16, 8):
        if n % tm == 0:
            return tm
    raise ValueError(f"token count {n} is not a multiple of {SUBLANES}")


def _ffn_specs(ws):
    return [ws(1, D_MODEL), ws(D_MODEL, D_FF_PAD), ws(D_MODEL, D_FF_PAD), ws(D_FF_PAD, D_MODEL),
            ws(1, D_MODEL)]


def _ffn_in_call(x, w, layer):
    n = x.shape[0]
    tm = _token_tile(n)
    row = lambda width: pl.BlockSpec((tm, width), lambda i: (i, 0))
    ws = functools.partial(_layer_spec, layer)
    return pl.pallas_call(
        _ffn_in_kernel,
        grid=(n // tm,),
        in_specs=[row(D_MODEL)] + _ffn_specs(ws) + [ws(1, D_MODEL), ws(D_MODEL, IN_COLS_PAD)],
        out_specs=[row(D_MODEL), row(IN_COLS_PAD)],
        out_shape=[jax.ShapeDtypeStruct((n, D_MODEL), F32),
                   jax.ShapeDtypeStruct((n, IN_COLS_PAD), F32)],
        compiler_params=_params(("parallel",)),
        name="ffn1_inproj",
    )(x, w["ffn1_pre"], w["ffn1_wg"], w["ffn1_wu"], w["ffn1_wd"], w["ffn1_post"], w["mix_pre"],
      w["w_in"])


def _out_ffn_call(x, yrw, ylru, ys5, yml, w, layer):
    n = x.shape[0]
    tm = _token_tile(n)
    row = lambda width: pl.BlockSpec((tm, width), lambda i: (i, 0))
    ws = functools.partial(_layer_spec, layer)
    return pl.pallas_call(
        _out_ffn_kernel,
        grid=(n // tm,),
        in_specs=[row(D_MODEL), row(GROUP_W), row(GROUP_W), row(GROUP_W), row(GROUP_W),
                  ws(D_MODEL, D_MODEL), ws(1, D_MODEL)] + _ffn_specs(ws),
        out_specs=row(D_MODEL),
        out_shape=jax.ShapeDtypeStruct((n, D_MODEL), F32),
        compiler_params=_params(("parallel",)),
        name="outproj_ffn2",
    )(x, yrw, ylru, ys5, yml, w["w_out"], w["mix_post"], w["ffn2_pre"], w["ffn2_wg"],
      w["ffn2_wu"], w["ffn2_wd"], w["ffn2_post"])


def _seq_spec(bt, tc, w, col_block):
    return pl.BlockSpec((bt, tc, w), lambda i, c: (i, c, col_block))


def _state_spec(bt, *dims):
    zeros = (0,) * len(dims)
    return pl.BlockSpec((bt,) + dims, lambda i, c: (i,) + zeros)


def _stacked_state_spec(layer, bt, *dims):
    zeros = (0,) * len(dims)
    return pl.BlockSpec((None, bt) + dims, lambda i, c: (layer, i) + zeros)


ANY_SPEC = pl.BlockSpec(memory_space=pl.ANY)


def _rwkv_kernel(p_ref, shift0_ref, s0_ref, mu_ref, w0_ref, w2_ref, a0_ref, a2_ref, g2_ref,
                 kkw_ref, ka_ref, rk_ref, lng_ref, lnb_ref, hsum_ref, _s_all_ref,
                 y_ref, shift_ref, s_ref, pbuf, s_scr, *, bt, tc):
    c = pl.program_id(1)
    rows = bt * tc
    lo = SUBLANES - 1

    @pl.when(c == 0)
    def _():
        pbuf[:, lo:SUBLANES, :] = shift0_ref[...]
        s_scr[...] = s0_ref[...]

    p3 = p_ref[...]
    pbuf[:, SUBLANES:SUBLANES + tc, :] = p3
    prev = pbuf[:, lo:lo + tc, :].reshape(rows, RW_COLS)
    last = p3[:, tc - 1:tc, :]
    pbuf[:, lo:SUBLANES, :] = last
    shift_ref[...] = last
    p = p3.reshape(rows, RW_COLS)

    xm = p + (prev - p) * mu_ref[...]
    g = GROUP_W
    r, k, v = xm[:, 0:g], xm[:, g:2 * g], xm[:, 2 * g:3 * g]
    lw = xm[:, 3 * g:3 * g + RW_W_RANK]
    la = xm[:, 3 * g + RW_W_RANK:3 * g + RW_W_RANK + RW_A_RANK]
    lg = xm[:, 3 * g + RW_W_RANK + RW_A_RANK:]
    logw = -RW_DECAY_SCALE * jax.nn.sigmoid(w0_ref[...] + _dot(jnp.tanh(lw), w2_ref[...]))
    a = jax.nn.sigmoid(a0_ref[...] + _dot(la, a2_ref[...]))
    gate = _dot(jax.nn.sigmoid(lg), g2_ref[...])

    hsum = hsum_ref[...]
    kk = k * kkw_ref[...]
    kk = kk * lax.rsqrt(jnp.maximum(_dot(kk * kk, hsum), 1e-12))
    k2 = k * (1.0 + (a - 1.0) * ka_ref[...])

    _, incl = _seq_masks(bt, tc)
    _, incl2 = _seq_masks(bt, tc, reps=2)
    later2 = _later_mask(bt, tc, reps=2)
    cum = _masked_sum(incl, logw)
    cum3 = cum.reshape(bt, tc, g)
    total3 = cum3[:, tc - 1:tc, :]
    rest = jnp.exp(total3 - cum3).reshape(rows, g)
    wtot3 = jnp.exp(total3)
    inv = jnp.exp(-cum)
    kka = kk * a
    at = -kk * jnp.exp(cum - logw)
    rt = r * jnp.exp(cum)
    bk_t = kka * inv, k2 * inv
    bk_w = kka * rest, k2 * rest

    n_double = max(1, (tc - 1).bit_length())
    heads = range(HEADS)
    lanes = [slice(h * HEAD_DIM, (h + 1) * HEAD_DIM) for h in heads]
    seqs = [slice(b * tc, (b + 1) * tc) for b in range(bt)]
    a_h, r_h, v_h = ([x[:, sl] for sl in lanes] for x in (at, rt, v))
    bk_h = [jnp.concatenate([bk_t[0][:, sl], bk_t[1][:, sl]], axis=0).astype(BF16)
            for sl in lanes]
    m_at = [jnp.where(later2, _dot_nt(bk_h[h], a_h[h].astype(BF16)), 0.0) for h in heads]
    m_r = [jnp.where(incl2, _dot_nt(r_h[h].astype(BF16), bk_h[h]), 0.0) for h in heads]
    s_old = [[s_scr[b, h] for b in range(bt)] for h in heads]
    from_state = [[_dot_nt(jnp.concatenate([a_h[h][rs], r_h[h][rs]], axis=0).astype(BF16),
                           s_old[h][b].astype(BF16)) for b, rs in enumerate(seqs)]
                  for h in heads]
    u0_t = jnp.concatenate([jnp.concatenate([fs[:tc] for fs in from_state[h]], axis=0)
                            for h in heads], axis=-1).T
    v_t = v.T
    u_t = [u0_t[sl] + _dot(v_t[sl], m_at[h][rows:]) for h, sl in enumerate(lanes)]
    pw = [m_at[h][:rows] for h in heads]
    for i in range(n_double):
        u_t = [u_t[h] + _dot(u_t[h], pw[h]) for h in heads]
        if i + 1 < n_double:
            pw = [_dot(pw[h], pw[h]) for h in heads]
    u_all = jnp.concatenate(u_t, axis=0).T
    uv = [jnp.concatenate([u_all[:, sl], v_h[h]], axis=0) for h, sl in enumerate(lanes)]
    o = jnp.concatenate(
        [jnp.concatenate([fs[tc:] for fs in from_state[h]], axis=0) + _dot(m_r[h], uv[h])
         for h in heads], axis=-1)
    for h, sl in enumerate(lanes):
        bkw = jnp.concatenate([bk_w[0][:, sl], bk_w[1][:, sl]], axis=0)
        for b, rs in enumerate(seqs):
            rs2 = slice(rows + rs.start, rows + rs.stop)
            uv_b = jnp.concatenate([uv[h][rs], uv[h][rs2]], axis=0)
            bkw_b = jnp.concatenate([bkw[rs], bkw[rs2]], axis=0)
            s_scr[b, h] = (s_old[h][b] * wtot3[b, :, sl]
                           + _dot_tn(uv_b.astype(BF16), bkw_b.astype(BF16)))

    mean = _dot(o, hsum) * (1.0 / HEAD_DIM)
    cen = o - mean
    var = _dot(cen * cen, hsum) * (1.0 / HEAD_DIM)
    y = cen * lax.rsqrt(var + RW_GN_EPS) * lng_ref[...] + lnb_ref[...]
    bonus = _dot(r * k2 * rk_ref[...], hsum) * v
    y_ref[...] = ((y + bonus) * gate).reshape(bt, tc, g)

    @pl.when(c == pl.num_programs(1) - 1)
    def _():
        s_ref[...] = s_scr[...]


def _rwkv_call(z, shift0, s0, s_all, w, layer, state_layer, bt, tc):
    b, l, _ = z.shape
    ws = functools.partial(_layer_spec, layer)
    mat = (HEADS, HEAD_DIM, HEAD_DIM)
    return pl.pallas_call(
        functools.partial(_rwkv_kernel, bt=bt, tc=tc),
        grid=(b // bt, l // tc),
        in_specs=[_seq_spec(bt, tc, RW_COLS, 0),
                  _stacked_state_spec(state_layer, bt, 1, RW_COLS),
                  _stacked_state_spec(state_layer, bt, *mat),
                  ws(1, RW_COLS), ws(1, GROUP_W), ws(RW_W_RANK, GROUP_W), ws(1, GROUP_W),
                  ws(RW_A_RANK, GROUP_W), ws(RW_G_RANK, GROUP_W),
                  ws(1, GROUP_W), ws(1, GROUP_W), ws(1, GROUP_W), ws(1, GROUP_W),
                  ws(1, GROUP_W), _const_spec((GROUP_W, GROUP_W)), ANY_SPEC],
        out_specs=[_seq_spec(bt, tc, GROUP_W, 0), _state_spec(bt, 1, RW_COLS),
                   _stacked_state_spec(layer, bt, *mat)],
        out_shape=[jax.ShapeDtypeStruct((b, l, GROUP_W), F32),
                   jax.ShapeDtypeStruct((b, 1, RW_COLS), F32),
                   jax.ShapeDtypeStruct(s_all.shape, F32)],
        input_output_aliases={15: 2},
        scratch_shapes=[pltpu.VMEM((bt, SUBLANES + tc, RW_COLS), F32),
                        pltpu.VMEM((bt,) + mat, F32)],
        compiler_params=_params(("parallel", "arbitrary")),
        name="rwkv7",
    )(z, shift0, s0, w["rw_mu"], w["rw_w0"], w["rw_w2"], w["rw_a0"], w["rw_a2"], w["rw_g2"],
      w["rw_kk"], w["rw_ka"], w["rw_rk"], w["rw_ln_g"], w["rw_ln_b"], w["head_sum"], s_all)


def _time_slab(t, g, tc):
    return pl.ds(g * SUBLANES * tc + t, SUBLANES, stride=tc)


def _lane_tiles(x):
    return [x[:, k * LANES:(k + 1) * LANES] for k in range(x.shape[-1] // LANES)]


def _lru_kernel(x_ref, gate_ref, buf0_ref, h0_ref, cw_ref, cb_ref, wax_ref, bax_ref, lam_ref,
                norm_ref, y_ref, buf_ref, h_ref, xbuf, abuf, bbuf, h_scr, *, bt, tc):
    c = pl.program_id(1)
    rows = bt * tc
    lo = SUBLANES - (CONV_W - 1)
    n_tiles = GROUP_W // LANES

    @pl.when(c == 0)
    def _():
        xbuf[:, lo:SUBLANES, :] = buf0_ref[...]
        h_scr[...] = h0_ref[...]

    x3 = x_ref[...]
    xbuf[:, SUBLANES:SUBLANES + tc, :] = x3
    cw = cw_ref[...]
    xc3 = cb_ref[...]
    for j in range(CONV_W):
        xc3 = xc3 + xbuf[:, lo + j:lo + j + tc, :] * cw[j:j + 1, :]
    tail = x3[:, tc - (CONV_W - 1):tc, :]
    xbuf[:, lo:SUBLANES, :] = tail
    buf_ref[...] = tail
    xc = xc3.reshape(rows, GROUP_W)

    gates = jax.nn.sigmoid(_dot(xc, wax_ref[...]) + bax_ref[...])
    gate_r, gate_i = gates[:, :GROUP_W], gates[:, GROUP_W:]
    log_a = -LRU_C * gate_r * jax.nn.softplus(-lam_ref[...])
    a = jnp.exp(log_a)
    bb = jnp.sqrt(-jnp.tanh(log_a) * (a * a + 1.0)) * (gate_i * xc)
    for k, (a_k, b_k) in enumerate(zip(_lane_tiles(a), _lane_tiles(bb))):
        abuf[k] = a_k
        bbuf[k] = b_k

    for g in range(bt // SUBLANES):
        seqs = slice(g * SUBLANES, (g + 1) * SUBLANES)
        h = _lane_tiles(h_scr[seqs, :])
        for t in range(tc):
            slab = _time_slab(t, g, tc)
            for k in range(n_tiles):
                h[k] = abuf[k, slab, :] * h[k] + bbuf[k, slab, :]
                bbuf[k, slab, :] = h[k]
        h_scr[seqs, :] = jnp.concatenate(h, axis=-1)
    h_ref[...] = h_scr[...]
    h_all = jnp.concatenate([bbuf[k] for k in range(n_tiles)], axis=-1)
    gate = gate_ref[...].reshape(rows, GROUP_W)
    y_ref[...] = _rms(h_all * jax.nn.gelu(gate), norm_ref[...]).reshape(bt, tc, GROUP_W)


def _lru_call(z, buf0, h0, w, layer, state_layer, bt, tc):
    b, l, _ = z.shape
    rows = bt * tc
    ws = functools.partial(_layer_spec, layer)
    cb = RW_COLS // GROUP_W
    tiles = pltpu.VMEM((GROUP_W // LANES, rows, LANES), F32)
    return pl.pallas_call(
        functools.partial(_lru_kernel, bt=bt, tc=tc),
        grid=(b // bt, l // tc),
        in_specs=[_seq_spec(bt, tc, GROUP_W, cb), _seq_spec(bt, tc, GROUP_W, cb + 1),
                  _stacked_state_spec(state_layer, bt, CONV_W - 1, GROUP_W),
                  _stacked_state_spec(state_layer, bt, GROUP_W),
                  ws(CONV_W, GROUP_W), ws(1, GROUP_W), ws(GROUP_W, 2 * GROUP_W),
                  ws(1, 2 * GROUP_W), ws(1, GROUP_W), ws(1, GROUP_W)],
        out_specs=[_seq_spec(bt, tc, GROUP_W, 0), _state_spec(bt, CONV_W - 1, GROUP_W),
                   _state_spec(bt, GROUP_W)],
        out_shape=[jax.ShapeDtypeStruct((b, l, GROUP_W), F32),
                   jax.ShapeDtypeStruct((b, CONV_W - 1, GROUP_W), F32),
                   jax.ShapeDtypeStruct((b, GROUP_W), F32)],
        scratch_shapes=[pltpu.VMEM((bt, SUBLANES + tc, GROUP_W), F32), tiles, tiles,
                        pltpu.VMEM((bt, GROUP_W), F32)],
        compiler_params=_params(("parallel", "arbitrary")),
        name="rglru",
    )(z, z, buf0, h0, w["lru_conv_w"], w["lru_conv_b"], w["lru_wax"], w["lru_bax"],
      w["lru_lambda"], w["lru_norm"])


def _s5_kernel(ulo_ref, uhi_ref, re0_ref, im0_ref, ar_ref, ai_ref, bb_ref, cc_ref, d_ref, gw_ref,
               gb_ref, norm_ref, y_ref, re_ref, im_ref, rbuf, ibuf, ybuf, re_scr, im_scr, *,
               bt, tc):
    c = pl.program_id(1)
    n_tiles = S5_STATE // LANES
    groups = bt // SUBLANES

    @pl.when(c == 0)
    def _():
        re_scr[...] = re0_ref[...]
        im_scr[...] = im0_ref[...]

    def time_major(ref):
        return jnp.concatenate([ref[g * SUBLANES:(g + 1) * SUBLANES, t, :]
                                for g in range(groups) for t in range(tc)], axis=0)

    u = jnp.concatenate([time_major(ulo_ref), time_major(uhi_ref)], axis=-1)
    ub = u.astype(BF16)
    slab_shape = (SUBLANES, LANES)
    a_r = [jnp.broadcast_to(t, slab_shape) for t in _lane_tiles(ar_ref[...])]
    a_i = [jnp.broadcast_to(t, slab_shape) for t in _lane_tiles(ai_ref[...])]

    half = n_tiles // 2
    halves = [range(0, half), range(half, n_tiles)]
    cols = lambda ks, base: slice(base + ks[0] * LANES, base + (ks[-1] + 1) * LANES)

    def project_in(ks):
        bu_r = jnp.dot(ub, bb_ref[:, cols(ks, 0)], preferred_element_type=F32)
        bu_i = jnp.dot(ub, bb_ref[:, cols(ks, S5_STATE)], preferred_element_type=F32)
        for k, r_k, i_k in zip(ks, _lane_tiles(bu_r), _lane_tiles(bu_i)):
            rbuf[k] = r_k
            ibuf[k] = i_k

    def recur(ks):
        for g in range(groups):
            seqs = slice(g * SUBLANES, (g + 1) * SUBLANES)
            tile = lambda k: slice(k * LANES, (k + 1) * LANES)
            s_r = {k: re_scr[seqs, tile(k)] for k in ks}
            s_i = {k: im_scr[seqs, tile(k)] for k in ks}
            for t in range(tc):
                slab = pl.ds((g * tc + t) * SUBLANES, SUBLANES)
                for k in ks:
                    n_r = a_r[k] * s_r[k] - a_i[k] * s_i[k] + rbuf[k, slab, :]
                    n_i = a_r[k] * s_i[k] + a_i[k] * s_r[k] + ibuf[k, slab, :]
                    s_r[k], s_i[k] = n_r, n_i
                    rbuf[k, slab, :] = n_r
                    ibuf[k, slab, :] = n_i
            for k in ks:
                re_scr[seqs, tile(k)] = s_r[k]
                im_scr[seqs, tile(k)] = s_i[k]

    def project_out(ks):
        x_r = jnp.concatenate([rbuf[k] for k in ks], axis=-1).astype(BF16)
        x_i = jnp.concatenate([ibuf[k] for k in ks], axis=-1).astype(BF16)
        return (jnp.dot(x_r, cc_ref[cols(ks, 0), :], preferred_element_type=F32)
                + jnp.dot(x_i, cc_ref[cols(ks, S5_STATE), :], preferred_element_type=F32))

    project_in(halves[0])
    project_in(halves[1])
    recur(halves[0])
    y = project_out(halves[0])
    recur(halves[1])
    y = y + project_out(halves[1])
    re_ref[...] = re_scr[...]
    im_ref[...] = im_scr[...]

    y = jax.nn.gelu(y + d_ref[...] * u)
    y = y * jax.nn.sigmoid(_dot(y, gw_ref[...]) + gb_ref[...])
    for k, y_k in enumerate(_lane_tiles(_rms(y, norm_ref[...]))):
        ybuf[k] = y_k
    for g in range(groups):
        for j in range(SUBLANES):
            seq_rows = pl.ds(g * tc * SUBLANES + j, tc, stride=SUBLANES)
            y_ref[g * SUBLANES + j] = jnp.concatenate(
                [ybuf[k, seq_rows, :] for k in range(GROUP_W // LANES)], axis=-1)


def _s5_call(z, re0, im0, w, layer, state_layer, bt, tc):
    b, l, _ = z.shape
    rows = bt * tc
    ws = functools.partial(_layer_spec, layer)
    cb = (RW_COLS + 2 * GROUP_W) // LANES
    state = _state_spec(bt, S5_STATE)
    state_in = _stacked_state_spec(state_layer, bt, S5_STATE)
    tiles = pltpu.VMEM((S5_STATE // LANES, rows, LANES), F32)
    return pl.pallas_call(
        functools.partial(_s5_kernel, bt=bt, tc=tc),
        grid=(b // bt, l // tc),
        in_specs=[_seq_spec(bt, tc, LANES, cb), _seq_spec(bt, tc, LANES, cb + 1),
                  state_in, state_in,
                  ws(1, S5_STATE), ws(1, S5_STATE), ws(GROUP_W, 2 * S5_STATE),
                  ws(2 * S5_STATE, GROUP_W), ws(1, GROUP_W), ws(GROUP_W, GROUP_W),
                  ws(1, GROUP_W), ws(1, GROUP_W)],
        out_specs=[_seq_spec(bt, tc, GROUP_W, 0), state, state],
        out_shape=[jax.ShapeDtypeStruct((b, l, GROUP_W), F32),
                   jax.ShapeDtypeStruct((b, S5_STATE), F32),
                   jax.ShapeDtypeStruct((b, S5_STATE), F32)],
        scratch_shapes=[tiles, tiles, pltpu.VMEM((GROUP_W // LANES, rows, LANES), F32),
                        pltpu.VMEM((bt, S5_STATE), F32), pltpu.VMEM((bt, S5_STATE), F32)],
        compiler_params=_params(("parallel", "arbitrary")),
        name="s5",
    )(z, z, re0, im0, w["s5_ar"], w["s5_ai"], w["s5_bb"], w["s5_cc"], w["s5_d"], w["s5_glu_w"],
      w["s5_glu_b"], w["s5_norm"])


def _seq_cummax(x, bt, tc, buf):
    rows = bt * tc
    rin = lax.broadcasted_iota(jnp.int32, (rows, 1), 0) & (tc - 1)
    d = 1
    while d < tc:
        if d < SUBLANES:
            buf[SUBLANES:SUBLANES + rows, :] = x
            moved = buf[SUBLANES - d:SUBLANES - d + rows, :]
        else:
            moved = jnp.concatenate([x[:d], x[:rows - d]], axis=0)
        x = jnp.maximum(x, jnp.where(rin >= d, moved, -jnp.inf))
        d *= 2
    return x


def _mlstm_kernel(q_ref, k_ref, v_ref, og_ref, ig_ref, fg_ref, c0_ref, n0_ref, m0_ref, bi_ref,
                  bf_ref, lng_ref, hsum_ref, _c_all_ref, y_ref, c_ref, n_ref, m_ref,
                  c_scr, n_scr, m_scr, gbuf, *, bt, tc):
    c = pl.program_id(1)
    rows = bt * tc

    @pl.when(c == 0)
    def _():
        c_scr[...] = c0_ref[...]
        n_scr[...] = n0_ref[...]
        m_scr[...] = m0_ref[...]
        gbuf[0:SUBLANES, :] = jnp.zeros((SUBLANES, LANES), F32)

    flat = lambda ref: ref[...].reshape(rows, ref.shape[-1])
    log_i = flat(ig_ref) + bi_ref[...]
    log_f = jax.nn.log_sigmoid(flat(fg_ref) + bf_ref[...])
    _, incl = _seq_masks(bt, tc)
    bcum = _masked_sum(incl, log_f)
    m3 = m_scr[...]
    bcum3 = bcum.reshape(bt, tc, LANES)
    b_last3 = bcum3[:, tc - 1:tc, :]
    inter_all = (bcum3 + m3).reshape(rows, LANES)
    wk3 = b_last3 - bcum3 + log_i.reshape(bt, tc, LANES)
    m_new3 = jnp.maximum(b_last3 + m3, jnp.max(wk3, axis=1, keepdims=True))
    sc_all = jnp.exp(wk3 - m_new3).reshape(rows, LANES)
    decay3 = jnp.exp(b_last3 + m3 - m_new3)
    m_scr[...] = m_new3
    m_ref[...] = m_new3

    src = log_i - bcum
    mt_all = jnp.maximum(bcum + _seq_cummax(src, bt, tc, gbuf), inter_all)
    col_all = bcum - mt_all
    ei_all = jnp.exp(inter_all - mt_all)
    floor_all = jnp.exp(-mt_all)
    eye = (lax.broadcasted_iota(jnp.int32, (SUBLANES, LANES), 0)
           == lax.broadcasted_iota(jnp.int32, (SUBLANES, LANES), 1)).astype(F32)
    src_rows = _dot_nt(eye, src, HI)

    q, k, v = flat(q_ref), flat(k_ref) * (HEAD_DIM ** -0.5), flat(v_ref)
    ones = jnp.ones((rows, HEAD_DIM), F32)
    pad = jnp.zeros((HEAD_DIM - SUBLANES, HEAD_DIM), F32)
    heads = range(HEADS)
    lanes = [slice(h * HEAD_DIM, (h + 1) * HEAD_DIM) for h in heads]
    cols = [slice(h, h + 1) for h in heads]
    seqs = [slice(b * tc, (b + 1) * tc) for b in range(bt)]
    q_h, k_h, v_h = ([x[:, sl] for sl in lanes] for x in (q, k, v))
    pm = [jnp.where(incl, jnp.exp(col_all[:, cols[h]] + src_rows[cols[h], :]), 0.0)
          * _dot_nt(q_h[h].astype(BF16), k_h[h].astype(BF16)) for h in heads]
    intra = [_dot(pm[h], jnp.concatenate([v_h[h], ones], axis=-1)) for h in heads]
    c_old = [[c_scr[b, h] for b in range(bt)] for h in heads]
    n_old = [[jnp.broadcast_to(n_scr[b, cols[h], :], (SUBLANES, HEAD_DIM)) for b in range(bt)]
             for h in heads]
    from_state = [jnp.concatenate(
        [_dot_nt(q_h[h][rs].astype(BF16),
                 jnp.concatenate([c_old[h][b], n_old[h][b], pad], axis=0).astype(BF16))
         for b, rs in enumerate(seqs)], axis=0) for h in heads]
    nd = [intra[h] + ei_all[:, cols[h]] * from_state[h] for h in heads]
    o = jnp.concatenate(
        [nd[h][:, :HEAD_DIM] / jnp.maximum(jnp.abs(nd[h][:, HEAD_DIM:HEAD_DIM + 1]),
                                           floor_all[:, cols[h]]) for h in heads], axis=-1)
    for h in heads:
        sc = sc_all[:, cols[h]]
        scv = sc * v_h[h]
        for b, rs in enumerate(seqs):
            c_scr[b, h] = (decay3[b, :, cols[h]] * c_old[h][b]
                           + _dot_tn(scv[rs].astype(BF16), k_h[h][rs].astype(BF16)))
        sck = (sc * k_h[h]).reshape(bt, tc, HEAD_DIM)
        n_scr[:, cols[h], :] = (decay3[:, :, cols[h]] * n_scr[:, cols[h], :]
                                + jnp.sum(sck, axis=1, keepdims=True))
    hsum = hsum_ref[...]
    cen = o - _dot(o, hsum) * (1.0 / HEAD_DIM)
    var = _dot(cen * cen, hsum) * (1.0 / HEAD_DIM)
    y = cen * lax.rsqrt(var + EPS) * lng_ref[...] * jax.nn.sigmoid(flat(og_ref))
    y_ref[...] = y.reshape(bt, tc, GROUP_W)

    @pl.when(c == pl.num_programs(1) - 1)
    def _():
        c_ref[...] = c_scr[...]
        n_ref[...] = n_scr[...]


def _mlstm_call(z, c0, n0, m0, c_all, w, layer, state_layer, bt, tc):
    b, l, _ = z.shape
    ws = functools.partial(_layer_spec, layer)
    cb = (RW_COLS + 3 * GROUP_W) // GROUP_W
    mat = (HEADS, HEAD_DIM, HEAD_DIM)
    stacked = functools.partial(_stacked_state_spec, state_layer, bt)
    return pl.pallas_call(
        functools.partial(_mlstm_kernel, bt=bt, tc=tc),
        grid=(b // bt, l // tc),
        in_specs=[_seq_spec(bt, tc, GROUP_W, cb), _seq_spec(bt, tc, GROUP_W, cb + 1),
                  _seq_spec(bt, tc, GROUP_W, cb + 2), _seq_spec(bt, tc, GROUP_W, cb + 3),
                  _seq_spec(bt, tc, LANES, IGATE_COL_BLOCK),
                  _seq_spec(bt, tc, LANES, IGATE_COL_BLOCK + 1),
                  stacked(*mat), stacked(HEADS, HEAD_DIM), stacked(1, LANES),
                  ws(1, LANES), ws(1, LANES), ws(1, GROUP_W),
                  _const_spec((GROUP_W, GROUP_W)), ANY_SPEC],
        out_specs=[_seq_spec(bt, tc, GROUP_W, 0), _stacked_state_spec(layer, bt, *mat),
                   _state_spec(bt, HEADS, HEAD_DIM), _state_spec(bt, 1, LANES)],
        out_shape=[jax.ShapeDtypeStruct((b, l, GROUP_W), F32),
                   jax.ShapeDtypeStruct(c_all.shape, F32),
                   jax.ShapeDtypeStruct((b, HEADS, HEAD_DIM), F32),
                   jax.ShapeDtypeStruct((b, 1, LANES), F32)],
        input_output_aliases={13: 1},
        scratch_shapes=[pltpu.VMEM((bt,) + mat, F32),
                        pltpu.VMEM((bt, HEADS, HEAD_DIM), F32),
                        pltpu.VMEM((bt, 1, LANES), F32),
                        pltpu.VMEM((SUBLANES + bt * tc, LANES), F32)],
        compiler_params=_params(("parallel", "arbitrary")),
        name="mlstm",
    )(z, z, z, z, z, z, c0, n0, m0, w["ml_bi"], w["ml_bf"], w["ml_ln_g"], w["head_sum"], c_all)


def _block_diag(blocks):
    *lead, n, r, c = blocks.shape
    eye = jnp.eye(n, dtype=blocks.dtype)
    out = eye[:, None, :, None] * blocks[..., :, :, None, :]
    return out.reshape(*lead, n * r, n * c)


def _prep(w):
    depth = w["ffn1_pre"].shape[0]
    row = lambda t: t.reshape(depth, 1, -1).astype(F32)
    out = {}
    for f in ("ffn1", "ffn2"):
        out[f + "_pre"] = row(w[f + "_pre"])
        out[f + "_post"] = row(w[f + "_post"])
        out[f + "_wg"] = _cast_t_call(jnp.swapaxes(w[f + "_wg"], 1, 2), D_FF_PAD,
                                      _row_chunks(D_FF))
        out[f + "_wu"] = _cast_t_call(jnp.swapaxes(w[f + "_wu"], 1, 2), D_FF_PAD,
                                      _row_chunks(D_FF))
        out[f + "_wd"] = _cast_call(w[f + "_wd"], D_FF_PAD, D_MODEL)
    out["mix_pre"] = row(w["mix_pre"])
    out["mix_post"] = row(w["mix_post"])
    out["w_in"] = _cast_t_call(
        jnp.swapaxes(w["w_in"], 1, 2), IN_COLS_PAD,
        _row_chunks(MAIN_COLS) + ((MAIN_COLS, HEADS, MAIN_COLS),
                                  (MAIN_COLS + HEADS, HEADS, MAIN_COLS + LANES)))
    out["w_out"] = _cast_call(w["w_out"], D_MODEL, D_MODEL)

    for name in ("rw_mu", "rw_w0", "rw_a0", "rw_kk", "rw_ka", "rw_rk", "rw_ln_g", "rw_ln_b"):
        out[name] = row(w[name])
    for name in ("rw_w2", "rw_a2", "rw_g2"):
        out[name] = w[name].astype(BF16)
    out["head_sum"] = _block_diag(jnp.ones((HEADS, HEAD_DIM, HEAD_DIM), BF16))

    out["lru_conv_w"] = w["lru_conv_w"].astype(F32)
    out["lru_conv_b"] = row(w["lru_conv_b"])
    out["lru_wax"] = jnp.concatenate(
        [_block_diag(w["lru_wa"]), _block_diag(w["lru_wx"])], axis=-1).astype(BF16)
    out["lru_bax"] = jnp.concatenate([row(w["lru_ba"]), row(w["lru_bx"])], axis=-1)
    out["lru_lambda"] = row(w["lru_lambda"])
    out["lru_norm"] = row(w["lru_norm"])

    lr, li = w["s5_a_re"].astype(F32), w["s5_a_im"].astype(F32)
    dt = jnp.exp(w["s5_log_dt"].astype(F32))[..., None]
    mag = jnp.exp(lr * dt)
    ar, ai = mag * jnp.cos(li * dt), mag * jnp.sin(li * dt)
    den = lr * lr + li * li
    zr = ((ar - 1.0) * lr + ai * li) / den
    zi = (ai * lr - (ar - 1.0) * li) / den
    b_re, b_im = w["s5_b_re"].astype(F32), w["s5_b_im"].astype(F32)
    bbr = zr[..., None] * b_re - zi[..., None] * b_im
    bbi = zr[..., None] * b_im + zi[..., None] * b_re
    packed = lambda t: _block_diag(jnp.swapaxes(t, -1, -2))
    out["s5_ar"], out["s5_ai"] = row(ar), row(ai)
    out["s5_bb"] = jnp.concatenate([packed(bbr), packed(bbi)], axis=-1).astype(BF16)
    out["s5_cc"] = jnp.concatenate(
        [packed(w["s5_c_re"]), -packed(w["s5_c_im"])], axis=-2).astype(BF16)
    out["s5_d"] = row(w["s5_d"])
    out["s5_glu_w"] = w["s5_glu_w"].astype(BF16)
    out["s5_glu_b"] = row(w["s5_glu_b"])
    out["s5_norm"] = row(w["s5_norm"])

    lane_block = lambda t: jnp.pad(t.astype(F32), ((0, 0), (0, LANES - HEADS))).reshape(
        depth, 1, LANES)
    out["ml_bi"] = lane_block(w["ml_bi"])
    out["ml_bf"] = lane_block(w["ml_bf"])
    out["ml_ln_g"] = row(w["ml_ln_g"])
    return out


def _tiling(b, l, rows_cap, tc_cap, seq_multiple=1):
    tc = tc_cap
    while tc > SUBLANES and l % tc:
        tc //= 2
    if l % tc or b % seq_multiple:
        raise ValueError(f"unsupported batch {b} / sequence length {l}")
    bt = seq_multiple
    while bt * 2 * tc <= rows_cap and b % (bt * 2) == 0:
        bt *= 2
    return bt, tc


def _stack_states(states):
    shift, wkv, conv, h, re, im, c, n, m = (t.astype(F32) for t in states)
    layers, b = shift.shape[:2]
    return (shift.reshape(layers, b, 1, RW_COLS), wkv, conv, h,
            re.reshape(layers, b, S5_STATE), im.reshape(layers, b, S5_STATE), c, n,
            jnp.pad(m, ((0, 0), (0, 0), (0, LANES - HEADS))).reshape(layers, b, 1, LANES))


def _zero_states(b):
    mat = (1, b, HEADS, HEAD_DIM, HEAD_DIM)
    return _stack_states((
        jnp.zeros((1, b, RW_COLS), F32), jnp.zeros(mat, F32),
        jnp.zeros((1, b, CONV_W - 1, GROUP_W), F32), jnp.zeros((1, b, GROUP_W), F32),
        jnp.zeros((1, b, S5_GROUPS, S5_P), F32), jnp.zeros((1, b, S5_GROUPS, S5_P), F32),
        jnp.zeros(mat, F32), jnp.zeros((1, b, HEADS, HEAD_DIM), F32),
        jnp.zeros((1, b, HEADS), F32)))


def _layer(x, st, state_layer, mats, w, layer):
    b, l, d = x.shape
    n = b * l
    shift0, s0, buf0, h0, re0, im0, c0, n0, m0 = st
    s_all, c_all = mats
    x1, z = _ffn_in_call(x.reshape(n, d), w, layer)
    z = z.reshape(b, l, IN_COLS_PAD)

    y_rw, n_shift, s_all = _rwkv_call(z, shift0, s0, s_all, w, layer, state_layer,
                                      *_tiling(b, l, 256, 32))
    y_lru, n_buf, n_h = _lru_call(z, buf0, h0, w, layer, state_layer,
                                  *_tiling(b, l, 1024, 128, SUBLANES))
    y_s5, n_re, n_im = _s5_call(z, re0, im0, w, layer, state_layer,
                                *_tiling(b, l, 512, 64, SUBLANES))
    y_ml, c_all, n_n, n_m = _mlstm_call(z, c0, n0, m0, c_all, w, layer, state_layer,
                                        *_tiling(b, l, 256, 128))

    flat = lambda t: t.reshape(n, GROUP_W)
    x3 = _out_ffn_call(x1, flat(y_rw), flat(y_lru), flat(y_s5), flat(y_ml), w, layer)
    new = (n_shift.reshape(b, RW_COLS), n_buf, n_h,
           n_re.reshape(b, S5_GROUPS, S5_P), n_im.reshape(b, S5_GROUPS, S5_P), n_n,
           n_m.reshape(b, LANES)[:, :HEADS])
    return x3.reshape(b, l, d), new, (s_all, c_all)


def kernel(x_prompt, x_sample, state_rwkv_shift, state_rwkv_wkv, state_lru_conv, state_lru_h, state_s5_re, state_s5_im, state_mlstm_C, state_mlstm_n, state_mlstm_m, ffn1_pre, ffn1_wg, ffn1_wu, ffn1_wd, ffn1_post, mix_pre, w_in, w_out, mix_post, ffn2_pre, ffn2_wg, ffn2_wu, ffn2_wd, ffn2_post, rw_mu, rw_w0, rw_w2, rw_a0, rw_a2, rw_g2, rw_kk, rw_ka, rw_rk, rw_ln_g, rw_ln_b, lru_conv_w, lru_conv_b, lru_wa, lru_ba, lru_wx, lru_bx, lru_lambda, lru_norm, s5_a_re, s5_a_im, s5_log_dt, s5_b_re, s5_b_im, s5_c_re, s5_c_im, s5_d, s5_glu_w, s5_glu_b, s5_norm, ml_bi, ml_bf, ml_ln_g):
    w = _prep(dict(
        ffn1_pre=ffn1_pre, ffn1_wg=ffn1_wg, ffn1_wu=ffn1_wu, ffn1_wd=ffn1_wd, ffn1_post=ffn1_post,
        mix_pre=mix_pre, w_in=w_in, w_out=w_out, mix_post=mix_post,
        ffn2_pre=ffn2_pre, ffn2_wg=ffn2_wg, ffn2_wu=ffn2_wu, ffn2_wd=ffn2_wd, ffn2_post=ffn2_post,
        rw_mu=rw_mu, rw_w0=rw_w0, rw_w2=rw_w2, rw_a0=rw_a0, rw_a2=rw_a2, rw_g2=rw_g2,
        rw_kk=rw_kk, rw_ka=rw_ka, rw_rk=rw_rk, rw_ln_g=rw_ln_g, rw_ln_b=rw_ln_b,
        lru_conv_w=lru_conv_w, lru_conv_b=lru_conv_b, lru_wa=lru_wa, lru_ba=lru_ba,
        lru_wx=lru_wx, lru_bx=lru_bx, lru_lambda=lru_lambda, lru_norm=lru_norm,
        s5_a_re=s5_a_re, s5_a_im=s5_a_im, s5_log_dt=s5_log_dt, s5_b_re=s5_b_re, s5_b_im=s5_b_im,
        s5_c_re=s5_c_re, s5_c_im=s5_c_im, s5_d=s5_d, s5_glu_w=s5_glu_w, s5_glu_b=s5_glu_b,
        s5_norm=s5_norm, ml_bi=ml_bi, ml_bf=ml_bf, ml_ln_g=ml_ln_g))
    sample_state = _stack_states((state_rwkv_shift, state_rwkv_wkv, state_lru_conv, state_lru_h,
                                  state_s5_re, state_s5_im, state_mlstm_C, state_mlstm_n,
                                  state_mlstm_m))
    depth = ffn1_pre.shape[0]
    b_p = x_prompt.shape[0]
    prompt_state = _zero_states(b_p)
    mat_zeros = lambda b: jnp.zeros((depth, b, HEADS, HEAD_DIM, HEAD_DIM), F32)
    mats_p = (mat_zeros(b_p), mat_zeros(b_p))
    mats_s = (mat_zeros(x_sample.shape[0]), mat_zeros(x_sample.shape[0]))
    y_p, y_s = x_prompt.astype(F32), x_sample.astype(F32)
    outs_p, outs_s = [], []
    for layer in range(depth):
        y_p, st_p, mats_p = _layer(y_p, prompt_state, 0, mats_p, w, layer)
        y_s, st_s, mats_s = _layer(y_s, sample_state, layer, mats_s, w, layer)
        outs_p.append(st_p)
        outs_s.append(st_s)

    def gather(outs, mats):
        shift, conv, h, re, im, n, m = (jnp.stack([o[i] for o in outs]) for i in range(7))
        return (shift, mats[0], conv, h, re, im, mats[1], n, m)

    return (y_p, y_s, *gather(outs_p, mats_p), *gather(outs_s, mats_s))
```

```python
import functools

import jax
import jax.numpy as jnp
from jax import lax
from jax.experimental import pallas as pl
from jax.experimental.pallas import tpu as pltpu

F32 = jnp.float32
BF16 = jnp.bfloat16

LANES = 128
SUBLANES = 8
VMEM_LIMIT_BYTES = 56 * 1024 * 1024

D_MODEL = 1024
GROUP_W = D_MODEL // 4
HEAD_DIM = 64
HEADS = GROUP_W // HEAD_DIM
RW_W_RANK = 64
RW_A_RANK = 64
RW_G_RANK = 128
RW_COLS = 3 * GROUP_W + RW_W_RANK + RW_A_RANK + RW_G_RANK
RW_DECAY_SCALE = 0.606531
RW_GN_EPS = 64e-5
LRU_C = 8.0
CONV_W = 4
S5_GROUP = 16
S5_GROUPS = GROUP_W // S5_GROUP
S5_P = 64
S5_STATE = S5_GROUPS * S5_P
MAIN_COLS = RW_COLS + 7 * GROUP_W
IN_COLS = MAIN_COLS + 2 * HEADS
IN_COLS_PAD = MAIN_COLS + 2 * LANES
IGATE_COL_BLOCK = MAIN_COLS // LANES
D_FF = 2752
D_FF_PAD = -(-D_FF // LANES) * LANES
EPS = 1e-6
HI = lax.Precision.HIGHEST


def _dot(a, b):
    return jnp.dot(a.astype(BF16), b.astype(BF16), preferred_element_type=F32)


def _dot_nt(a, b, precision=None):
    return lax.dot_general(a, b, (((1,), (1,)), ((), ())), precision=precision,
                           preferred_element_type=F32)


def _dot_tn(a, b):
    return lax.dot_general(a, b, (((0,), (0,)), ((), ())), preferred_element_type=F32)


def _rms(x, g):
    return x * lax.rsqrt(jnp.mean(x * x, axis=-1, keepdims=True) + EPS) * g


def _seq_masks(bt, tc, reps=1):
    rows = bt * tc
    r = lax.broadcasted_iota(jnp.int32, (rows, reps * rows), 0)
    c = lax.broadcasted_iota(jnp.int32, (rows, reps * rows), 1) & (rows - 1)
    if bt == 1:
        return r > c, r >= c
    start = r - (r & (tc - 1))
    return (c < r) & (c >= start), (c <= r) & (c >= start)


def _later_mask(bt, tc, reps=1):
    rows = bt * tc
    r = lax.broadcasted_iota(jnp.int32, (reps * rows, rows), 0) & (rows - 1)
    c = lax.broadcasted_iota(jnp.int32, (reps * rows, rows), 1)
    if bt == 1:
        return r < c
    return (r < c) & (r >= c - (c & (tc - 1)))


def _masked_sum(mask, x):
    m = mask.astype(BF16)
    hi = x.astype(BF16)
    rest = x - hi.astype(F32)
    mid = rest.astype(BF16)
    lo = (rest - mid.astype(F32)).astype(BF16)
    dot = lambda piece: jnp.dot(m, piece, preferred_element_type=F32)
    return dot(hi) + dot(mid) + dot(lo)


def _const_spec(shape):
    nd = len(shape)
    return pl.BlockSpec(shape, lambda *_: (0,) * nd, pipeline_mode=pl.Buffered(1))


def _layer_spec(layer, *shape):
    zeros = (0,) * len(shape)
    return pl.BlockSpec((None,) + shape, lambda *_: (layer,) + zeros,
                        pipeline_mode=pl.Buffered(1))


def _params(sem):
    return pltpu.CompilerParams(dimension_semantics=sem, vmem_limit_bytes=VMEM_LIMIT_BYTES)


def _cast_kernel(x_ref, o_ref):
    rows, cols = x_ref.shape
    o_ref[...] = jnp.zeros(o_ref.shape, BF16)
    o_ref[0:rows, 0:cols] = x_ref[...].astype(BF16)


def _cast_call(x, out_rows, out_cols):
    depth, rows, cols = x.shape
    return pl.pallas_call(
        _cast_kernel,
        grid=(depth,),
        in_specs=[pl.BlockSpec((None, rows, cols), lambda i: (i, 0, 0))],
        out_specs=pl.BlockSpec((None, out_rows, out_cols), lambda i: (i, 0, 0)),
        out_shape=jax.ShapeDtypeStruct((depth, out_rows, out_cols), BF16),
        compiler_params=_params(("parallel",)),
        name="cast_weights",
    )(x)


def _cast_t_kernel(x_ref, o_ref, *, segments):
    cols = x_ref.shape[1]
    for src, n, dst in segments:
        chunk = x_ref[src:src + n, :]
        if n < LANES:
            chunk = jnp.concatenate([chunk, jnp.zeros((LANES - n, cols), chunk.dtype)], axis=0)
        o_ref[:, dst:dst + LANES] = chunk.T.astype(BF16)


def _row_chunks(n):
    return tuple((i, min(LANES, n - i), i) for i in range(0, n, LANES))


def _cast_t_call(x_t, out_cols, segments):
    depth, rows, cols = x_t.shape
    assert len(segments) * LANES == out_cols
    return pl.pallas_call(
        functools.partial(_cast_t_kernel, segments=segments),
        grid=(depth,),
        in_specs=[pl.BlockSpec((None, rows, cols), lambda i: (i, 0, 0))],
        out_specs=pl.BlockSpec((None, cols, out_cols), lambda i: (i, 0, 0)),
        out_shape=jax.ShapeDtypeStruct((depth, cols, out_cols), BF16),
        compiler_params=_params(("parallel",)),
        name="cast_weights_t",
    )(x_t)


FFN_ROW_SPLIT = 2


def _row_parts(n):
    step = n // FFN_ROW_SPLIT
    return [slice(i * step, (i + 1) * step) for i in range(FFN_ROW_SPLIT)]


def _swiglu_half_step(xs, pre, wg, wu, wd, post):
    hs = [_rms(x, pre).astype(BF16) for x in xs]
    gs = [jnp.dot(h, wg, preferred_element_type=F32) for h in hs]
    us = [jnp.dot(h, wu, preferred_element_type=F32) for h in hs]
    acts = [(jax.nn.silu(g) * u).astype(BF16) for g, u in zip(gs, us)]
    fs = [jnp.dot(act, wd, preferred_element_type=F32) for act in acts]
    return [x + 0.5 * _rms(f, post) for x, f in zip(xs, fs)]


def _ffn_in_kernel(x_ref, pre_ref, wg_ref, wu_ref, wd_ref, post_ref, mpre_ref, win_ref,
                   xo_ref, z_ref):
    parts = _row_parts(x_ref.shape[0])
    x1 = _swiglu_half_step([x_ref[rs, :] for rs in parts], pre_ref[...], wg_ref[...],
                           wu_ref[...], wd_ref[...], post_ref[...])
    hs = [_rms(x, mpre_ref[...]).astype(BF16) for x in x1]
    zs = [jnp.dot(h, win_ref[...], preferred_element_type=F32) for h in hs]
    for rs, x, z in zip(parts, x1, zs):
        xo_ref[rs, :] = x
        z_ref[rs, :] = z


def _out_ffn_kernel(x_ref, yrw_ref, ylru_ref, ys5_ref, yml_ref, wout_ref, mpost_ref,
                    pre_ref, wg_ref, wu_ref, wd_ref, post_ref, xo_ref):
    parts = _row_parts(x_ref.shape[0])
    cats = [jnp.concatenate([yrw_ref[rs, :], ylru_ref[rs, :], ys5_ref[rs, :], yml_ref[rs, :]],
                            axis=-1).astype(BF16) for rs in parts]
    ms = [jnp.dot(cat, wout_ref[...], preferred_element_type=F32) for cat in cats]
    x2 = [x_ref[rs, :] + _rms(m, mpost_ref[...]) for rs, m in zip(parts, ms)]
    x3 = _swiglu_half_step(x2, pre_ref[...], wg_ref[...], wu_ref[...], wd_ref[...],
                           post_ref[...])
    for rs, x in zip(parts, x3):
        xo_ref[rs, :] = x


def _token_tile(n):
    for tm in (512, 256, 128, 64, 32, 16, 8):
        if n % tm == 0:
            return tm
    raise ValueError(f"token count {n} is not a multiple of {SUBLANES}")


def _ffn_specs(ws):
    return [ws(1, D_MODEL), ws(D_MODEL, D_FF_PAD), ws(D_MODEL, D_FF_PAD), ws(D_FF_PAD, D_MODEL),
            ws(1, D_MODEL)]


def _ffn_in_call(x, w, layer):
    n = x.shape[0]
    tm = _token_tile(n)
    row = lambda width: pl.BlockSpec((tm, width), lambda i: (i, 0))
    ws = functools.partial(_layer_spec, layer)
    return pl.pallas_call(
        _ffn_in_kernel,
        grid=(n // tm,),
        in_specs=[row(D_MODEL)] + _ffn_specs(ws) + [ws(1, D_MODEL), ws(D_MODEL, IN_COLS_PAD)],
        out_specs=[row(D_MODEL), row(IN_COLS_PAD)],
        out_shape=[jax.ShapeDtypeStruct((n, D_MODEL), F32),
                   jax.ShapeDtypeStruct((n, IN_COLS_PAD), F32)],
        compiler_params=_params(("parallel",)),
        name="ffn1_inproj",
    )(x, w["ffn1_pre"], w["ffn1_wg"], w["ffn1_wu"], w["ffn1_wd"], w["ffn1_post"], w["mix_pre"],
      w["w_in"])


def _out_ffn_call(x, yrw, ylru, ys5, yml, w, layer):
    n = x.shape[0]
    tm = _token_tile(n)
    row = lambda width: pl.BlockSpec((tm, width), lambda i: (i, 0))
    ws = functools.partial(_layer_spec, layer)
    return pl.pallas_call(
        _out_ffn_kernel,
        grid=(n // tm,),
        in_specs=[row(D_MODEL), row(GROUP_W), row(GROUP_W), row(GROUP_W), row(GROUP_W),
                  ws(D_MODEL, D_MODEL), ws(1, D_MODEL)] + _ffn_specs(ws),
        out_specs=row(D_MODEL),
        out_shape=jax.ShapeDtypeStruct((n, D_MODEL), F32),
        compiler_params=_params(("parallel",)),
        name="outproj_ffn2",
    )(x, yrw, ylru, ys5, yml, w["w_out"], w["mix_post"], w["ffn2_pre"], w["ffn2_wg"],
      w["ffn2_wu"], w["ffn2_wd"], w["ffn2_post"])


def _seq_spec(bt, tc, w, col_block):
    return pl.BlockSpec((bt, tc, w), lambda i, c: (i, c, col_block))


def _state_spec(bt, *dims):
    zeros = (0,) * len(dims)
    return pl.BlockSpec((bt,) + dims, lambda i, c: (i,) + zeros)


def _stacked_state_spec(layer, bt, *dims):
    zeros = (0,) * len(dims)
    return pl.BlockSpec((None, bt) + dims, lambda i, c: (layer, i) + zeros)


ANY_SPEC = pl.BlockSpec(memory_space=pl.ANY)


def _rwkv_kernel(p_ref, shift0_ref, s0_ref, mu_ref, w0_ref, w2_ref, a0_ref, a2_ref, g2_ref,
                 kkw_ref, ka_ref, rk_ref, lng_ref, lnb_ref, hsum_ref, _s_all_ref,
                 y_ref, shift_ref, s_ref, pbuf, s_scr, *, bt, tc):
    c = pl.program_id(1)
    rows = bt * tc
    lo = SUBLANES - 1

    @pl.when(c == 0)
    def _():
        pbuf[:, lo:SUBLANES, :] = shift0_ref[...]
        s_scr[...] = s0_ref[...]

    p3 = p_ref[...]
    pbuf[:, SUBLANES:SUBLANES + tc, :] = p3
    prev = pbuf[:, lo:lo + tc, :].reshape(rows, RW_COLS)
    last = p3[:, tc - 1:tc, :]
    pbuf[:, lo:SUBLANES, :] = last
    shift_ref[...] = last
    p = p3.reshape(rows, RW_COLS)

    xm = p + (prev - p) * mu_ref[...]
    g = GROUP_W
    r, k, v = xm[:, 0:g], xm[:, g:2 * g], xm[:, 2 * g:3 * g]
    lw = xm[:, 3 * g:3 * g + RW_W_RANK]
    la = xm[:, 3 * g + RW_W_RANK:3 * g + RW_W_RANK + RW_A_RANK]
    lg = xm[:, 3 * g + RW_W_RANK + RW_A_RANK:]
    logw = -RW_DECAY_SCALE * jax.nn.sigmoid(w0_ref[...] + _dot(jnp.tanh(lw), w2_ref[...]))
    a = jax.nn.sigmoid(a0_ref[...] + _dot(la, a2_ref[...]))
    gate = _dot(jax.nn.sigmoid(lg), g2_ref[...])

    hsum = hsum_ref[...]
    kk = k * kkw_ref[...]
    kk = kk * lax.rsqrt(jnp.maximum(_dot(kk * kk, hsum), 1e-12))
    k2 = k * (1.0 + (a - 1.0) * ka_ref[...])

    _, incl = _seq_masks(bt, tc)
    _, incl2 = _seq_masks(bt, tc, reps=2)
    later2 = _later_mask(bt, tc, reps=2)
    cum = _masked_sum(incl, logw)
    cum3 = cum.reshape(bt, tc, g)
    total3 = cum3[:, tc - 1:tc, :]
    rest = jnp.exp(total3 - cum3).reshape(rows, g)
    wtot3 = jnp.exp(total3)
    inv = jnp.exp(-cum)
    kka = kk * a
    at = -kk * jnp.exp(cum - logw)
    rt = r * jnp.exp(cum)
    bk_t = kka * inv, k2 * inv
    bk_w = kka * rest, k2 * rest

    n_double = max(1, (tc - 1).bit_length())
    heads = range(HEADS)
    lanes = [slice(h * HEAD_DIM, (h + 1) * HEAD_DIM) for h in heads]
    seqs = [slice(b * tc, (b + 1) * tc) for b in range(bt)]
    a_h, r_h, v_h = ([x[:, sl] for sl in lanes] for x in (at, rt, v))
    bk_h = [jnp.concatenate([bk_t[0][:, sl], bk_t[1][:, sl]], axis=0).astype(BF16)
            for sl in lanes]
    m_at = [jnp.where(later2, _dot_nt(bk_h[h], a_h[h].astype(BF16)), 0.0) for h in heads]
    m_r = [jnp.where(incl2, _dot_nt(r_h[h].astype(BF16), bk_h[h]), 0.0) for h in heads]
    s_old = [[s_scr[b, h] for b in range(bt)] for h in heads]
    from_state = [[_dot_nt(jnp.concatenate([a_h[h][rs], r_h[h][rs]], axis=0).astype(BF16),
                           s_old[h][b].astype(BF16)) for b, rs in enumerate(seqs)]
                  for h in heads]
    u0_t = jnp.concatenate([jnp.concatenate([fs[:tc] for fs in from_state[h]], axis=0)
                            for h in heads], axis=-1).T
    v_t = v.T
    u_t = [u0_t[sl] + _dot(v_t[sl], m_at[h][rows:]) for h, sl in enumerate(lanes)]
    pw = [m_at[h][:rows] for h in heads]
    for i in range(n_double):
        u_t = [u_t[h] + _dot(u_t[h], pw[h]) for h in heads]
        if i + 1 < n_double:
            pw = [_dot(pw[h], pw[h]) for h in heads]
    u_all = jnp.concatenate(u_t, axis=0).T
    uv = [jnp.concatenate([u_all[:, sl], v_h[h]], axis=0) for h, sl in enumerate(lanes)]
    o = jnp.concatenate(
        [jnp.concatenate([fs[tc:] for fs in from_state[h]], axis=0) + _dot(m_r[h], uv[h])
         for h in heads], axis=-1)
    for h, sl in enumerate(lanes):
        bkw = jnp.concatenate([bk_w[0][:, sl], bk_w[1][:, sl]], axis=0)
        for b, rs in enumerate(seqs):
            rs2 = slice(rows + rs.start, rows + rs.stop)
            uv_b = jnp.concatenate([uv[h][rs], uv[h][rs2]], axis=0)
            bkw_b = jnp.concatenate([bkw[rs], bkw[rs2]], axis=0)
            s_scr[b, h] = (s_old[h][b] * wtot3[b, :, sl]
                           + _dot_tn(uv_b.astype(BF16), bkw_b.astype(BF16)))

    mean = _dot(o, hsum) * (1.0 / HEAD_DIM)
    cen = o - mean
    var = _dot(cen * cen, hsum) * (1.0 / HEAD_DIM)
    y = cen * lax.rsqrt(var + RW_GN_EPS) * lng_ref[...] + lnb_ref[...]
    bonus = _dot(r * k2 * rk_ref[...], hsum) * v
    y_ref[...] = ((y + bonus) * gate).reshape(bt, tc, g)

    @pl.when(c == pl.num_programs(1) - 1)
    def _():
        s_ref[...] = s_scr[...]


def _rwkv_call(z, shift0, s0, s_all, w, layer, state_layer, bt, tc):
    b, l, _ = z.shape
    ws = functools.partial(_layer_spec, layer)
    mat = (HEADS, HEAD_DIM, HEAD_DIM)
    return pl.pallas_call(
        functools.partial(_rwkv_kernel, bt=bt, tc=tc),
        grid=(b // bt, l // tc),
        in_specs=[_seq_spec(bt, tc, RW_COLS, 0),
                  _stacked_state_spec(state_layer, bt, 1, RW_COLS),
                  _stacked_state_spec(state_layer, bt, *mat),
                  ws(1, RW_COLS), ws(1, GROUP_W), ws(RW_W_RANK, GROUP_W), ws(1, GROUP_W),
                  ws(RW_A_RANK, GROUP_W), ws(RW_G_RANK, GROUP_W),
                  ws(1, GROUP_W), ws(1, GROUP_W), ws(1, GROUP_W), ws(1, GROUP_W),
                  ws(1, GROUP_W), _const_spec((GROUP_W, GROUP_W)), ANY_SPEC],
        out_specs=[_seq_spec(bt, tc, GROUP_W, 0), _state_spec(bt, 1, RW_COLS),
                   _stacked_state_spec(layer, bt, *mat)],
        out_shape=[jax.ShapeDtypeStruct((b, l, GROUP_W), F32),
                   jax.ShapeDtypeStruct((b, 1, RW_COLS), F32),
                   jax.ShapeDtypeStruct(s_all.shape, F32)],
        input_output_aliases={15: 2},
        scratch_shapes=[pltpu.VMEM((bt, SUBLANES + tc, RW_COLS), F32),
                        pltpu.VMEM((bt,) + mat, F32)],
        compiler_params=_params(("parallel", "arbitrary")),
        name="rwkv7",
    )(z, shift0, s0, w["rw_mu"], w["rw_w0"], w["rw_w2"], w["rw_a0"], w["rw_a2"], w["rw_g2"],
      w["rw_kk"], w["rw_ka"], w["rw_rk"], w["rw_ln_g"], w["rw_ln_b"], w["head_sum"], s_all)


def _lane_tiles(x):
    return [x[:, k * LANES:(k + 1) * LANES] for k in range(x.shape[-1] // LANES)]


def _time_major(ref, groups, tc):
    return jnp.concatenate([ref[g * SUBLANES:(g + 1) * SUBLANES, t, :]
                            for g in range(groups) for t in range(tc)], axis=0)


def _seq_rows(g, j, tc):
    return pl.ds(g * tc * SUBLANES + j, tc, stride=SUBLANES)


def _lru_kernel(xlo_ref, xhi_ref, gate_ref, buf0lo_ref, buf0hi_ref, h0_ref, cw_ref, cb_ref,
                wax_ref, bax_ref, lam_ref, norm_ref, y_ref, buf_ref, h_ref,
                cbuf, abuf, bbuf, h_scr, *, bt, tc):
    c = pl.program_id(1)
    rows = bt * tc
    groups = bt // SUBLANES
    n_tiles = GROUP_W // LANES
    hist = (CONV_W - 1) * SUBLANES
    span = hist + tc * SUBLANES
    tile = lambda k: slice(k * LANES, (k + 1) * LANES)

    @pl.when(c == 0)
    def _():
        h_scr[...] = h0_ref[...]
        for k, ref in enumerate((buf0lo_ref, buf0hi_ref)):
            for g in range(groups):
                for j in range(CONV_W - 1):
                    cbuf[k, pl.ds(g * span + j * SUBLANES, SUBLANES), :] = (
                        ref[g * SUBLANES:(g + 1) * SUBLANES, j, :])

    cw, cb = cw_ref[...], cb_ref[...]
    xc_tiles = []
    for k, ref in enumerate((xlo_ref, xhi_ref)):
        x_tm = _time_major(ref, groups, tc)
        per_group = []
        for g in range(groups):
            base = g * span
            cbuf[k, pl.ds(base + hist, tc * SUBLANES), :] = x_tm[g * tc * SUBLANES:
                                                                (g + 1) * tc * SUBLANES]
            acc = cb[:, tile(k)]
            for j in range(CONV_W):
                acc = acc + (cbuf[k, pl.ds(base + j * SUBLANES, tc * SUBLANES), :]
                             * cw[j:j + 1, tile(k)])
            per_group.append(acc)
            cbuf[k, pl.ds(base, hist), :] = cbuf[k, pl.ds(base + tc * SUBLANES, hist), :]
        xc_tiles.append(jnp.concatenate(per_group, axis=0))
    xc = jnp.concatenate(xc_tiles, axis=-1)

    gates = jax.nn.sigmoid(_dot(xc, wax_ref[...]) + bax_ref[...])
    gate_r, gate_i = gates[:, :GROUP_W], gates[:, GROUP_W:]
    log_a = -LRU_C * gate_r * jax.nn.softplus(-lam_ref[...])
    a = jnp.exp(log_a)
    bb = jnp.sqrt(-jnp.tanh(log_a) * (a * a + 1.0)) * (gate_i * xc)
    for k, (a_k, b_k) in enumerate(zip(_lane_tiles(a), _lane_tiles(bb))):
        abuf[k] = a_k
        bbuf[k] = b_k

    for g in range(groups):
        seqs = slice(g * SUBLANES, (g + 1) * SUBLANES)
        h = _lane_tiles(h_scr[seqs, :])
        for t in range(tc):
            slab = pl.ds((g * tc + t) * SUBLANES, SUBLANES)
            for k in range(n_tiles):
                h[k] = abuf[k, slab, :] * h[k] + bbuf[k, slab, :]
                bbuf[k, slab, :] = h[k]
        h_scr[seqs, :] = jnp.concatenate(h, axis=-1)
    h_ref[...] = h_scr[...]

    h_all = jnp.concatenate(
        [jnp.concatenate([bbuf[k, _seq_rows(g, j, tc), :] for k in range(n_tiles)], axis=-1)
         for g in range(groups) for j in range(SUBLANES)], axis=0)
    gate = gate_ref[...].reshape(rows, GROUP_W)
    y_ref[...] = _rms(h_all * jax.nn.gelu(gate), norm_ref[...]).reshape(bt, tc, GROUP_W)

    @pl.when(c == pl.num_programs(1) - 1)
    def _():
        for g in range(groups):
            for j in range(SUBLANES):
                buf_ref[g * SUBLANES + j] = jnp.concatenate(
                    [cbuf[k, pl.ds(g * span + j, CONV_W - 1, stride=SUBLANES), :]
                     for k in range(n_tiles)], axis=-1)


def _lru_call(z, buf0, h0, w, layer, state_layer, bt, tc):
    b, l, _ = z.shape
    rows = bt * tc
    groups = bt // SUBLANES
    ws = functools.partial(_layer_spec, layer)
    cb = RW_COLS // LANES
    n_tiles = GROUP_W // LANES
    tiles = pltpu.VMEM((n_tiles, rows, LANES), F32)
    hist_spec = lambda k: pl.BlockSpec((None, bt, CONV_W - 1, LANES),
                                       lambda i, c: (state_layer, i, 0, k))
    return pl.pallas_call(
        functools.partial(_lru_kernel, bt=bt, tc=tc),
        grid=(b // bt, l // tc),
        in_specs=[_seq_spec(bt, tc, LANES, cb), _seq_spec(bt, tc, LANES, cb + 1),
                  _seq_spec(bt, tc, GROUP_W, cb // n_tiles + 1), hist_spec(0), hist_spec(1),
                  _stacked_state_spec(state_layer, bt, GROUP_W),
                  ws(CONV_W, GROUP_W), ws(1, GROUP_W), ws(GROUP_W, 2 * GROUP_W),
                  ws(1, 2 * GROUP_W), ws(1, GROUP_W), ws(1, GROUP_W)],
        out_specs=[_seq_spec(bt, tc, GROUP_W, 0), _state_spec(bt, CONV_W - 1, GROUP_W),
                   _state_spec(bt, GROUP_W)],
        out_shape=[jax.ShapeDtypeStruct((b, l, GROUP_W), F32),
                   jax.ShapeDtypeStruct((b, CONV_W - 1, GROUP_W), F32),
                   jax.ShapeDtypeStruct((b, GROUP_W), F32)],
        scratch_shapes=[pltpu.VMEM((n_tiles, groups * (CONV_W - 1 + tc) * SUBLANES, LANES), F32),
                        tiles, tiles, pltpu.VMEM((bt, GROUP_W), F32)],
        compiler_params=_params(("parallel", "arbitrary")),
        name="rglru",
    )(z, z, z, buf0, buf0, h0, w["lru_conv_w"], w["lru_conv_b"], w["lru_wax"], w["lru_bax"],
      w["lru_lambda"], w["lru_norm"])


def _s5_kernel(ulo_ref, uhi_ref, re0_ref, im0_ref, ar_ref, ai_ref, bb_ref, cc_ref, d_ref, gw_ref,
               gb_ref, norm_ref, y_ref, re_ref, im_ref, rbuf, ibuf, ybuf, re_scr, im_scr, *,
               bt, tc):
    c = pl.program_id(1)
    n_tiles = S5_STATE // LANES
    groups = bt // SUBLANES

    @pl.when(c == 0)
    def _():
        re_scr[...] = re0_ref[...]
        im_scr[...] = im0_ref[...]

    u = jnp.concatenate([_time_major(ulo_ref, groups, tc), _time_major(uhi_ref, groups, tc)],
                        axis=-1)
    ub = u.astype(BF16)
    slab_shape = (SUBLANES, LANES)
    a_r = [jnp.broadcast_to(t, slab_shape) for t in _lane_tiles(ar_ref[...])]
    a_i = [jnp.broadcast_to(t, slab_shape) for t in _lane_tiles(ai_ref[...])]

    half = n_tiles // 2
    halves = [range(0, half), range(half, n_tiles)]
    cols = lambda ks, base: slice(base + ks[0] * LANES, base + (ks[-1] + 1) * LANES)

    def project_in(ks):
        bu_r = jnp.dot(ub, bb_ref[:, cols(ks, 0)], preferred_element_type=F32)
        bu_i = jnp.dot(ub, bb_ref[:, cols(ks, S5_STATE)], preferred_element_type=F32)
        for k, r_k, i_k in zip(ks, _lane_tiles(bu_r), _lane_tiles(bu_i)):
            rbuf[k] = r_k
            ibuf[k] = i_k

    def recur(ks):
        for g in range(groups):
            seqs = slice(g * SUBLANES, (g + 1) * SUBLANES)
            tile = lambda k: slice(k * LANES, (k + 1) * LANES)
            s_r = {k: re_scr[seqs, tile(k)] for k in ks}
            s_i = {k: im_scr[seqs, tile(k)] for k in ks}
            for t in range(tc):
                slab = pl.ds((g * tc + t) * SUBLANES, SUBLANES)
                for k in ks:
                    n_r = a_r[k] * s_r[k] - a_i[k] * s_i[k] + rbuf[k, slab, :]
                    n_i = a_r[k] * s_i[k] + a_i[k] * s_r[k] + ibuf[k, slab, :]
                    s_r[k], s_i[k] = n_r, n_i
                    rbuf[k, slab, :] = n_r
                    ibuf[k, slab, :] = n_i
            for k in ks:
                re_scr[seqs, tile(k)] = s_r[k]
                im_scr[seqs, tile(k)] = s_i[k]

    def project_out(ks):
        x_r = jnp.concatenate([rbuf[k] for k in ks], axis=-1).astype(BF16)
        x_i = jnp.concatenate([ibuf[k] for k in ks], axis=-1).astype(BF16)
        return (jnp.dot(x_r, cc_ref[cols(ks, 0), :], preferred_element_type=F32)
                + jnp.dot(x_i, cc_ref[cols(ks, S5_STATE), :], preferred_element_type=F32))

    project_in(halves[0])
    project_in(halves[1])
    recur(halves[0])
    y = project_out(halves[0])
    recur(halves[1])
    y = y + project_out(halves[1])
    re_ref[...] = re_scr[...]
    im_ref[...] = im_scr[...]

    y = jax.nn.gelu(y + d_ref[...] * u)
    y = y * jax.nn.sigmoid(_dot(y, gw_ref[...]) + gb_ref[...])
    for k, y_k in enumerate(_lane_tiles(_rms(y, norm_ref[...]))):
        ybuf[k] = y_k
    for g in range(groups):
        for j in range(SUBLANES):
            y_ref[g * SUBLANES + j] = jnp.concatenate(
                [ybuf[k, _seq_rows(g, j, tc), :] for k in range(GROUP_W // LANES)], axis=-1)


def _s5_call(z, re0, im0, w, layer, state_layer, bt, tc):
    b, l, _ = z.shape
    rows = bt * tc
    ws = functools.partial(_layer_spec, layer)
    cb = (RW_COLS + 2 * GROUP_W) // LANES
    state = _state_spec(bt, S5_STATE)
    state_in = _stacked_state_spec(state_layer, bt, S5_STATE)
    tiles = pltpu.VMEM((S5_STATE // LANES, rows, LANES), F32)
    return pl.pallas_call(
        functools.partial(_s5_kernel, bt=bt, tc=tc),
        grid=(b // bt, l // tc),
        in_specs=[_seq_spec(bt, tc, LANES, cb), _seq_spec(bt, tc, LANES, cb + 1),
                  state_in, state_in,
                  ws(1, S5_STATE), ws(1, S5_STATE), ws(GROUP_W, 2 * S5_STATE),
                  ws(2 * S5_STATE, GROUP_W), ws(1, GROUP_W), ws(GROUP_W, GROUP_W),
                  ws(1, GROUP_W), ws(1, GROUP_W)],
        out_specs=[_seq_spec(bt, tc, GROUP_W, 0), state, state],
        out_shape=[jax.ShapeDtypeStruct((b, l, GROUP_W), F32),
                   jax.ShapeDtypeStruct((b, S5_STATE), F32),
                   jax.ShapeDtypeStruct((b, S5_STATE), F32)],
        scratch_shapes=[tiles, tiles, pltpu.VMEM((GROUP_W // LANES, rows, LANES), F32),
                        pltpu.VMEM((bt, S5_STATE), F32), pltpu.VMEM((bt, S5_STATE), F32)],
        compiler_params=_params(("parallel", "arbitrary")),
        name="s5",
    )(z, z, re0, im0, w["s5_ar"], w["s5_ai"], w["s5_bb"], w["s5_cc"], w["s5_d"], w["s5_glu_w"],
      w["s5_glu_b"], w["s5_norm"])


def _seq_cummax(x, bt, tc, buf):
    rows = bt * tc
    rin = lax.broadcasted_iota(jnp.int32, (rows, 1), 0) & (tc - 1)
    d = 1
    while d < tc:
        if d < SUBLANES:
            buf[SUBLANES:SUBLANES + rows, :] = x
            moved = buf[SUBLANES - d:SUBLANES - d + rows, :]
        else:
            moved = jnp.concatenate([x[:d], x[:rows - d]], axis=0)
        x = jnp.maximum(x, jnp.where(rin >= d, moved, -jnp.inf))
        d *= 2
    return x


def _mlstm_kernel(q_ref, k_ref, v_ref, og_ref, ig_ref, fg_ref, c0_ref, n0_ref, m0_ref, bi_ref,
                  bf_ref, lng_ref, hsum_ref, _c_all_ref, y_ref, c_ref, n_ref, m_ref,
                  c_scr, n_scr, m_scr, gbuf, *, bt, tc):
    c = pl.program_id(1)
    rows = bt * tc

    @pl.when(c == 0)
    def _():
        c_scr[...] = c0_ref[...]
        n_scr[...] = n0_ref[...]
        m_scr[...] = m0_ref[...]
        gbuf[0:SUBLANES, :] = jnp.zeros((SUBLANES, LANES), F32)

    flat = lambda ref: ref[...].reshape(rows, ref.shape[-1])
    log_i = flat(ig_ref) + bi_ref[...]
    log_f = jax.nn.log_sigmoid(flat(fg_ref) + bf_ref[...])
    _, incl = _seq_masks(bt, tc)
    bcum = _masked_sum(incl, log_f)
    m3 = m_scr[...]
    bcum3 = bcum.reshape(bt, tc, LANES)
    b_last3 = bcum3[:, tc - 1:tc, :]
    inter_all = (bcum3 + m3).reshape(rows, LANES)
    wk3 = b_last3 - bcum3 + log_i.reshape(bt, tc, LANES)
    m_new3 = jnp.maximum(b_last3 + m3, jnp.max(wk3, axis=1, keepdims=True))
    sc_all = jnp.exp(wk3 - m_new3).reshape(rows, LANES)
    decay3 = jnp.exp(b_last3 + m3 - m_new3)
    m_scr[...] = m_new3
    m_ref[...] = m_new3

    src = log_i - bcum
    mt_all = jnp.maximum(bcum + _seq_cummax(src, bt, tc, gbuf), inter_all)
    col_all = bcum - mt_all
    ei_all = jnp.exp(inter_all - mt_all)
    floor_all = jnp.exp(-mt_all)
    eye = (lax.broadcasted_iota(jnp.int32, (SUBLANES, LANES), 0)
           == lax.broadcasted_iota(jnp.int32, (SUBLANES, LANES), 1)).astype(F32)
    src_rows = _dot_nt(eye, src, HI)

    q, k, v = flat(q_ref), flat(k_ref) * (HEAD_DIM ** -0.5), flat(v_ref)
    ones = jnp.ones((rows, HEAD_DIM), F32)
    pad = jnp.zeros((HEAD_DIM - SUBLANES, HEAD_DIM), F32)
    heads = range(HEADS)
    lanes = [slice(h * HEAD_DIM, (h + 1) * HEAD_DIM) for h in heads]
    cols = [slice(h, h + 1) for h in heads]
    seqs = [slice(b * tc, (b + 1) * tc) for b in range(bt)]
    q_h, k_h, v_h = ([x[:, sl] for sl in lanes] for x in (q, k, v))
    pm = [jnp.where(incl, jnp.exp(col_all[:, cols[h]] + src_rows[cols[h], :]), 0.0)
          * _dot_nt(q_h[h].astype(BF16), k_h[h].astype(BF16)) for h in heads]
    intra = [_dot(pm[h], jnp.concatenate([v_h[h], ones], axis=-1)) for h in heads]
    c_old = [[c_scr[b, h] for b in range(bt)] for h in heads]
    n_old = [[jnp.broadcast_to(n_scr[b, cols[h], :], (SUBLANES, HEAD_DIM)) for b in range(bt)]
             for h in heads]
    from_state = [jnp.concatenate(
        [_dot_nt(q_h[h][rs].astype(BF16),
                 jnp.concatenate([c_old[h][b], n_old[h][b], pad], axis=0).astype(BF16))
         for b, rs in enumerate(seqs)], axis=0) for h in heads]
    nd = [intra[h] + ei_all[:, cols[h]] * from_state[h] for h in heads]
    o = jnp.concatenate(
        [nd[h][:, :HEAD_DIM] / jnp.maximum(jnp.abs(nd[h][:, HEAD_DIM:HEAD_DIM + 1]),
                                           floor_all[:, cols[h]]) for h in heads], axis=-1)
    for h in heads:
        sc = sc_all[:, cols[h]]
        scv = sc * v_h[h]
        for b, rs in enumerate(seqs):
            c_scr[b, h] = (decay3[b, :, cols[h]] * c_old[h][b]
                           + _dot_tn(scv[rs].astype(BF16), k_h[h][rs].astype(BF16)))
        sck = (sc * k_h[h]).reshape(bt, tc, HEAD_DIM)
        n_scr[:, cols[h], :] = (decay3[:, :, cols[h]] * n_scr[:, cols[h], :]
                                + jnp.sum(sck, axis=1, keepdims=True))
    hsum = hsum_ref[...]
    cen = o - _dot(o, hsum) * (1.0 / HEAD_DIM)
    var = _dot(cen * cen, hsum) * (1.0 / HEAD_DIM)
    y = cen * lax.rsqrt(var + EPS) * lng_ref[...] * jax.nn.sigmoid(flat(og_ref))
    y_ref[...] = y.reshape(bt, tc, GROUP_W)

    @pl.when(c == pl.num_programs(1) - 1)
    def _():
        c_ref[...] = c_scr[...]
        n_ref[...] = n_scr[...]


def _mlstm_call(z, c0, n0, m0, c_all, w, layer, state_layer, bt, tc):
    b, l, _ = z.shape
    ws = functools.partial(_layer_spec, layer)
    cb = (RW_COLS + 3 * GROUP_W) // GROUP_W
    mat = (HEADS, HEAD_DIM, HEAD_DIM)
    stacked = functools.partial(_stacked_state_spec, state_layer, bt)
    return pl.pallas_call(
        functools.partial(_mlstm_kernel, bt=bt, tc=tc),
        grid=(b // bt, l // tc),
        in_specs=[_seq_spec(bt, tc, GROUP_W, cb), _seq_spec(bt, tc, GROUP_W, cb + 1),
                  _seq_spec(bt, tc, GROUP_W, cb + 2), _seq_spec(bt, tc, GROUP_W, cb + 3),
                  _seq_spec(bt, tc, LANES, IGATE_COL_BLOCK),
                  _seq_spec(bt, tc, LANES, IGATE_COL_BLOCK + 1),
                  stacked(*mat), stacked(HEADS, HEAD_DIM), stacked(1, LANES),
                  ws(1, LANES), ws(1, LANES), ws(1, GROUP_W),
                  _const_spec((GROUP_W, GROUP_W)), ANY_SPEC],
        out_specs=[_seq_spec(bt, tc, GROUP_W, 0), _stacked_state_spec(layer, bt, *mat),
                   _state_spec(bt, HEADS, HEAD_DIM), _state_spec(bt, 1, LANES)],
        out_shape=[jax.ShapeDtypeStruct((b, l, GROUP_W), F32),
                   jax.ShapeDtypeStruct(c_all.shape, F32),
                   jax.ShapeDtypeStruct((b, HEADS, HEAD_DIM), F32),
                   jax.ShapeDtypeStruct((b, 1, LANES), F32)],
        input_output_aliases={13: 1},
        scratch_shapes=[pltpu.VMEM((bt,) + mat, F32),
                        pltpu.VMEM((bt, HEADS, HEAD_DIM), F32),
                        pltpu.VMEM((bt, 1, LANES), F32),
                        pltpu.VMEM((SUBLANES + bt * tc, LANES), F32)],
        compiler_params=_params(("parallel", "arbitrary")),
        name="mlstm",
    )(z, z, z, z, z, z, c0, n0, m0, w["ml_bi"], w["ml_bf"], w["ml_ln_g"], w["head_sum"], c_all)


def _block_diag(blocks):
    *lead, n, r, c = blocks.shape
    eye = jnp.eye(n, dtype=blocks.dtype)
    out = eye[:, None, :, None] * blocks[..., :, :, None, :]
    return out.reshape(*lead, n * r, n * c)


def _prep(w):
    depth = w["ffn1_pre"].shape[0]
    row = lambda t: t.reshape(depth, 1, -1).astype(F32)
    out = {}
    for f in ("ffn1", "ffn2"):
        out[f + "_pre"] = row(w[f + "_pre"])
        out[f + "_post"] = row(w[f + "_post"])
        out[f + "_wg"] = _cast_t_call(jnp.swapaxes(w[f + "_wg"], 1, 2), D_FF_PAD,
                                      _row_chunks(D_FF))
        out[f + "_wu"] = _cast_t_call(jnp.swapaxes(w[f + "_wu"], 1, 2), D_FF_PAD,
                                      _row_chunks(D_FF))
        out[f + "_wd"] = _cast_call(w[f + "_wd"], D_FF_PAD, D_MODEL)
    out["mix_pre"] = row(w["mix_pre"])
    out["mix_post"] = row(w["mix_post"])
    out["w_in"] = _cast_t_call(
        jnp.swapaxes(w["w_in"], 1, 2), IN_COLS_PAD,
        _row_chunks(MAIN_COLS) + ((MAIN_COLS, HEADS, MAIN_COLS),
                                  (MAIN_COLS + HEADS, HEADS, MAIN_COLS + LANES)))
    out["w_out"] = _cast_call(w["w_out"], D_MODEL, D_MODEL)

    for name in ("rw_mu", "rw_w0", "rw_a0", "rw_kk", "rw_ka", "rw_rk", "rw_ln_g", "rw_ln_b"):
        out[name] = row(w[name])
    for name in ("rw_w2", "rw_a2", "rw_g2"):
        out[name] = w[name].astype(BF16)
    out["head_sum"] = _block_diag(jnp.ones((HEADS, HEAD_DIM, HEAD_DIM), BF16))

    out["lru_conv_w"] = w["lru_conv_w"].astype(F32)
    out["lru_conv_b"] = row(w["lru_conv_b"])
    out["lru_wax"] = jnp.concatenate(
        [_block_diag(w["lru_wa"]), _block_diag(w["lru_wx"])], axis=-1).astype(BF16)
    out["lru_bax"] = jnp.concatenate([row(w["lru_ba"]), row(w["lru_bx"])], axis=-1)
    out["lru_lambda"] = row(w["lru_lambda"])
    out["lru_norm"] = row(w["lru_norm"])

    lr, li = w["s5_a_re"].astype(F32), w["s5_a_im"].astype(F32)
    dt = jnp.exp(w["s5_log_dt"].astype(F32))[..., None]
    mag = jnp.exp(lr * dt)
    ar, ai = mag * jnp.cos(li * dt), mag * jnp.sin(li * dt)
    den = lr * lr + li * li
    zr = ((ar - 1.0) * lr + ai * li) / den
    zi = (ai * lr - (ar - 1.0) * li) / den
    b_re, b_im = w["s5_b_re"].astype(F32), w["s5_b_im"].astype(F32)
    bbr = zr[..., None] * b_re - zi[..., None] * b_im
    bbi = zr[..., None] * b_im + zi[..., None] * b_re
    packed = lambda t: _block_diag(jnp.swapaxes(t, -1, -2))
    out["s5_ar"], out["s5_ai"] = row(ar), row(ai)
    out["s5_bb"] = jnp.concatenate([packed(bbr), packed(bbi)], axis=-1).astype(BF16)
    out["s5_cc"] = jnp.concatenate(
        [packed(w["s5_c_re"]), -packed(w["s5_c_im"])], axis=-2).astype(BF16)
    out["s5_d"] = row(w["s5_d"])
    out["s5_glu_w"] = w["s5_glu_w"].astype(BF16)
    out["s5_glu_b"] = row(w["s5_glu_b"])
    out["s5_norm"] = row(w["s5_norm"])

    lane_block = lambda t: jnp.pad(t.astype(F32), ((0, 0), (0, LANES - HEADS))).reshape(
        depth, 1, LANES)
    out["ml_bi"] = lane_block(w["ml_bi"])
    out["ml_bf"] = lane_block(w["ml_bf"])
    out["ml_ln_g"] = row(w["ml_ln_g"])
    return out


def _tiling(b, l, rows_cap, tc_cap, seq_multiple=1):
    tc = tc_cap
    while tc > SUBLANES and l % tc:
        tc //= 2
    if l % tc or b % seq_multiple:
        raise ValueError(f"unsupported batch {b} / sequence length {l}")
    bt = seq_multiple
    while bt * 2 * tc <= rows_cap and b % (bt * 2) == 0:
        bt *= 2
    return bt, tc


def _stack_states(states):
    shift, wkv, conv, h, re, im, c, n, m = (t.astype(F32) for t in states)
    layers, b = shift.shape[:2]
    return (shift.reshape(layers, b, 1, RW_COLS), wkv, conv, h,
            re.reshape(layers, b, S5_STATE), im.reshape(layers, b, S5_STATE), c, n,
            jnp.pad(m, ((0, 0), (0, 0), (0, LANES - HEADS))).reshape(layers, b, 1, LANES))


def _zero_states(b):
    mat = (1, b, HEADS, HEAD_DIM, HEAD_DIM)
    return _stack_states((
        jnp.zeros((1, b, RW_COLS), F32), jnp.zeros(mat, F32),
        jnp.zeros((1, b, CONV_W - 1, GROUP_W), F32), jnp.zeros((1, b, GROUP_W), F32),
        jnp.zeros((1, b, S5_GROUPS, S5_P), F32), jnp.zeros((1, b, S5_GROUPS, S5_P), F32),
        jnp.zeros(mat, F32), jnp.zeros((1, b, HEADS, HEAD_DIM), F32),
        jnp.zeros((1, b, HEADS), F32)))


def _layer(x, st, state_layer, mats, w, layer):
    b, l, d = x.shape
    n = b * l
    shift0, s0, buf0, h0, re0, im0, c0, n0, m0 = st
    s_all, c_all = mats
    x1, z = _ffn_in_call(x.reshape(n, d), w, layer)
    z = z.reshape(b, l, IN_COLS_PAD)

    y_rw, n_shift, s_all = _rwkv_call(z, shift0, s0, s_all, w, layer, state_layer,
                                      *_tiling(b, l, 256, 32))
    y_lru, n_buf, n_h = _lru_call(z, buf0, h0, w, layer, state_layer,
                                  *_tiling(b, l, 1024, 128, SUBLANES))
    y_s5, n_re, n_im = _s5_call(z, re0, im0, w, layer, state_layer,
                                *_tiling(b, l, 1024, 128, SUBLANES))
    y_ml, c_all, n_n, n_m = _mlstm_call(z, c0, n0, m0, c_all, w, layer, state_layer,
                                        *_tiling(b, l, 256, 128))

    flat = lambda t: t.reshape(n, GROUP_W)
    x3 = _out_ffn_call(x1, flat(y_rw), flat(y_lru), flat(y_s5), flat(y_ml), w, layer)
    new = (n_shift.reshape(b, RW_COLS), n_buf, n_h,
           n_re.reshape(b, S5_GROUPS, S5_P), n_im.reshape(b, S5_GROUPS, S5_P), n_n,
           n_m.reshape(b, LANES)[:, :HEADS])
    return x3.reshape(b, l, d), new, (s_all, c_all)


def kernel(x_prompt, x_sample, state_rwkv_shift, state_rwkv_wkv, state_lru_conv, state_lru_h, state_s5_re, state_s5_im, state_mlstm_C, state_mlstm_n, state_mlstm_m, ffn1_pre, ffn1_wg, ffn1_wu, ffn1_wd, ffn1_post, mix_pre, w_in, w_out, mix_post, ffn2_pre, ffn2_wg, ffn2_wu, ffn2_wd, ffn2_post, rw_mu, rw_w0, rw_w2, rw_a0, rw_a2, rw_g2, rw_kk, rw_ka, rw_rk, rw_ln_g, rw_ln_b, lru_conv_w, lru_conv_b, lru_wa, lru_ba, lru_wx, lru_bx, lru_lambda, lru_norm, s5_a_re, s5_a_im, s5_log_dt, s5_b_re, s5_b_im, s5_c_re, s5_c_im, s5_d, s5_glu_w, s5_glu_b, s5_norm, ml_bi, ml_bf, ml_ln_g):
    w = _prep(dict(
        ffn1_pre=ffn1_pre, ffn1_wg=ffn1_wg, ffn1_wu=ffn1_wu, ffn1_wd=ffn1_wd, ffn1_post=ffn1_post,
        mix_pre=mix_pre, w_in=w_in, w_out=w_out, mix_post=mix_post,
        ffn2_pre=ffn2_pre, ffn2_wg=ffn2_wg, ffn2_wu=ffn2_wu, ffn2_wd=ffn2_wd, ffn2_post=ffn2_post,
        rw_mu=rw_mu, rw_w0=rw_w0, rw_w2=rw_w2, rw_a0=rw_a0, rw_a2=rw_a2, rw_g2=rw_g2,
        rw_kk=rw_kk, rw_ka=rw_ka, rw_rk=rw_rk, rw_ln_g=rw_ln_g, rw_ln_b=rw_ln_b,
        lru_conv_w=lru_conv_w, lru_conv_b=lru_conv_b, lru_wa=lru_wa, lru_ba=lru_ba,
        lru_wx=lru_wx, lru_bx=lru_bx, lru_lambda=lru_lambda, lru_norm=lru_norm,
        s5_a_re=s5_a_re, s5_a_im=s5_a_im, s5_log_dt=s5_log_dt, s5_b_re=s5_b_re, s5_b_im=s5_b_im,
        s5_c_re=s5_c_re, s5_c_im=s5_c_im, s5_d=s5_d, s5_glu_w=s5_glu_w, s5_glu_b=s5_glu_b,
        s5_norm=s5_norm, ml_bi=ml_bi, ml_bf=ml_bf, ml_ln_g=ml_ln_g))
    sample_state = _stack_states((state_rwkv_shift, state_rwkv_wkv, state_lru_conv, state_lru_h,
                                  state_s5_re, state_s5_im, state_mlstm_C, state_mlstm_n,
                                  state_mlstm_m))
    depth = ffn1_pre.shape[0]
    b_p = x_prompt.shape[0]
    prompt_state = _zero_states(b_p)
    mat_zeros = lambda b: jnp.zeros((depth, b, HEADS, HEAD_DIM, HEAD_DIM), F32)
    mats_p = (mat_zeros(b_p), mat_zeros(b_p))
    mats_s = (mat_zeros(x_sample.shape[0]), mat_zeros(x_sample.shape[0]))
    y_p, y_s = x_prompt.astype(F32), x_sample.astype(F32)
    outs_p, outs_s = [], []
    for layer in range(depth):
        y_p, st_p, mats_p = _layer(y_p, prompt_state, 0, mats_p, w, layer)
        y_s, st_s, mats_s = _layer(y_s, sample_state, layer, mats_s, w, layer)
        outs_p.append(st_p)
        outs_s.append(st_s)

    def gather(outs, mats):
        shift, conv, h, re, im, n, m = (jnp.stack([o[i] for o in outs]) for i in range(7))
        return (shift, mats[0], conv, h, re, im, mats[1], n, m)

    return (y_p, y_s, *gather(outs_p, mats_p), *gather(outs_s, mats_s))
```

```python
import functools

import jax
import jax.numpy as jnp
from jax import lax
from jax.experimental import pallas as pl
from jax.experimental.pallas import tpu as pltpu

F32 = jnp.float32
BF16 = jnp.bfloat16

LANES = 128
SUBLANES = 8
VMEM_LIMIT_BYTES = 56 * 1024 * 1024

D_MODEL = 1024
GROUP_W = D_MODEL // 4
HEAD_DIM = 64
HEADS = GROUP_W // HEAD_DIM
RW_W_RANK = 64
RW_A_RANK = 64
RW_G_RANK = 128
RW_COLS = 3 * GROUP_W + RW_W_RANK + RW_A_RANK + RW_G_RANK
RW_DECAY_SCALE = 0.606531
RW_GN_EPS = 64e-5
LRU_C = 8.0
CONV_W = 4
S5_GROUP = 16
S5_GROUPS = GROUP_W // S5_GROUP
S5_P = 64
S5_STATE = S5_GROUPS * S5_P
MAIN_COLS = RW_COLS + 7 * GROUP_W
IN_COLS = MAIN_COLS + 2 * HEADS
IN_COLS_PAD = MAIN_COLS + 2 * LANES
IGATE_COL_BLOCK = MAIN_COLS // LANES
D_FF = 2752
D_FF_PAD = -(-D_FF // LANES) * LANES
EPS = 1e-6
HI = lax.Precision.HIGHEST


def _dot(a, b):
    return jnp.dot(a.astype(BF16), b.astype(BF16), preferred_element_type=F32)


def _dot_nt(a, b, precision=None):
    return lax.dot_general(a, b, (((1,), (1,)), ((), ())), precision=precision,
                           preferred_element_type=F32)


def _dot_tn(a, b):
    return lax.dot_general(a, b, (((0,), (0,)), ((), ())), preferred_element_type=F32)


def _rms(x, g):
    return x * lax.rsqrt(jnp.mean(x * x, axis=-1, keepdims=True) + EPS) * g


def _seq_masks(bt, tc, reps=1):
    rows = bt * tc
    r = lax.broadcasted_iota(jnp.int32, (rows, reps * rows), 0)
    c = lax.broadcasted_iota(jnp.int32, (rows, reps * rows), 1) & (rows - 1)
    if bt == 1:
        return r > c, r >= c
    start = r - (r & (tc - 1))
    return (c < r) & (c >= start), (c <= r) & (c >= start)


def _later_mask(bt, tc, reps=1):
    rows = bt * tc
    r = lax.broadcasted_iota(jnp.int32, (reps * rows, rows), 0) & (rows - 1)
    c = lax.broadcasted_iota(jnp.int32, (reps * rows, rows), 1)
    if bt == 1:
        return r < c
    return (r < c) & (r >= c - (c & (tc - 1)))


def _masked_sum(mask, x):
    m = mask.astype(BF16)
    hi = x.astype(BF16)
    rest = x - hi.astype(F32)
    mid = rest.astype(BF16)
    lo = (rest - mid.astype(F32)).astype(BF16)
    dot = lambda piece: jnp.dot(m, piece, preferred_element_type=F32)
    return dot(hi) + dot(mid) + dot(lo)


def _const_spec(shape):
    nd = len(shape)
    return pl.BlockSpec(shape, lambda *_: (0,) * nd, pipeline_mode=pl.Buffered(1))


def _layer_spec(layer, *shape):
    zeros = (0,) * len(shape)
    return pl.BlockSpec((None,) + shape, lambda *_: (layer,) + zeros,
                        pipeline_mode=pl.Buffered(1))


def _params(sem):
    return pltpu.CompilerParams(dimension_semantics=sem, vmem_limit_bytes=VMEM_LIMIT_BYTES)


def _cast_kernel(x_ref, o_ref):
    rows, cols = x_ref.shape
    o_ref[...] = jnp.zeros(o_ref.shape, BF16)
    o_ref[0:rows, 0:cols] = x_ref[...].astype(BF16)


def _cast_call(x, out_rows, out_cols):
    depth, rows, cols = x.shape
    return pl.pallas_call(
        _cast_kernel,
        grid=(depth,),
        in_specs=[pl.BlockSpec((None, rows, cols), lambda i: (i, 0, 0))],
        out_specs=pl.BlockSpec((None, out_rows, out_cols), lambda i: (i, 0, 0)),
        out_shape=jax.ShapeDtypeStruct((depth, out_rows, out_cols), BF16),
        compiler_params=_params(("parallel",)),
        name="cast_weights",
    )(x)


def _cast_t_kernel(x_ref, o_ref, *, segments):
    cols = x_ref.shape[1]
    for src, n, dst in segments:
        chunk = x_ref[src:src + n, :]
        if n < LANES:
            chunk = jnp.concatenate([chunk, jnp.zeros((LANES - n, cols), chunk.dtype)], axis=0)
        o_ref[:, dst:dst + LANES] = chunk.T.astype(BF16)


def _row_chunks(n):
    return tuple((i, min(LANES, n - i), i) for i in range(0, n, LANES))


def _cast_t_call(x_t, out_cols, segments):
    depth, rows, cols = x_t.shape
    assert len(segments) * LANES == out_cols
    return pl.pallas_call(
        functools.partial(_cast_t_kernel, segments=segments),
        grid=(depth,),
        in_specs=[pl.BlockSpec((None, rows, cols), lambda i: (i, 0, 0))],
        out_specs=pl.BlockSpec((None, cols, out_cols), lambda i: (i, 0, 0)),
        out_shape=jax.ShapeDtypeStruct((depth, cols, out_cols), BF16),
        compiler_params=_params(("parallel",)),
        name="cast_weights_t",
    )(x_t)


FFN_ROW_SPLIT = 2


def _row_parts(n):
    step = n // FFN_ROW_SPLIT
    return [slice(i * step, (i + 1) * step) for i in range(FFN_ROW_SPLIT)]


def _swiglu_half_step(xs, pre, wg, wu, wd, post):
    hs = [_rms(x, pre).astype(BF16) for x in xs]
    gs = [jnp.dot(h, wg, preferred_element_type=F32) for h in hs]
    us = [jnp.dot(h, wu, preferred_element_type=F32) for h in hs]
    acts = [(jax.nn.silu(g) * u).astype(BF16) for g, u in zip(gs, us)]
    fs = [jnp.dot(act, wd, preferred_element_type=F32) for act in acts]
    return [x + 0.5 * _rms(f, post) for x, f in zip(xs, fs)]


def _ffn_in_kernel(x_ref, pre_ref, wg_ref, wu_ref, wd_ref, post_ref, mpre_ref, win_ref,
                   xo_ref, z_ref):
    parts = _row_parts(x_ref.shape[0])
    x1 = _swiglu_half_step([x_ref[rs, :] for rs in parts], pre_ref[...], wg_ref[...],
                           wu_ref[...], wd_ref[...], post_ref[...])
    hs = [_rms(x, mpre_ref[...]).astype(BF16) for x in x1]
    zs = [jnp.dot(h, win_ref[...], preferred_element_type=F32) for h in hs]
    for rs, x, z in zip(parts, x1, zs):
        xo_ref[rs, :] = x
        z_ref[rs, :] = z


def _out_ffn_kernel(x_ref, yrw_ref, ylru_ref, ys5_ref, yml_ref, wout_ref, mpost_ref,
                    pre_ref, wg_ref, wu_ref, wd_ref, post_ref, xo_ref):
    parts = _row_parts(x_ref.shape[0])
    cats = [jnp.concatenate([yrw_ref[rs, :], ylru_ref[rs, :], ys5_ref[rs, :], yml_ref[rs, :]],
                            axis=-1).astype(BF16) for rs in parts]
    ms = [jnp.dot(cat, wout_ref[...], preferred_element_type=F32) for cat in cats]
    x2 = [x_ref[rs, :] + _rms(m, mpost_ref[...]) for rs, m in zip(parts, ms)]
    x3 = _swiglu_half_step(x2, pre_ref[...], wg_ref[...], wu_ref[...], wd_ref[...],
                           post_ref[...])
    for rs, x in zip(parts, x3):
        xo_ref[rs, :] = x


def _token_tile(n):
    for tm in (512, 256, 128, 64, 32, 16, 8):
        if n % tm == 0:
            return tm
    raise ValueError(f"token count {n} is not a multiple of {SUBLANES}")


def _ffn_specs(ws):
    return [ws(1, D_MODEL), ws(D_MODEL, D_FF_PAD), ws(D_MODEL, D_FF_PAD), ws(D_FF_PAD, D_MODEL),
            ws(1, D_MODEL)]


def _ffn_in_call(x, w, layer):
    n = x.shape[0]
    tm = _token_tile(n)
    row = lambda width: pl.BlockSpec((tm, width), lambda i: (i, 0))
    ws = functools.partial(_layer_spec, layer)
    return pl.pallas_call(
        _ffn_in_kernel,
        grid=(n // tm,),
        in_specs=[row(D_MODEL)] + _ffn_specs(ws) + [ws(1, D_MODEL), ws(D_MODEL, IN_COLS_PAD)],
        out_specs=[row(D_MODEL), row(IN_COLS_PAD)],
        out_shape=[jax.ShapeDtypeStruct((n, D_MODEL), F32),
                   jax.ShapeDtypeStruct((n, IN_COLS_PAD), F32)],
        compiler_params=_params(("parallel",)),
        name="ffn1_inproj",
    )(x, w["ffn1_pre"], w["ffn1_wg"], w["ffn1_wu"], w["ffn1_wd"], w["ffn1_post"], w["mix_pre"],
      w["w_in"])


def _out_ffn_call(x, yrw, ylru, ys5, yml, w, layer):
    n = x.shape[0]
    tm = _token_tile(n)
    row = lambda width: pl.BlockSpec((tm, width), lambda i: (i, 0))
    ws = functools.partial(_layer_spec, layer)
    return pl.pallas_call(
        _out_ffn_kernel,
        grid=(n // tm,),
        in_specs=[row(D_MODEL), row(GROUP_W), row(GROUP_W), row(GROUP_W), row(GROUP_W),
                  ws(D_MODEL, D_MODEL), ws(1, D_MODEL)] + _ffn_specs(ws),
        out_specs=row(D_MODEL),
        out_shape=jax.ShapeDtypeStruct((n, D_MODEL), F32),
        compiler_params=_params(("parallel",)),
        name="outproj_ffn2",
    )(x, yrw, ylru, ys5, yml, w["w_out"], w["mix_post"], w["ffn2_pre"], w["ffn2_wg"],
      w["ffn2_wu"], w["ffn2_wd"], w["ffn2_post"])


def _seq_spec(bt, tc, w, col_block):
    return pl.BlockSpec((bt, tc, w), lambda i, c: (i, c, col_block))


def _state_spec(bt, *dims):
    zeros = (0,) * len(dims)
    return pl.BlockSpec((bt,) + dims, lambda i, c: (i,) + zeros)


def _stacked_state_spec(layer, bt, *dims):
    zeros = (0,) * len(dims)
    return pl.BlockSpec((None, bt) + dims, lambda i, c: (layer, i) + zeros)


ANY_SPEC = pl.BlockSpec(memory_space=pl.ANY)


def _rwkv_kernel(*refs, bt, tc, subs):
    for sub in range(subs):
        _rwkv_chunk(*refs, bt=bt, tc=tc, ts=slice(sub * tc, (sub + 1) * tc),
                    first=sub == 0, final=sub == subs - 1)


def _rwkv_chunk(p_ref, shift0_ref, s0_ref, mu_ref, w0_ref, w2_ref, a0_ref, a2_ref, g2_ref,
                kkw_ref, ka_ref, rk_ref, lng_ref, lnb_ref, hsum_ref, _s_all_ref,
                y_ref, shift_ref, s_ref, pbuf, s_scr, *, bt, tc, ts, first, final):
    c = pl.program_id(1)
    rows = bt * tc
    lo = SUBLANES - 1

    if first:
        @pl.when(c == 0)
        def _():
            pbuf[:, lo:SUBLANES, :] = shift0_ref[...]
            s_scr[...] = s0_ref[...]

    p3 = p_ref[:, ts, :]
    pbuf[:, SUBLANES:SUBLANES + tc, :] = p3
    prev = pbuf[:, lo:lo + tc, :].reshape(rows, RW_COLS)
    last = p3[:, tc - 1:tc, :]
    pbuf[:, lo:SUBLANES, :] = last
    shift_ref[...] = last
    p = p3.reshape(rows, RW_COLS)

    xm = p + (prev - p) * mu_ref[...]
    g = GROUP_W
    r, k, v = xm[:, 0:g], xm[:, g:2 * g], xm[:, 2 * g:3 * g]
    lw = xm[:, 3 * g:3 * g + RW_W_RANK]
    la = xm[:, 3 * g + RW_W_RANK:3 * g + RW_W_RANK + RW_A_RANK]
    lg = xm[:, 3 * g + RW_W_RANK + RW_A_RANK:]
    logw = -RW_DECAY_SCALE * jax.nn.sigmoid(w0_ref[...] + _dot(jnp.tanh(lw), w2_ref[...]))
    a = jax.nn.sigmoid(a0_ref[...] + _dot(la, a2_ref[...]))
    gate = _dot(jax.nn.sigmoid(lg), g2_ref[...])

    hsum = hsum_ref[...]
    kk = k * kkw_ref[...]
    kk = kk * lax.rsqrt(jnp.maximum(_dot(kk * kk, hsum), 1e-12))
    k2 = k * (1.0 + (a - 1.0) * ka_ref[...])

    _, incl = _seq_masks(bt, tc)
    _, incl2 = _seq_masks(bt, tc, reps=2)
    later2 = _later_mask(bt, tc, reps=2)
    cum = _masked_sum(incl, logw)
    cum3 = cum.reshape(bt, tc, g)
    total3 = cum3[:, tc - 1:tc, :]
    rest = jnp.exp(total3 - cum3).reshape(rows, g)
    wtot3 = jnp.exp(total3)
    inv = jnp.exp(-cum)
    kka = kk * a
    at = -kk * jnp.exp(cum - logw)
    rt = r * jnp.exp(cum)
    bk_t = kka * inv, k2 * inv
    bk_w = kka * rest, k2 * rest

    n_double = max(1, (tc - 1).bit_length())
    heads = range(HEADS)
    lanes = [slice(h * HEAD_DIM, (h + 1) * HEAD_DIM) for h in heads]
    seqs = [slice(b * tc, (b + 1) * tc) for b in range(bt)]
    a_h, r_h, v_h = ([x[:, sl] for sl in lanes] for x in (at, rt, v))
    bk_h = [jnp.concatenate([bk_t[0][:, sl], bk_t[1][:, sl]], axis=0).astype(BF16)
            for sl in lanes]
    m_at = [jnp.where(later2, _dot_nt(bk_h[h], a_h[h].astype(BF16)), 0.0) for h in heads]
    m_r = [jnp.where(incl2, _dot_nt(r_h[h].astype(BF16), bk_h[h]), 0.0) for h in heads]
    s_old = [[s_scr[b, h] for b in range(bt)] for h in heads]
    from_state = [[_dot_nt(jnp.concatenate([a_h[h][rs], r_h[h][rs]], axis=0).astype(BF16),
                           s_old[h][b].astype(BF16)) for b, rs in enumerate(seqs)]
                  for h in heads]
    u0_t = jnp.concatenate([jnp.concatenate([fs[:tc] for fs in from_state[h]], axis=0)
                            for h in heads], axis=-1).T
    v_t = v.T
    u_t = [u0_t[sl] + _dot(v_t[sl], m_at[h][rows:]) for h, sl in enumerate(lanes)]
    pw = [m_at[h][:rows] for h in heads]
    for i in range(n_double):
        u_t = [u_t[h] + _dot(u_t[h], pw[h]) for h in heads]
        if i + 1 < n_double:
            pw = [_dot(pw[h], pw[h]) for h in heads]
    u_all = jnp.concatenate(u_t, axis=0).T
    uv = [jnp.concatenate([u_all[:, sl], v_h[h]], axis=0) for h, sl in enumerate(lanes)]
    o = jnp.concatenate(
        [jnp.concatenate([fs[tc:] for fs in from_state[h]], axis=0) + _dot(m_r[h], uv[h])
         for h in heads], axis=-1)
    for h, sl in enumerate(lanes):
        bkw = jnp.concatenate([bk_w[0][:, sl], bk_w[1][:, sl]], axis=0)
        for b, rs in enumerate(seqs):
            rs2 = slice(rows + rs.start, rows + rs.stop)
            uv_b = jnp.concatenate([uv[h][rs], uv[h][rs2]], axis=0)
            bkw_b = jnp.concatenate([bkw[rs], bkw[rs2]], axis=0)
            s_scr[b, h] = (s_old[h][b] * wtot3[b, :, sl]
                           + _dot_tn(uv_b.astype(BF16), bkw_b.astype(BF16)))

    mean = _dot(o, hsum) * (1.0 / HEAD_DIM)
    cen = o - mean
    var = _dot(cen * cen, hsum) * (1.0 / HEAD_DIM)
    y = cen * lax.rsqrt(var + RW_GN_EPS) * lng_ref[...] + lnb_ref[...]
    bonus = _dot(r * k2 * rk_ref[...], hsum) * v
    y_ref[:, ts, :] = ((y + bonus) * gate).reshape(bt, tc, g)

    if final:
        @pl.when(c == pl.num_programs(1) - 1)
        def _():
            s_ref[...] = s_scr[...]


def _rwkv_call(z, shift0, s0, s_all, w, layer, state_layer, bt, tc):
    b, l, _ = z.shape
    ws = functools.partial(_layer_spec, layer)
    mat = (HEADS, HEAD_DIM, HEAD_DIM)
    subs = 2 if l % (2 * tc) == 0 else 1
    return pl.pallas_call(
        functools.partial(_rwkv_kernel, bt=bt, tc=tc, subs=subs),
        grid=(b // bt, l // (tc * subs)),
        in_specs=[_seq_spec(bt, tc * subs, RW_COLS, 0),
                  _stacked_state_spec(state_layer, bt, 1, RW_COLS),
                  _stacked_state_spec(state_layer, bt, *mat),
                  ws(1, RW_COLS), ws(1, GROUP_W), ws(RW_W_RANK, GROUP_W), ws(1, GROUP_W),
                  ws(RW_A_RANK, GROUP_W), ws(RW_G_RANK, GROUP_W),
                  ws(1, GROUP_W), ws(1, GROUP_W), ws(1, GROUP_W), ws(1, GROUP_W),
                  ws(1, GROUP_W), _const_spec((GROUP_W, GROUP_W)), ANY_SPEC],
        out_specs=[_seq_spec(bt, tc * subs, GROUP_W, 0), _state_spec(bt, 1, RW_COLS),
                   _stacked_state_spec(layer, bt, *mat)],
        out_shape=[jax.ShapeDtypeStruct((b, l, GROUP_W), F32),
                   jax.ShapeDtypeStruct((b, 1, RW_COLS), F32),
                   jax.ShapeDtypeStruct(s_all.shape, F32)],
        input_output_aliases={15: 2},
        scratch_shapes=[pltpu.VMEM((bt, SUBLANES + tc, RW_COLS), F32),
                        pltpu.VMEM((bt,) + mat, F32)],
        compiler_params=_params(("parallel", "arbitrary")),
        name="rwkv7",
    )(z, shift0, s0, w["rw_mu"], w["rw_w0"], w["rw_w2"], w["rw_a0"], w["rw_a2"], w["rw_g2"],
      w["rw_kk"], w["rw_ka"], w["rw_rk"], w["rw_ln_g"], w["rw_ln_b"], w["head_sum"], s_all)


def _lane_tiles(x):
    return [x[:, k * LANES:(k + 1) * LANES] for k in range(x.shape[-1] // LANES)]


def _time_major(ref, groups, tc):
    return jnp.concatenate([ref[g * SUBLANES:(g + 1) * SUBLANES, t, :]
                            for g in range(groups) for t in range(tc)], axis=0)


def _seq_rows(g, j, tc):
    return pl.ds(g * tc * SUBLANES + j, tc, stride=SUBLANES)


def _lru_kernel(xlo_ref, xhi_ref, gate_ref, buf0lo_ref, buf0hi_ref, h0_ref, cw_ref, cb_ref,
                wax_ref, bax_ref, lam_ref, norm_ref, y_ref, buf_ref, h_ref,
                cbuf, abuf, bbuf, h_scr, *, bt, tc):
    c = pl.program_id(1)
    rows = bt * tc
    groups = bt // SUBLANES
    n_tiles = GROUP_W // LANES
    hist = (CONV_W - 1) * SUBLANES
    span = hist + tc * SUBLANES
    tile = lambda k: slice(k * LANES, (k + 1) * LANES)

    @pl.when(c == 0)
    def _():
        h_scr[...] = h0_ref[...]
        for k, ref in enumerate((buf0lo_ref, buf0hi_ref)):
            for g in range(groups):
                for j in range(CONV_W - 1):
                    cbuf[k, pl.ds(g * span + j * SUBLANES, SUBLANES), :] = (
                        ref[g * SUBLANES:(g + 1) * SUBLANES, j, :])

    cw, cb = cw_ref[...], cb_ref[...]
    xc_tiles = []
    for k, ref in enumerate((xlo_ref, xhi_ref)):
        x_tm = _time_major(ref, groups, tc)
        per_group = []
        for g in range(groups):
            base = g * span
            cbuf[k, pl.ds(base + hist, tc * SUBLANES), :] = x_tm[g * tc * SUBLANES:
                                                                (g + 1) * tc * SUBLANES]
            acc = cb[:, tile(k)]
            for j in range(CONV_W):
                acc = acc + (cbuf[k, pl.ds(base + j * SUBLANES, tc * SUBLANES), :]
                             * cw[j:j + 1, tile(k)])
            per_group.append(acc)
            cbuf[k, pl.ds(base, hist), :] = cbuf[k, pl.ds(base + tc * SUBLANES, hist), :]
        xc_tiles.append(jnp.concatenate(per_group, axis=0))
    xc = jnp.concatenate(xc_tiles, axis=-1)

    gates = jax.nn.sigmoid(_dot(xc, wax_ref[...]) + bax_ref[...])
    gate_r, gate_i = gates[:, :GROUP_W], gates[:, GROUP_W:]
    log_a = -LRU_C * gate_r * jax.nn.softplus(-lam_ref[...])
    a = jnp.exp(log_a)
    bb = jnp.sqrt(-jnp.tanh(log_a) * (a * a + 1.0)) * (gate_i * xc)
    for k, (a_k, b_k) in enumerate(zip(_lane_tiles(a), _lane_tiles(bb))):
        abuf[k] = a_k
        bbuf[k] = b_k

    for g in range(groups):
        seqs = slice(g * SUBLANES, (g + 1) * SUBLANES)
        h = _lane_tiles(h_scr[seqs, :])
        for t in range(tc):
            slab = pl.ds((g * tc + t) * SUBLANES, SUBLANES)
            for k in range(n_tiles):
                h[k] = abuf[k, slab, :] * h[k] + bbuf[k, slab, :]
                bbuf[k, slab, :] = h[k]
        h_scr[seqs, :] = jnp.concatenate(h, axis=-1)
    h_ref[...] = h_scr[...]

    h_all = jnp.concatenate(
        [jnp.concatenate([bbuf[k, _seq_rows(g, j, tc), :] for k in range(n_tiles)], axis=-1)
         for g in range(groups) for j in range(SUBLANES)], axis=0)
    gate = gate_ref[...].reshape(rows, GROUP_W)
    y_ref[...] = _rms(h_all * jax.nn.gelu(gate), norm_ref[...]).reshape(bt, tc, GROUP_W)

    @pl.when(c == pl.num_programs(1) - 1)
    def _():
        for g in range(groups):
            for j in range(SUBLANES):
                buf_ref[g * SUBLANES + j] = jnp.concatenate(
                    [cbuf[k, pl.ds(g * span + j, CONV_W - 1, stride=SUBLANES), :]
                     for k in range(n_tiles)], axis=-1)


def _lru_call(z, buf0, h0, w, layer, state_layer, bt, tc):
    b, l, _ = z.shape
    rows = bt * tc
    groups = bt // SUBLANES
    ws = functools.partial(_layer_spec, layer)
    cb = RW_COLS // LANES
    n_tiles = GROUP_W // LANES
    tiles = pltpu.VMEM((n_tiles, rows, LANES), F32)
    hist_spec = lambda k: pl.BlockSpec((None, bt, CONV_W - 1, LANES),
                                       lambda i, c: (state_layer, i, 0, k))
    return pl.pallas_call(
        functools.partial(_lru_kernel, bt=bt, tc=tc),
        grid=(b // bt, l // tc),
        in_specs=[_seq_spec(bt, tc, LANES, cb), _seq_spec(bt, tc, LANES, cb + 1),
                  _seq_spec(bt, tc, GROUP_W, cb // n_tiles + 1), hist_spec(0), hist_spec(1),
                  _stacked_state_spec(state_layer, bt, GROUP_W),
                  ws(CONV_W, GROUP_W), ws(1, GROUP_W), ws(GROUP_W, 2 * GROUP_W),
                  ws(1, 2 * GROUP_W), ws(1, GROUP_W), ws(1, GROUP_W)],
        out_specs=[_seq_spec(bt, tc, GROUP_W, 0), _state_spec(bt, CONV_W - 1, GROUP_W),
                   _state_spec(bt, GROUP_W)],
        out_shape=[jax.ShapeDtypeStruct((b, l, GROUP_W), F32),
                   jax.ShapeDtypeStruct((b, CONV_W - 1, GROUP_W), F32),
                   jax.ShapeDtypeStruct((b, GROUP_W), F32)],
        scratch_shapes=[pltpu.VMEM((n_tiles, groups * (CONV_W - 1 + tc) * SUBLANES, LANES), F32),
                        tiles, tiles, pltpu.VMEM((bt, GROUP_W), F32)],
        compiler_params=_params(("parallel", "arbitrary")),
        name="rglru",
    )(z, z, z, buf0, buf0, h0, w["lru_conv_w"], w["lru_conv_b"], w["lru_wax"], w["lru_bax"],
      w["lru_lambda"], w["lru_norm"])


def _s5_kernel(ulo_ref, uhi_ref, re0_ref, im0_ref, ar_ref, ai_ref, bb_ref, cc_ref, d_ref, gw_ref,
               gb_ref, norm_ref, y_ref, re_ref, im_ref, rbuf, ibuf, ybuf, re_scr, im_scr, *,
               bt, tc):
    c = pl.program_id(1)
    n_tiles = S5_STATE // LANES
    groups = bt // SUBLANES

    @pl.when(c == 0)
    def _():
        re_scr[...] = re0_ref[...]
        im_scr[...] = im0_ref[...]

    u = jnp.concatenate([_time_major(ulo_ref, groups, tc), _time_major(uhi_ref, groups, tc)],
                        axis=-1)
    ub = u.astype(BF16)
    slab_shape = (SUBLANES, LANES)
    a_r = [jnp.broadcast_to(t, slab_shape) for t in _lane_tiles(ar_ref[...])]
    a_i = [jnp.broadcast_to(t, slab_shape) for t in _lane_tiles(ai_ref[...])]

    half = n_tiles // 2
    halves = [range(0, half), range(half, n_tiles)]
    cols = lambda ks, base: slice(base + ks[0] * LANES, base + (ks[-1] + 1) * LANES)

    def project_in(ks):
        bu_r = jnp.dot(ub, bb_ref[:, cols(ks, 0)], preferred_element_type=F32)
        bu_i = jnp.dot(ub, bb_ref[:, cols(ks, S5_STATE)], preferred_element_type=F32)
        for k, r_k, i_k in zip(ks, _lane_tiles(bu_r), _lane_tiles(bu_i)):
            rbuf[k] = r_k
            ibuf[k] = i_k

    def recur(ks):
        for g in range(groups):
            seqs = slice(g * SUBLANES, (g + 1) * SUBLANES)
            tile = lambda k: slice(k * LANES, (k + 1) * LANES)
            s_r = {k: re_scr[seqs, tile(k)] for k in ks}
            s_i = {k: im_scr[seqs, tile(k)] for k in ks}
            for t in range(tc):
                slab = pl.ds((g * tc + t) * SUBLANES, SUBLANES)
                for k in ks:
                    n_r = a_r[k] * s_r[k] - a_i[k] * s_i[k] + rbuf[k, slab, :]
                    n_i = a_r[k] * s_i[k] + a_i[k] * s_r[k] + ibuf[k, slab, :]
                    s_r[k], s_i[k] = n_r, n_i
                    rbuf[k, slab, :] = n_r
                    ibuf[k, slab, :] = n_i
            for k in ks:
                re_scr[seqs, tile(k)] = s_r[k]
                im_scr[seqs, tile(k)] = s_i[k]

    def project_out(ks):
        x_r = jnp.concatenate([rbuf[k] for k in ks], axis=-1).astype(BF16)
        x_i = jnp.concatenate([ibuf[k] for k in ks], axis=-1).astype(BF16)
        return (jnp.dot(x_r, cc_ref[cols(ks, 0), :], preferred_element_type=F32)
                + jnp.dot(x_i, cc_ref[cols(ks, S5_STATE), :], preferred_element_type=F32))

    project_in(halves[0])
    project_in(halves[1])
    recur(halves[0])
    y = project_out(halves[0])
    recur(halves[1])
    y = y + project_out(halves[1])
    re_ref[...] = re_scr[...]
    im_ref[...] = im_scr[...]

    y = jax.nn.gelu(y + d_ref[...] * u)
    y = y * jax.nn.sigmoid(_dot(y, gw_ref[...]) + gb_ref[...])
    for k, y_k in enumerate(_lane_tiles(_rms(y, norm_ref[...]))):
        ybuf[k] = y_k
    for g in range(groups):
        for j in range(SUBLANES):
            y_ref[g * SUBLANES + j] = jnp.concatenate(
                [ybuf[k, _seq_rows(g, j, tc), :] for k in range(GROUP_W // LANES)], axis=-1)


def _s5_call(z, re0, im0, w, layer, state_layer, bt, tc):
    b, l, _ = z.shape
    rows = bt * tc
    ws = functools.partial(_layer_spec, layer)
    cb = (RW_COLS + 2 * GROUP_W) // LANES
    state = _state_spec(bt, S5_STATE)
    state_in = _stacked_state_spec(state_layer, bt, S5_STATE)
    tiles = pltpu.VMEM((S5_STATE // LANES, rows, LANES), F32)
    return pl.pallas_call(
        functools.partial(_s5_kernel, bt=bt, tc=tc),
        grid=(b // bt, l // tc),
        in_specs=[_seq_spec(bt, tc, LANES, cb), _seq_spec(bt, tc, LANES, cb + 1),
                  state_in, state_in,
                  ws(1, S5_STATE), ws(1, S5_STATE), ws(GROUP_W, 2 * S5_STATE),
                  ws(2 * S5_STATE, GROUP_W), ws(1, GROUP_W), ws(GROUP_W, GROUP_W),
                  ws(1, GROUP_W), ws(1, GROUP_W)],
        out_specs=[_seq_spec(bt, tc, GROUP_W, 0), state, state],
        out_shape=[jax.ShapeDtypeStruct((b, l, GROUP_W), F32),
                   jax.ShapeDtypeStruct((b, S5_STATE), F32),
                   jax.ShapeDtypeStruct((b, S5_STATE), F32)],
        scratch_shapes=[tiles, tiles, pltpu.VMEM((GROUP_W // LANES, rows, LANES), F32),
                        pltpu.VMEM((bt, S5_STATE), F32), pltpu.VMEM((bt, S5_STATE), F32)],
        compiler_params=_params(("parallel", "arbitrary")),
        name="s5",
    )(z, z, re0, im0, w["s5_ar"], w["s5_ai"], w["s5_bb"], w["s5_cc"], w["s5_d"], w["s5_glu_w"],
      w["s5_glu_b"], w["s5_norm"])


def _seq_cummax(x, bt, tc, buf):
    rows = bt * tc
    rin = lax.broadcasted_iota(jnp.int32, (rows, 1), 0) & (tc - 1)
    d = 1
    while d < tc:
        if d < SUBLANES:
            buf[SUBLANES:SUBLANES + rows, :] = x
            moved = buf[SUBLANES - d:SUBLANES - d + rows, :]
        else:
            moved = jnp.concatenate([x[:d], x[:rows - d]], axis=0)
        x = jnp.maximum(x, jnp.where(rin >= d, moved, -jnp.inf))
        d *= 2
    return x


def _mlstm_kernel(*refs, bt, tc, subs):
    for sub in range(subs):
        _mlstm_chunk(*refs, bt=bt, tc=tc, ts=slice(sub * tc, (sub + 1) * tc),
                     first=sub == 0, final=sub == subs - 1)


def _mlstm_chunk(q_ref, k_ref, v_ref, og_ref, ig_ref, fg_ref, c0_ref, n0_ref, m0_ref, bi_ref,
                 bf_ref, lng_ref, hsum_ref, _c_all_ref, y_ref, c_ref, n_ref, m_ref,
                 c_scr, n_scr, m_scr, gbuf, *, bt, tc, ts, first, final):
    c = pl.program_id(1)
    rows = bt * tc

    if first:
        @pl.when(c == 0)
        def _():
            c_scr[...] = c0_ref[...]
            n_scr[...] = n0_ref[...]
            m_scr[...] = m0_ref[...]
            gbuf[0:SUBLANES, :] = jnp.zeros((SUBLANES, LANES), F32)

    flat = lambda ref: ref[:, ts, :].reshape(rows, ref.shape[-1])
    log_i = flat(ig_ref) + bi_ref[...]
    log_f = jax.nn.log_sigmoid(flat(fg_ref) + bf_ref[...])
    _, incl = _seq_masks(bt, tc)
    bcum = _masked_sum(incl, log_f)
    m3 = m_scr[...]
    bcum3 = bcum.reshape(bt, tc, LANES)
    b_last3 = bcum3[:, tc - 1:tc, :]
    inter_all = (bcum3 + m3).reshape(rows, LANES)
    wk3 = b_last3 - bcum3 + log_i.reshape(bt, tc, LANES)
    m_new3 = jnp.maximum(b_last3 + m3, jnp.max(wk3, axis=1, keepdims=True))
    sc_all = jnp.exp(wk3 - m_new3).reshape(rows, LANES)
    decay3 = jnp.exp(b_last3 + m3 - m_new3)
    m_scr[...] = m_new3
    m_ref[...] = m_new3

    src = log_i - bcum
    mt_all = jnp.maximum(bcum + _seq_cummax(src, bt, tc, gbuf), inter_all)
    col_all = bcum - mt_all
    ei_all = jnp.exp(inter_all - mt_all)
    floor_all = jnp.exp(-mt_all)
    eye = (lax.broadcasted_iota(jnp.int32, (SUBLANES, LANES), 0)
           == lax.broadcasted_iota(jnp.int32, (SUBLANES, LANES), 1)).astype(F32)
    src_rows = _dot_nt(eye, src, HI)

    q, k, v = flat(q_ref), flat(k_ref) * (HEAD_DIM ** -0.5), flat(v_ref)
    ones = jnp.ones((rows, HEAD_DIM), F32)
    pad = jnp.zeros((HEAD_DIM - SUBLANES, HEAD_DIM), F32)
    heads = range(HEADS)
    lanes = [slice(h * HEAD_DIM, (h + 1) * HEAD_DIM) for h in heads]
    cols = [slice(h, h + 1) for h in heads]
    seqs = [slice(b * tc, (b + 1) * tc) for b in range(bt)]
    q_h, k_h, v_h = ([x[:, sl] for sl in lanes] for x in (q, k, v))
    pm = [jnp.where(incl, jnp.exp(col_all[:, cols[h]] + src_rows[cols[h], :]), 0.0)
          * _dot_nt(q_h[h].astype(BF16), k_h[h].astype(BF16)) for h in heads]
    intra = [_dot(pm[h], jnp.concatenate([v_h[h], ones], axis=-1)) for h in heads]
    c_old = [[c_scr[b, h] for b in range(bt)] for h in heads]
    n_old = [[jnp.broadcast_to(n_scr[b, cols[h], :], (SUBLANES, HEAD_DIM)) for b in range(bt)]
             for h in heads]
    from_state = [jnp.concatenate(
        [_dot_nt(q_h[h][rs].astype(BF16),
                 jnp.concatenate([c_old[h][b], n_old[h][b], pad], axis=0).astype(BF16))
         for b, rs in enumerate(seqs)], axis=0) for h in heads]
    nd = [intra[h] + ei_all[:, cols[h]] * from_state[h] for h in heads]
    o = jnp.concatenate(
        [nd[h][:, :HEAD_DIM] / jnp.maximum(jnp.abs(nd[h][:, HEAD_DIM:HEAD_DIM + 1]),
                                           floor_all[:, cols[h]]) for h in heads], axis=-1)
    for h in heads:
        sc = sc_all[:, cols[h]]
        scv = sc * v_h[h]
        for b, rs in enumerate(seqs):
            c_scr[b, h] = (decay3[b, :, cols[h]] * c_old[h][b]
                           + _dot_tn(scv[rs].astype(BF16), k_h[h][rs].astype(BF16)))
        sck = (sc * k_h[h]).reshape(bt, tc, HEAD_DIM)
        n_scr[:, cols[h], :] = (decay3[:, :, cols[h]] * n_scr[:, cols[h], :]
                                + jnp.sum(sck, axis=1, keepdims=True))
    hsum = hsum_ref[...]
    cen = o - _dot(o, hsum) * (1.0 / HEAD_DIM)
    var = _dot(cen * cen, hsum) * (1.0 / HEAD_DIM)
    y = cen * lax.rsqrt(var + EPS) * lng_ref[...] * jax.nn.sigmoid(flat(og_ref))
    y_ref[:, ts, :] = y.reshape(bt, tc, GROUP_W)

    if final:
        @pl.when(c == pl.num_programs(1) - 1)
        def _():
            c_ref[...] = c_scr[...]
            n_ref[...] = n_scr[...]


def _mlstm_call(z, c0, n0, m0, c_all, w, layer, state_layer, bt, tc):
    b, l, _ = z.shape
    ws = functools.partial(_layer_spec, layer)
    cb = (RW_COLS + 3 * GROUP_W) // GROUP_W
    mat = (HEADS, HEAD_DIM, HEAD_DIM)
    stacked = functools.partial(_stacked_state_spec, state_layer, bt)
    subs = 2 if l % (2 * tc) == 0 else 1
    tb = tc * subs
    return pl.pallas_call(
        functools.partial(_mlstm_kernel, bt=bt, tc=tc, subs=subs),
        grid=(b // bt, l // tb),
        in_specs=[_seq_spec(bt, tb, GROUP_W, cb), _seq_spec(bt, tb, GROUP_W, cb + 1),
                  _seq_spec(bt, tb, GROUP_W, cb + 2), _seq_spec(bt, tb, GROUP_W, cb + 3),
                  _seq_spec(bt, tb, LANES, IGATE_COL_BLOCK),
                  _seq_spec(bt, tb, LANES, IGATE_COL_BLOCK + 1),
                  stacked(*mat), stacked(HEADS, HEAD_DIM), stacked(1, LANES),
                  ws(1, LANES), ws(1, LANES), ws(1, GROUP_W),
                  _const_spec((GROUP_W, GROUP_W)), ANY_SPEC],
        out_specs=[_seq_spec(bt, tb, GROUP_W, 0), _stacked_state_spec(layer, bt, *mat),
                   _state_spec(bt, HEADS, HEAD_DIM), _state_spec(bt, 1, LANES)],
        out_shape=[jax.ShapeDtypeStruct((b, l, GROUP_W), F32),
                   jax.ShapeDtypeStruct(c_all.shape, F32),
                   jax.ShapeDtypeStruct((b, HEADS, HEAD_DIM), F32),
                   jax.ShapeDtypeStruct((b, 1, LANES), F32)],
        input_output_aliases={13: 1},
        scratch_shapes=[pltpu.VMEM((bt,) + mat, F32),
                        pltpu.VMEM((bt, HEADS, HEAD_DIM), F32),
                        pltpu.VMEM((bt, 1, LANES), F32),
                        pltpu.VMEM((SUBLANES + bt * tc, LANES), F32)],
        compiler_params=_params(("parallel", "arbitrary")),
        name="mlstm",
    )(z, z, z, z, z, z, c0, n0, m0, w["ml_bi"], w["ml_bf"], w["ml_ln_g"], w["head_sum"], c_all)


def _block_diag(blocks):
    *lead, n, r, c = blocks.shape
    eye = jnp.eye(n, dtype=blocks.dtype)
    out = eye[:, None, :, None] * blocks[..., :, :, None, :]
    return out.reshape(*lead, n * r, n * c)


def _prep(w):
    depth = w["ffn1_pre"].shape[0]
    row = lambda t: t.reshape(depth, 1, -1).astype(F32)
    out = {}
    for f in ("ffn1", "ffn2"):
        out[f + "_pre"] = row(w[f + "_pre"])
        out[f + "_post"] = row(w[f + "_post"])
        out[f + "_wg"] = _cast_t_call(jnp.swapaxes(w[f + "_wg"], 1, 2), D_FF_PAD,
                                      _row_chunks(D_FF))
        out[f + "_wu"] = _cast_t_call(jnp.swapaxes(w[f + "_wu"], 1, 2), D_FF_PAD,
                                      _row_chunks(D_FF))
        out[f + "_wd"] = _cast_call(w[f + "_wd"], D_FF_PAD, D_MODEL)
    out["mix_pre"] = row(w["mix_pre"])
    out["mix_post"] = row(w["mix_post"])
    out["w_in"] = _cast_t_call(
        jnp.swapaxes(w["w_in"], 1, 2), IN_COLS_PAD,
        _row_chunks(MAIN_COLS) + ((MAIN_COLS, HEADS, MAIN_COLS),
                                  (MAIN_COLS + HEADS, HEADS, MAIN_COLS + LANES)))
    out["w_out"] = _cast_call(w["w_out"], D_MODEL, D_MODEL)

    for name in ("rw_mu", "rw_w0", "rw_a0", "rw_kk", "rw_ka", "rw_rk", "rw_ln_g", "rw_ln_b"):
        out[name] = row(w[name])
    for name in ("rw_w2", "rw_a2", "rw_g2"):
        out[name] = w[name].astype(BF16)
    out["head_sum"] = _block_diag(jnp.ones((HEADS, HEAD_DIM, HEAD_DIM), BF16))

    out["lru_conv_w"] = w["lru_conv_w"].astype(F32)
    out["lru_conv_b"] = row(w["lru_conv_b"])
    out["lru_wax"] = jnp.concatenate(
        [_block_diag(w["lru_wa"]), _block_diag(w["lru_wx"])], axis=-1).astype(BF16)
    out["lru_bax"] = jnp.concatenate([row(w["lru_ba"]), row(w["lru_bx"])], axis=-1)
    out["lru_lambda"] = row(w["lru_lambda"])
    out["lru_norm"] = row(w["lru_norm"])

    lr, li = w["s5_a_re"].astype(F32), w["s5_a_im"].astype(F32)
    dt = jnp.exp(w["s5_log_dt"].astype(F32))[..., None]
    mag = jnp.exp(lr * dt)
    ar, ai = mag * jnp.cos(li * dt), mag * jnp.sin(li * dt)
    den = lr * lr + li * li
    zr = ((ar - 1.0) * lr + ai * li) / den
    zi = (ai * lr - (ar - 1.0) * li) / den
    b_re, b_im = w["s5_b_re"].astype(F32), w["s5_b_im"].astype(F32)
    bbr = zr[..., None] * b_re - zi[..., None] * b_im
    bbi = zr[..., None] * b_im + zi[..., None] * b_re
    packed = lambda t: _block_diag(jnp.swapaxes(t, -1, -2))
    out["s5_ar"], out["s5_ai"] = row(ar), row(ai)
    out["s5_bb"] = jnp.concatenate([packed(bbr), packed(bbi)], axis=-1).astype(BF16)
    out["s5_cc"] = jnp.concatenate(
        [packed(w["s5_c_re"]), -packed(w["s5_c_im"])], axis=-2).astype(BF16)
    out["s5_d"] = row(w["s5_d"])
    out["s5_glu_w"] = w["s5_glu_w"].astype(BF16)
    out["s5_glu_b"] = row(w["s5_glu_b"])
    out["s5_norm"] = row(w["s5_norm"])

    lane_block = lambda t: jnp.pad(t.astype(F32), ((0, 0), (0, LANES - HEADS))).reshape(
        depth, 1, LANES)
    out["ml_bi"] = lane_block(w["ml_bi"])
    out["ml_bf"] = lane_block(w["ml_bf"])
    out["ml_ln_g"] = row(w["ml_ln_g"])
    return out


def _tiling(b, l, rows_cap, tc_cap, seq_multiple=1):
    tc = tc_cap
    while tc > SUBLANES and l % tc:
        tc //= 2
    if l % tc or b % seq_multiple:
        raise ValueError(f"unsupported batch {b} / sequence length {l}")
    bt = seq_multiple
    while bt * 2 * tc <= rows_cap and b % (bt * 2) == 0:
        bt *= 2
    return bt, tc


def _stack_states(states):
    shift, wkv, conv, h, re, im, c, n, m = (t.astype(F32) for t in states)
    layers, b = shift.shape[:2]
    return (shift.reshape(layers, b, 1, RW_COLS), wkv, conv, h,
            re.reshape(layers, b, S5_STATE), im.reshape(layers, b, S5_STATE), c, n,
            jnp.pad(m, ((0, 0), (0, 0), (0, LANES - HEADS))).reshape(layers, b, 1, LANES))


def _zero_states(b):
    mat = (1, b, HEADS, HEAD_DIM, HEAD_DIM)
    return _stack_states((
        jnp.zeros((1, b, RW_COLS), F32), jnp.zeros(mat, F32),
        jnp.zeros((1, b, CONV_W - 1, GROUP_W), F32), jnp.zeros((1, b, GROUP_W), F32),
        jnp.zeros((1, b, S5_GROUPS, S5_P), F32), jnp.zeros((1, b, S5_GROUPS, S5_P), F32),
        jnp.zeros(mat, F32), jnp.zeros((1, b, HEADS, HEAD_DIM), F32),
        jnp.zeros((1, b, HEADS), F32)))


def _layer(x, st, state_layer, mats, w, layer):
    b, l, d = x.shape
    n = b * l
    shift0, s0, buf0, h0, re0, im0, c0, n0, m0 = st
    s_all, c_all = mats
    x1, z = _ffn_in_call(x.reshape(n, d), w, layer)
    z = z.reshape(b, l, IN_COLS_PAD)

    y_rw, n_shift, s_all = _rwkv_call(z, shift0, s0, s_all, w, layer, state_layer,
                                      *_tiling(b, l, 256, 32))
    y_lru, n_buf, n_h = _lru_call(z, buf0, h0, w, layer, state_layer,
                                  *_tiling(b, l, 1024, 128, SUBLANES))
    y_s5, n_re, n_im = _s5_call(z, re0, im0, w, layer, state_layer,
                                *_tiling(b, l, 1024, 128, SUBLANES))
    y_ml, c_all, n_n, n_m = _mlstm_call(z, c0, n0, m0, c_all, w, layer, state_layer,
                                        *_tiling(b, l, 256, 128))

    flat = lambda t: t.reshape(n, GROUP_W)
    x3 = _out_ffn_call(x1, flat(y_rw), flat(y_lru), flat(y_s5), flat(y_ml), w, layer)
    new = (n_shift.reshape(b, RW_COLS), n_buf, n_h,
           n_re.reshape(b, S5_GROUPS, S5_P), n_im.reshape(b, S5_GROUPS, S5_P), n_n,
           n_m.reshape(b, LANES)[:, :HEADS])
    return x3.reshape(b, l, d), new, (s_all, c_all)


def kernel(x_prompt, x_sample, state_rwkv_shift, state_rwkv_wkv, state_lru_conv, state_lru_h, state_s5_re, state_s5_im, state_mlstm_C, state_mlstm_n, state_mlstm_m, ffn1_pre, ffn1_wg, ffn1_wu, ffn1_wd, ffn1_post, mix_pre, w_in, w_out, mix_post, ffn2_pre, ffn2_wg, ffn2_wu, ffn2_wd, ffn2_post, rw_mu, rw_w0, rw_w2, rw_a0, rw_a2, rw_g2, rw_kk, rw_ka, rw_rk, rw_ln_g, rw_ln_b, lru_conv_w, lru_conv_b, lru_wa, lru_ba, lru_wx, lru_bx, lru_lambda, lru_norm, s5_a_re, s5_a_im, s5_log_dt, s5_b_re, s5_b_im, s5_c_re, s5_c_im, s5_d, s5_glu_w, s5_glu_b, s5_norm, ml_bi, ml_bf, ml_ln_g):
    w = _prep(dict(
        ffn1_pre=ffn1_pre, ffn1_wg=ffn1_wg, ffn1_wu=ffn1_wu, ffn1_wd=ffn1_wd, ffn1_post=ffn1_post,
        mix_pre=mix_pre, w_in=w_in, w_out=w_out, mix_post=mix_post,
        ffn2_pre=ffn2_pre, ffn2_wg=ffn2_wg, ffn2_wu=ffn2_wu, ffn2_wd=ffn2_wd, ffn2_post=ffn2_post,
        rw_mu=rw_mu, rw_w0=rw_w0, rw_w2=rw_w2, rw_a0=rw_a0, rw_a2=rw_a2, rw_g2=rw_g2,
        rw_kk=rw_kk, rw_ka=rw_ka, rw_rk=rw_rk, rw_ln_g=rw_ln_g, rw_ln_b=rw_ln_b,
        lru_conv_w=lru_conv_w, lru_conv_b=lru_conv_b, lru_wa=lru_wa, lru_ba=lru_ba,
        lru_wx=lru_wx, lru_bx=lru_bx, lru_lambda=lru_lambda, lru_norm=lru_norm,
        s5_a_re=s5_a_re, s5_a_im=s5_a_im, s5_log_dt=s5_log_dt, s5_b_re=s5_b_re, s5_b_im=s5_b_im,
        s5_c_re=s5_c_re, s5_c_im=s5_c_im, s5_d=s5_d, s5_glu_w=s5_glu_w, s5_glu_b=s5_glu_b,
        s5_norm=s5_norm, ml_bi=ml_bi, ml_bf=ml_bf, ml_ln_g=ml_ln_g))
    sample_state = _stack_states((state_rwkv_shift, state_rwkv_wkv, state_lru_conv, state_lru_h,
                                  state_s5_re, state_s5_im, state_mlstm_C, state_mlstm_n,
                                  state_mlstm_m))
    depth = ffn1_pre.shape[0]
    b_p = x_prompt.shape[0]
    prompt_state = _zero_states(b_p)
    mat_zeros = lambda b: jnp.zeros((depth, b, HEADS, HEAD_DIM, HEAD_DIM), F32)
    mats_p = (mat_zeros(b_p), mat_zeros(b_p))
    mats_s = (mat_zeros(x_sample.shape[0]), mat_zeros(x_sample.shape[0]))
    y_p, y_s = x_prompt.astype(F32), x_sample.astype(F32)
    outs_p, outs_s = [], []
    for layer in range(depth):
        y_p, st_p, mats_p = _layer(y_p, prompt_state, 0, mats_p, w, layer)
        y_s, st_s, mats_s = _layer(y_s, sample_state, layer, mats_s, w, layer)
        outs_p.append(st_p)
        outs_s.append(st_s)

    def gather(outs, mats):
        shift, conv, h, re, im, n, m = (jnp.stack([o[i] for o in outs]) for i in range(7))
        return (shift, mats[0], conv, h, re, im, mats[1], n, m)

    return (y_p, y_s, *gather(outs_p, mats_p), *gather(outs_s, mats_s))
```
